```python
import jax
import jax.numpy as jnp
from jax import lax
import numpy as np

D_MODEL = 1024
BATCH = 16
SEQ = 4096
DEPTH = 1

PLE_DIM = 256
HEAD_DIM = 64
NSA_WIDTH = D_MODEL // 2
RWKV_WIDTH = D_MODEL - NSA_WIDTH
NSA_HEADS = NSA_WIDTH // HEAD_DIM
NSA_KV_HEADS = 2
NSA_GROUP = NSA_HEADS // NSA_KV_HEADS
NSA_KV_WIDTH = NSA_KV_HEADS * HEAD_DIM
CMP_LEN = 32
CMP_STRIDE = 16
CMP_HIDDEN = 2 * HEAD_DIM
SEL_BLOCK = 64
SEL_TOPK = 16
WINDOW = 512
Q_BLOCK = 32
RWKV_HEADS = RWKV_WIDTH // HEAD_DIM
DECAY_LORA = 64
ICLR_LORA = 64
GATE_LORA = 128
D_FF = 4 * D_MODEL
NORM_EPS = 1e-6
GN_EPS = 64e-5
NEG_INF = -1e30
FORCE_SCORE = 1e4
NSA_SIZES = (NSA_WIDTH,) + (NSA_KV_WIDTH,) * 6 + (3 * NSA_HEADS,)
RWKV_SIZES = (RWKV_WIDTH,) * 3 + (DECAY_LORA, ICLR_LORA, GATE_LORA)
NSA_COLS = sum(NSA_SIZES)
RWKV_COLS = sum(RWKV_SIZES)
IN_COLS = NSA_COLS + RWKV_COLS

kernel_name = "hymba_nsa_rwkv7_sandwich_ple"


def rmsnorm(x, g):
    xf = x.astype(jnp.float32)
    y = xf * lax.rsqrt(jnp.mean(xf * xf, axis=-1, keepdims=True) + NORM_EPS)
    return (y * g.astype(jnp.float32)).astype(x.dtype)


def split_cols(z, sizes):
    return jnp.split(z, np.cumsum(sizes)[:-1].tolist(), axis=-1)


def alibi_slopes(n):
    return jnp.asarray([2.0 ** (-8.0 * (h + 1) / n) for h in range(n)], jnp.float32)


def masked_softmax(s, mask):
    p = jax.nn.softmax(jnp.where(mask, s.astype(jnp.float32), NEG_INF), axis=-1)
    return p * jnp.any(mask, axis=-1, keepdims=True)


def compress_blocks(kv, pe, w1, b1, w2):
    S = kv.shape[2]
    n_cmp = (S - CMP_LEN) // CMP_STRIDE + 1
    idx = jnp.arange(n_cmp)[:, None] * CMP_STRIDE + jnp.arange(CMP_LEN)[None, :]
    blocks = kv[:, :, idx] + pe
    flat = blocks.reshape(blocks.shape[:3] + (CMP_LEN * HEAD_DIM,))
    return jax.nn.gelu(flat @ w1 + b1) @ w2


def cmp_to_sel_map(n_cmp, n_sel):
    c0 = jnp.arange(n_cmp) * CMP_STRIDE
    s0 = jnp.arange(n_sel) * SEL_BLOCK
    ov = (jnp.minimum(c0[:, None] + CMP_LEN - 1, s0[None, :] + SEL_BLOCK - 1)
          - jnp.maximum(c0[:, None], s0[None, :]) + 1)
    return jnp.clip(ov, 0).astype(jnp.float32) / CMP_STRIDE


def nsa_mixer(q, k_cmp, v_cmp, k_slc, v_slc, k_win, v_win, gates, cmp_k_params, cmp_v_params):
    B, S, _ = q.shape
    G, R = NSA_KV_HEADS, NSA_GROUP
    f32 = jnp.float32
    qh = q.reshape(B, S, G, R, HEAD_DIM).transpose(0, 2, 3, 1, 4) * (HEAD_DIM ** -0.5)

    def kv_heads(t):
        return t.reshape(B, S, G, HEAD_DIM).transpose(0, 2, 1, 3)

    kc = compress_blocks(kv_heads(k_cmp), *cmp_k_params)
    vc = compress_blocks(kv_heads(v_cmp), *cmp_v_params)
    n_cmp = kc.shape[2]
    n_sel = S // SEL_BLOCK
    top_k = min(SEL_TOPK, n_sel)
    cmp_end = jnp.arange(n_cmp) * CMP_STRIDE + (CMP_LEN - 1)
    sel_map = cmp_to_sel_map(n_cmp, n_sel)
    ks_blk = kv_heads(k_slc).reshape(B, G, n_sel, SEL_BLOCK, HEAD_DIM)
    vs_blk = kv_heads(v_slc).reshape(B, G, n_sel, SEL_BLOCK, HEAD_DIM)
    pad = ((0, 0), (0, 0), (WINDOW, 0), (0, 0))
    kw_pad = jnp.pad(kv_heads(k_win), pad)
    vw_pad = jnp.pad(kv_heads(v_win), pad)
    gates = gates.reshape(B, S, G, R, 3).transpose(0, 2, 3, 1, 4)
    slopes = alibi_slopes(NSA_HEADS).reshape(G, R, 1, 1)
    b_idx = jnp.arange(B)[:, None, None, None]
    g_idx = jnp.arange(G)[None, :, None, None]
    sel_ids = jnp.arange(n_sel)

    def query_block(c0):
        t = c0 + jnp.arange(Q_BLOCK)
        qb = lax.dynamic_slice_in_dim(qh, c0, Q_BLOCK, axis=3)
        dist = (t[:, None] - cmp_end[None, :]).astype(f32)
        s = jnp.einsum('bgrqd,bgnd->bgrqn', qb, kc).astype(f32) - slopes * dist
        p_cmp = masked_softmax(s, dist >= 0)
        o_cmp = jnp.einsum('bgrqn,bgnd->bgrqd', p_cmp.astype(vc.dtype), vc)
        imp = jnp.einsum('bgrqn,nj->bgqj', p_cmp, sel_map)
        cur = (t // SEL_BLOCK)[:, None]
        forced = (sel_ids == 0) | (sel_ids == cur) | (sel_ids == cur - 1)
        score = jnp.where(forced, FORCE_SCORE, jnp.where(sel_ids <= cur, imp, -1.0))
        _, idx = lax.top_k(score, top_k)
        k_sel = ks_blk[b_idx, g_idx, idx]
        v_sel = vs_blk[b_idx, g_idx, idx]
        pos = idx[..., None] * SEL_BLOCK + jnp.arange(SEL_BLOCK)
        dist = (t[:, None, None] - pos)[:, :, None].astype(f32)
        s = jnp.einsum('bgrqd,bgqkld->bgrqkl', qb, k_sel).astype(f32) - slopes[..., None] * dist
        n_keys = top_k * SEL_BLOCK
        p_sel = masked_softmax(s.reshape(B, G, R, Q_BLOCK, n_keys),
                               (dist >= 0).reshape(B, G, 1, Q_BLOCK, n_keys))
        o_sel = jnp.einsum('bgrqkl,bgqkld->bgrqd', p_sel.reshape(s.shape).astype(v_sel.dtype), v_sel)
        kpos = c0 - WINDOW + jnp.arange(WINDOW + Q_BLOCK)
        kw = lax.dynamic_slice_in_dim(kw_pad, c0, WINDOW + Q_BLOCK, axis=2)
        vw = lax.dynamic_slice_in_dim(vw_pad, c0, WINDOW + Q_BLOCK, axis=2)
        dist = t[:, None] - kpos[None, :]
        mask = (kpos >= 0)[None, :] & (dist >= 0) & (dist < WINDOW)
        s = jnp.einsum('bgrqd,bgnd->bgrqn', qb, kw).astype(f32) - slopes * dist.astype(f32)
        p_win = masked_softmax(s, mask)
        o_win = jnp.einsum('bgrqn,bgnd->bgrqd', p_win.astype(vw.dtype), vw)
        gb = lax.dynamic_slice_in_dim(gates, c0, Q_BLOCK, axis=3)
        return gb[..., 0:1] * o_cmp + gb[..., 1:2] * o_sel + gb[..., 2:3] * o_win

    out = lax.map(query_block, jnp.arange(S // Q_BLOCK) * Q_BLOCK)
    return out.transpose(1, 0, 4, 2, 3, 5).reshape(B, S, NSA_WIDTH)


def wkv7_scan(r, w, k, v, a, b):
    B, S, H, N = r.shape
    xs = tuple(t.astype(jnp.float32).transpose(1, 0, 2, 3) for t in (r, w, k, v, a, b))

    def step(state, inp):
        r_t, w_t, k_t, v_t, a_t, b_t = inp
        sa = jnp.einsum('bhvk,bhk->bhv', state, a_t)
        state = (state * w_t[:, :, None, :] + sa[..., None] * b_t[:, :, None, :]
                 + v_t[..., None] * k_t[:, :, None, :])
        return state, jnp.einsum('bhvk,bhk->bhv', state, r_t)

    _, y = lax.scan(step, jnp.zeros((B, H, N, N), jnp.float32), xs)
    return y.transpose(1, 0, 2, 3)


def rwkv7_mixer(z, shift_mu, w0, w_lora_up, a0, a_lora_up, g_lora_up, k_k, k_a, r_k, lnx_w, lnx_b):
    B, S, _ = z.shape
    f32 = jnp.float32
    z_prev = jnp.pad(z, ((0, 0), (1, 0), (0, 0)))[:, :-1]
    z = z + (z_prev - z) * shift_mu
    r, k, v, wd, ad, gd = split_cols(z, RWKV_SIZES)
    w = -jax.nn.softplus(-(w0 + jnp.tanh(wd) @ w_lora_up)) - 0.5
    a = jax.nn.sigmoid(a0 + ad @ a_lora_up)
    g = jax.nn.sigmoid(gd) @ g_lora_up

    def hd(t):
        return t.reshape(B, S, RWKV_HEADS, HEAD_DIM)

    kk = hd(k * k_k).astype(f32)
    kk = kk * lax.rsqrt(jnp.maximum(jnp.sum(kk * kk, axis=-1, keepdims=True), 1e-24))
    k = k * (1.0 + (a - 1.0) * k_a)
    r, k, v, a = hd(r), hd(k), hd(v), hd(a)
    decay = jnp.exp(-jnp.exp(hd(w).astype(f32)))
    y = wkv7_scan(r, decay, k, v, -kk, kk * a)
    mu = jnp.mean(y, axis=-1, keepdims=True)
    var = jnp.mean(jnp.square(y - mu), axis=-1, keepdims=True)
    y = ((y - mu) * lax.rsqrt(var + GN_EPS) * lnx_w.reshape(RWKV_HEADS, HEAD_DIM)
         + lnx_b.reshape(RWKV_HEADS, HEAD_DIM))
    y = y + jnp.sum(r * k * r_k, axis=-1, keepdims=True) * v
    return (y.reshape(B, S, RWKV_WIDTH) * g).astype(z.dtype)


def hybrid_layer(x, p_l, g_mix_pre, g_mix_post, g_mlp_pre, g_mlp_post, w_in, nsa_gate_bias,
                 cmp_pe_k, cmp_k_w1, cmp_k_b1, cmp_k_w2, cmp_pe_v, cmp_v_w1, cmp_v_b1, cmp_v_w2,
                 shift_mu, w0, w_lora_up, a0, a_lora_up, g_lora_up, k_k, k_a, r_k, lnx_w, lnx_b,
                 w_out, w_up, w_down, w_ple, w_ple_gate):
    h = rmsnorm(x, g_mix_pre)
    z = h @ w_in
    q, kc, vc, ks, vs, kw, vw, gate_logits = split_cols(z[..., :NSA_COLS], NSA_SIZES)
    y_nsa = nsa_mixer(q, kc, vc, ks, vs, kw, vw, jax.nn.sigmoid(gate_logits + nsa_gate_bias),
                      (cmp_pe_k, cmp_k_w1, cmp_k_b1, cmp_k_w2), (cmp_pe_v, cmp_v_w1, cmp_v_b1, cmp_v_w2))
    y_rwkv = rwkv7_mixer(z[..., NSA_COLS:], shift_mu, w0, w_lora_up, a0, a_lora_up, g_lora_up,
                         k_k, k_a, r_k, lnx_w, lnx_b)
    mix = jnp.concatenate([y_nsa, y_rwkv], axis=-1) @ w_out
    x = x + rmsnorm(mix, g_mix_post)
    h = rmsnorm(x, g_mlp_pre)
    f = jnp.square(jax.nn.relu(h @ w_up)) @ w_down
    x = x + rmsnorm(f, g_mlp_post)
    return x + jax.nn.sigmoid(x @ w_ple_gate) * (p_l @ w_ple)


def setup_inputs(seed: int = 0) -> dict:
    key = jax.random.key(seed)
    keys = iter(jax.random.split(key, 40))
    L = DEPTH

    def nrm(shape, scale):
        return jax.random.normal(next(keys), shape, jnp.float32) * scale

    def gain(shape):
        return 1.0 + nrm(shape, 0.05)

    return {
        "x": nrm((BATCH, SEQ, D_MODEL), 1.0),
        "p": nrm((L, BATCH, SEQ, PLE_DIM), 1.0),
        "g_mix_pre": gain((L, D_MODEL)),
        "g_mix_post": gain((L, D_MODEL)),
        "g_mlp_pre": gain((L, D_MODEL)),
        "g_mlp_post": gain((L, D_MODEL)),
        "w_in": nrm((L, D_MODEL, IN_COLS), D_MODEL ** -0.5),
        "nsa_gate_bias": nrm((L, 3 * NSA_HEADS), 0.1),
        "cmp_pe_k": nrm((L, CMP_LEN, HEAD_DIM), 0.1),
        "cmp_k_w1": nrm((L, CMP_LEN * HEAD_DIM, CMP_HIDDEN), (CMP_LEN * HEAD_DIM) ** -0.5),
        "cmp_k_b1": nrm((L, CMP_HIDDEN), 0.02),
        "cmp_k_w2": nrm((L, CMP_HIDDEN, HEAD_DIM), CMP_HIDDEN ** -0.5),
        "cmp_pe_v": nrm((L, CMP_LEN, HEAD_DIM), 0.1),
        "cmp_v_w1": nrm((L, CMP_LEN * HEAD_DIM, CMP_HIDDEN), (CMP_LEN * HEAD_DIM) ** -0.5),
        "cmp_v_b1": nrm((L, CMP_HIDDEN), 0.02),
        "cmp_v_w2": nrm((L, CMP_HIDDEN, HEAD_DIM), CMP_HIDDEN ** -0.5),
        "shift_mu": jax.random.uniform(next(keys), (L, RWKV_COLS), jnp.float32),
        "w0": nrm((L, RWKV_WIDTH), 0.5),
        "w_lora_up": nrm((L, DECAY_LORA, RWKV_WIDTH), 0.5 * DECAY_LORA ** -0.5),
        "a0": nrm((L, RWKV_WIDTH), 0.5),
        "a_lora_up": nrm((L, ICLR_LORA, RWKV_WIDTH), ICLR_LORA ** -0.5),
        "g_lora_up": nrm((L, GATE_LORA, RWKV_WIDTH), GATE_LORA ** -0.5),
        "k_k": 0.85 + nrm((L, RWKV_WIDTH), 0.05),
        "k_a": gain((L, RWKV_WIDTH)),
        "r_k": nrm((L, RWKV_HEADS, HEAD_DIM), 0.1),
        "lnx_w": gain((L, RWKV_WIDTH)),
        "lnx_b": nrm((L, RWKV_WIDTH), 0.02),
        "w_out": nrm((L, D_MODEL, D_MODEL), D_MODEL ** -0.5),
        "w_up": nrm((L, D_MODEL, D_FF), D_MODEL ** -0.5),
        "w_down": nrm((L, D_FF, D_MODEL), D_FF ** -0.5),
        "w_ple": nrm((L, PLE_DIM, D_MODEL), PLE_DIM ** -0.5),
        "w_ple_gate": nrm((L, D_MODEL, D_MODEL), D_MODEL ** -0.5),
    }


def reference(x, p, g_mix_pre, g_mix_post, g_mlp_pre, g_mlp_post, w_in, nsa_gate_bias,
              cmp_pe_k, cmp_k_w1, cmp_k_b1, cmp_k_w2, cmp_pe_v, cmp_v_w1, cmp_v_b1, cmp_v_w2,
              shift_mu, w0, w_lora_up, a0, a_lora_up, g_lora_up, k_k, k_a, r_k, lnx_w, lnx_b,
              w_out, w_up, w_down, w_ple, w_ple_gate):
    for i in range(DEPTH):
        x = hybrid_layer(x, p[i], g_mix_pre[i], g_mix_post[i], g_mlp_pre[i], g_mlp_post[i], w_in[i],
                         nsa_gate_bias[i], cmp_pe_k[i], cmp_k_w1[i], cmp_k_b1[i], cmp_k_w2[i],
                         cmp_pe_v[i], cmp_v_w1[i], cmp_v_b1[i], cmp_v_w2[i], shift_mu[i], w0[i],
                         w_lora_up[i], a0[i], a_lora_up[i], g_lora_up[i], k_k[i], k_a[i], r_k[i],
                         lnx_w[i], lnx_b[i], w_out[i], w_up[i], w_down[i], w_ple[i], w_ple_gate[i])
    return x
```

```python
import functools

import jax
import jax.numpy as jnp
import numpy as np
from jax import lax
from jax.experimental import pallas as pl
from jax.experimental.pallas import tpu as pltpu

f32 = jnp.float32
bf16 = jnp.bfloat16

D_MODEL = 1024
HEAD_DIM = 64
NSA_HEADS = 8
NSA_KV_HEADS = 2
NSA_GROUP = NSA_HEADS // NSA_KV_HEADS
CMP_LEN = 32
CMP_STRIDE = 16
CMP_HIDDEN = 2 * HEAD_DIM
SEL_BLOCK = 64
SEL_TOPK = 16
WINDOW = 512
RWKV_WIDTH = 512
RWKV_COLS = 1792
DECAY_LORA = 64
ICLR_LORA = 64
GATE_LORA = 128
D_FF = 4 * D_MODEL
PLE_DIM = 256
NORM_EPS = 1e-6
GN_EPS = 64e-5
NEG_INF = -1e30
FORCE_SCORE = 1e4

PAIR = 2 * HEAD_DIM
CHUNK = 64
VMEM_LIMIT = 56 * 1024 * 1024

_Q0, _KV0, _GATE0, _RW0, _WCOLS = 0, 1024, 1792, 1920, 3712


def _dot(a, b):
    return jnp.dot(a.astype(bf16), b.astype(bf16), preferred_element_type=f32)


def _dot_nt(a, b):
    return lax.dot_general(a.astype(bf16), b.astype(bf16), (((1,), (1,)), ((), ())), preferred_element_type=f32)


def _dot_tn(a, b):
    return lax.dot_general(a.astype(bf16), b.astype(bf16), (((0,), (0,)), ((), ())), preferred_element_type=f32)


def _split_dot(x, w):
    hi = x.astype(bf16)
    lo = (x - hi.astype(f32)).astype(bf16)
    return jnp.dot(hi, w, preferred_element_type=f32) + jnp.dot(lo, w, preferred_element_type=f32)


def _rms(x, g):
    ms = jnp.mean(x * x, axis=-1, keepdims=True)
    return x * lax.rsqrt(ms + NORM_EPS) * g


def _sigmoid(x):
    return 1.0 / (1.0 + jnp.exp(-x))


def _inproj_body(x_ref, g_ref, w_ref, gb_ref, q_ref, kc_ref, vc_ref, ks_ref, vs_ref, kw_ref, vw_ref,
                 gate_ref, z_ref):
    h = _rms(x_ref[0], g_ref[...]).astype(bf16)
    q = jnp.dot(h, w_ref[:, _Q0:_KV0], preferred_element_type=f32) * (HEAD_DIM ** -0.5)
    for hd in range(NSA_HEADS):
        q_ref[0, hd] = q[:, hd * PAIR:(hd + 1) * PAIR].astype(bf16)
    kv = jnp.dot(h, w_ref[:, _KV0:_GATE0], preferred_element_type=f32)
    kc_ref[0] = kv[:, 0:128]
    vc_ref[0] = kv[:, 128:256]
    ks_ref[0] = kv[:, 256:384].astype(bf16)
    vs_ref[0] = kv[:, 384:512].astype(bf16)
    kw_ref[0] = kv[:, 512:640].astype(bf16)
    vw_ref[0] = kv[:, 640:768].astype(bf16)
    gl = jnp.dot(h, w_ref[:, _GATE0:_RW0], preferred_element_type=f32)
    gate_ref[0] = _sigmoid(gl + gb_ref[...])
    z_ref[0] = jnp.dot(h, w_ref[:, _RW0:_WCOLS], preferred_element_type=f32)


def _inproj(x, g, wcat, gbias, tm):
    B, S, D = x.shape
    tok = lambda w: pl.BlockSpec((1, tm, w), lambda b, i: (b, i, 0))
    const = lambda shp: pl.BlockSpec(shp, lambda b, i: (0,) * len(shp))
    return pl.pallas_call(
        _inproj_body,
        grid=(B, S // tm),
        in_specs=[tok(D), const((1, D)), const((D, _WCOLS)), const((1, 128))],
        out_specs=[pl.BlockSpec((1, NSA_HEADS, tm, PAIR), lambda b, i: (b, 0, i, 0))]
        + [tok(128)] * 7 + [tok(RWKV_COLS)],
        out_shape=[jax.ShapeDtypeStruct((B, NSA_HEADS, S, PAIR), bf16),
                   jax.ShapeDtypeStruct((B, S, 128), f32), jax.ShapeDtypeStruct((B, S, 128), f32)]
        + [jax.ShapeDtypeStruct((B, S, 128), bf16)] * 4
        + [jax.ShapeDtypeStruct((B, S, 128), f32), jax.ShapeDtypeStruct((B, S, RWKV_COLS), f32)],
        compiler_params=pltpu.CompilerParams(
            dimension_semantics=("arbitrary", "arbitrary"), vmem_limit_bytes=VMEM_LIMIT),
        name="inproj",
    )(x, g, wcat, gbias)


def _gelu_tanh(x):
    return x * (0.5 * (1.0 + jnp.tanh(np.sqrt(2.0 / np.pi) * (x + 0.044715 * (x * x * x)))))


def _compress_one(x, pe_ref, w_ref, b1_ref, w2_ref):
    n = x.shape[0]
    lo = _dot(x + pe_ref[0:1, :], w_ref[0])
    hi = _dot(x + pe_ref[1:2, :], w_ref[1])
    pre = lo + pltpu.roll(hi, n - 1, axis=0) + b1_ref[...]
    return _dot(_gelu_tanh(pre), w2_ref[...])


def _compress_body(xk_ref, xv_ref, pek_ref, wk_ref, bk_ref, w2k_ref, pev_ref, wv_ref, bv_ref, w2v_ref,
                   kc_ref, vc_ref):
    kc_ref[0] = _compress_one(xk_ref[0], pek_ref, wk_ref, bk_ref, w2k_ref).astype(bf16)
    vc_ref[0] = _compress_one(xv_ref[0], pev_ref, wv_ref, bv_ref, w2v_ref).astype(bf16)


def _compress(xk, xv, kparams, vparams):
    B, NC, W = xk.shape
    const = lambda a: pl.BlockSpec(a.shape, lambda b: (0,) * a.ndim)
    seq = pl.BlockSpec((1, NC, W), lambda b: (b, 0, 0))
    out = pl.BlockSpec((1, NC, PAIR), lambda b: (b, 0, 0))
    return pl.pallas_call(
        _compress_body,
        grid=(B,),
        in_specs=[seq, seq] + [const(a) for a in kparams] + [const(a) for a in vparams],
        out_specs=[out, out],
        out_shape=[jax.ShapeDtypeStruct((B, NC, PAIR), bf16)] * 2,
        compiler_params=pltpu.CompilerParams(dimension_semantics=("arbitrary",), vmem_limit_bytes=VMEM_LIMIT),
        name="compress",
    )(xk, xv, *kparams, *vparams)


def _nsa_body(q_ref, kc_ref, vc_ref, ks_ref, vs_ref, kw_ref, vw_ref, gate_ref, selT_ref, o_ref,
              m_scr, l_scr, acc_scr, *, tq, tk, seq):
    R = NSA_GROUP
    t0 = pl.program_id(1) * tq
    nc = seq // CMP_STRIDE
    ns = seq // SEL_BLOCK
    tok = (t0 + lax.broadcasted_iota(jnp.int32, (tq, 1), 0)).astype(f32)
    gate = gate_ref[0]
    lane = lax.broadcasted_iota(jnp.int32, (tq, PAIR), 1)

    def slope(g, r):
        return 2.0 ** (-(R * g + r + 1))

    def rows(a, r):
        return a[r * tq:(r + 1) * tq]

    outs = [None] * R
    for g in range(NSA_KV_HEADS):
        q4 = q_ref[0, R * g:R * (g + 1)].reshape(R * tq, PAIR)

        sc = _dot_nt(q4, kc_ref[0])
        cend = (lax.broadcasted_iota(jnp.int32, (1, nc), 1) * CMP_STRIDE + (CMP_LEN - 1)).astype(f32)
        dist_c = tok - cend
        vis_c = dist_c >= 0.0
        any_c = (tok >= float(CMP_LEN - 1)).astype(f32)
        vc = vc_ref[0]
        o_cmp = []
        psum = None
        for r in range(R):
            s = jnp.where(vis_c, rows(sc, r) - slope(g, r) * dist_c, NEG_INF)
            e = jnp.exp(s - jnp.max(s, axis=-1, keepdims=True))
            p = e * (any_c / jnp.sum(e, axis=-1, keepdims=True))
            o_cmp.append(_dot(p, vc))
            psum = p if psum is None else psum + p

        imp_t = lax.dot_general(selT_ref[...], psum.astype(bf16), (((1,), (1,)), ((), ())),
                                preferred_element_type=f32)
        p_lo = (psum - psum.astype(bf16).astype(f32)).astype(bf16)
        imp_t = imp_t + lax.dot_general(selT_ref[...], p_lo, (((1,), (1,)), ((), ())),
                                        preferred_element_type=f32)
        jrow = lax.broadcasted_iota(jnp.int32, (ns, tq), 0)
        cur = (t0 + lax.broadcasted_iota(jnp.int32, (ns, tq), 1)) // SEL_BLOCK
        forced = (jrow == 0) | (jrow == cur) | (jrow == cur - 1)
        score = jnp.where(forced, FORCE_SCORE, jnp.where(jrow <= cur, imp_t, -1.0))
        cnt = jnp.zeros((ns, tq), f32)
        for i in range(ns):
            si = score[i:i + 1, :]
            cnt = cnt + jnp.where(jrow > i, jnp.where(si >= score, 1.0, 0.0), jnp.where(si > score, 1.0, 0.0))
        bias_q = jnp.where(cnt < float(SEL_TOPK), 0.0, NEG_INF).T.astype(bf16)

        m_scr[...] = jnp.full(m_scr.shape, NEG_INF, f32)
        l_scr[...] = jnp.zeros(l_scr.shape, f32)
        acc_scr[...] = jnp.zeros(acc_scr.shape, f32)

        def sel_tile(kt, diag):
            k0 = pl.multiple_of(kt * tk, tk)
            kt_ = ks_ref[0, pl.ds(k0, tk), :]
            vt_ = vs_ref[0, pl.ds(k0, tk), :]
            s4 = _dot_nt(q4, kt_)
            kpos = k0 + lax.broadcasted_iota(jnp.int32, (1, tk), 1)
            dist = tok - kpos.astype(f32)
            blk = lax.broadcasted_iota(jnp.int32, (ns, tk), 0)
            expand = jnp.where(blk == (k0 + lax.broadcasted_iota(jnp.int32, (ns, tk), 1)) // SEL_BLOCK,
                               1.0, 0.0).astype(bf16)
            pre = jnp.dot(bias_q, expand, preferred_element_type=f32)
            if diag:
                pre = jnp.where(dist >= 0.0, pre, NEG_INF)
            for r in range(R):
                s = rows(s4, r) - slope(g, r) * dist + pre
                m_prev = m_scr[r][:, 0:1]
                m_new = jnp.maximum(m_prev, jnp.max(s, axis=-1, keepdims=True))
                p = jnp.exp(s - m_new)
                alpha = jnp.exp(m_prev - m_new)
                l_scr[r] = alpha * l_scr[r] + jnp.sum(p, axis=-1, keepdims=True)
                acc_scr[r] = alpha * acc_scr[r] + _dot(p, vt_)
                m_scr[r] = jnp.broadcast_to(m_new, (tq, PAIR))

        n_full = t0 // tk

        def full_step(kt, carry):
            sel_tile(kt, False)
            return carry

        lax.fori_loop(0, n_full, full_step, 0)
        sel_tile(n_full, True)
        o_sel = [acc_scr[r] / l_scr[r] for r in range(R)]

        w0 = pl.multiple_of(jnp.maximum(t0 - WINDOW, 0), tq)
        kwt = kw_ref[0, pl.ds(w0, WINDOW + tq), :]
        vwt = vw_ref[0, pl.ds(w0, WINDOW + tq), :]
        sw = _dot_nt(q4, kwt)
        kpos = w0 + lax.broadcasted_iota(jnp.int32, (1, WINDOW + tq), 1)
        dist_w = tok - kpos.astype(f32)
        vis_w = jnp.abs(dist_w - (WINDOW - 1) / 2.0) < WINDOW / 2.0
        for r in range(R):
            s = jnp.where(vis_w, rows(sw, r) - slope(g, r) * dist_w, NEG_INF)
            e = jnp.exp(s - jnp.max(s, axis=-1, keepdims=True))
            o_win = _dot(e, vwt) / jnp.sum(e, axis=-1, keepdims=True)
            c0 = 3 * (R * g + r)
            o = (gate[:, c0:c0 + 1] * o_cmp[r] + gate[:, c0 + 1:c0 + 2] * o_sel[r]
                 + gate[:, c0 + 2:c0 + 3] * o_win)
            outs[r] = o if g == 0 else jnp.where(lane < HEAD_DIM, outs[r], o)

    for r in range(R):
        o_ref[0, :, r * PAIR:(r + 1) * PAIR] = outs[r]


def _nsa(q, kc, vc, ks, vs, kw, vw, gates, sel_t, tq, tk):
    B, H, S, _ = q.shape
    nc = S // CMP_STRIDE
    full = lambda n: pl.BlockSpec((1, n, PAIR), lambda b, i: (b, 0, 0))
    body = functools.partial(_nsa_body, tq=tq, tk=tk, seq=S)
    return pl.pallas_call(
        body,
        grid=(B, S // tq),
        in_specs=[pl.BlockSpec((1, H, tq, PAIR), lambda b, i: (b, 0, i, 0)),
                  full(nc), full(nc), full(S), full(S), full(S), full(S),
                  pl.BlockSpec((1, tq, 128), lambda b, i: (b, i, 0)),
                  pl.BlockSpec(sel_t.shape, lambda b, i: (0, 0))],
        out_specs=pl.BlockSpec((1, tq, NSA_GROUP * PAIR), lambda b, i: (b, i, 0)),
        out_shape=jax.ShapeDtypeStruct((B, S, NSA_GROUP * PAIR), f32),
        scratch_shapes=[pltpu.VMEM((NSA_GROUP, tq, PAIR), f32)] * 3,
        compiler_params=pltpu.CompilerParams(
            dimension_semantics=("arbitrary", "arbitrary"), vmem_limit_bytes=VMEM_LIMIT),
        name="nsa",
    )(q, kc, vc, ks, vs, kw, vw, gates, sel_t)


def _rwkv_prep_body(z_ref, zp_ref, mu_ref, w0_ref, wl_ref, a0_ref, al_ref, gl_ref, kk_ref, ka_ref, rk_ref,
                    bd_ref, r_out, lw_out, k_out, v_out, kk_out, b_out, g_out, bv_out):
    i = pl.program_id(1)
    z = z_ref[0]
    tm = z.shape[0]
    prev_row = jnp.where(i > 0, zp_ref[0, 7:8, :], 0.0)
    row = lax.broadcasted_iota(jnp.int32, (tm, 1), 0)
    z_prev = jnp.where(row == 0, prev_row, pltpu.roll(z, 1, axis=0))
    zs = z + (z_prev - z) * mu_ref[...]
    r = zs[:, 0:512]
    k = zs[:, 512:1024]
    v = zs[:, 1024:1536]
    lora = zs[:, 1536:1664]
    gd = zs[:, 1664:1792]
    wlog = w0_ref[...] + _dot(jnp.tanh(lora), wl_ref[...])
    sp = jnp.maximum(-wlog, 0.0) + jnp.log(1.0 + jnp.exp(-jnp.abs(wlog)))
    lw = -jnp.exp(-sp - 0.5)
    a = _sigmoid(a0_ref[...] + _dot(lora, al_ref[...]))
    g = _dot(_sigmoid(gd), gl_ref[...])
    kk = k * kk_ref[...]
    ss = _split_dot(kk * kk, bd_ref[...])
    kk = kk * lax.rsqrt(jnp.maximum(ss, 1e-24))
    k2 = k * (1.0 + (a - 1.0) * ka_ref[...])
    bonus = _split_dot(r * k2 * rk_ref[...], bd_ref[...])
    r_out[0] = r
    lw_out[0] = lw
    k_out[0] = k2
    v_out[0] = v
    kk_out[0] = kk
    b_out[0] = kk * a
    g_out[0] = g
    bv_out[0] = bonus * v


def _rwkv_prep(z, params, tm):
    B, S, W = z.shape
    tok = lambda w: pl.BlockSpec((1, tm, w), lambda b, i: (b, i, 0))
    const = lambda a: pl.BlockSpec(a.shape, lambda b, i: (0,) * a.ndim)
    prev = pl.BlockSpec((1, 8, W), lambda b, i: (b, jnp.maximum(i * (tm // 8) - 1, 0), 0))
    return pl.pallas_call(
        _rwkv_prep_body,
        grid=(B, S // tm),
        in_specs=[tok(W), prev] + [const(a) for a in params],
        out_specs=[tok(RWKV_WIDTH)] * 8,
        out_shape=[jax.ShapeDtypeStruct((B, S, RWKV_WIDTH), f32)] * 8,
        compiler_params=pltpu.CompilerParams(
            dimension_semantics=("arbitrary", "arbitrary"), vmem_limit_bytes=VMEM_LIMIT),
        name="rwkv_prep",
    )(z, z, *params)


def _rwkv_scan_body(r_ref, lw_ref, k_ref, v_ref, kk_ref, b_ref, g_ref, bv_ref, lnw_ref, lnb_ref, y_ref,
                    s_scr, *, tt):
    C = CHUNK

    @pl.when(pl.program_id(1) == 0)
    def _():
        s_scr[...] = jnp.zeros(s_scr.shape, f32)

    ri = lax.broadcasted_iota(jnp.int32, (C, C), 0)
    ci = lax.broadcasted_iota(jnp.int32, (C, C), 1)
    tri_incl = ri >= ci
    tri_strict = ri > ci
    cum_mat = jnp.where(tri_incl, 1.0, 0.0)
    eye = jnp.where(ri == ci, 1.0, 0.0)
    lane = lax.broadcasted_iota(jnp.int32, (1, PAIR), 1)
    pr = lax.broadcasted_iota(jnp.int32, (PAIR, PAIR), 0) // HEAD_DIM
    pc = lax.broadcasted_iota(jnp.int32, (PAIR, PAIR), 1) // HEAD_DIM
    blockdiag = pr == pc
    head_mean = jnp.where(blockdiag, 1.0, 0.0).astype(bf16)
    first = lane < HEAD_DIM

    def chunk(c, carry):
        rows = pl.ds(pl.multiple_of(c * C, C), C)
        for p in range(RWKV_WIDTH // PAIR):
            cols = slice(p * PAIR, (p + 1) * PAIR)
            r_ = r_ref[0, rows, cols]
            lw_ = lw_ref[0, rows, cols]
            k_ = k_ref[0, rows, cols]
            v_ = v_ref[0, rows, cols]
            kk_ = kk_ref[0, rows, cols]
            b_ = b_ref[0, rows, cols]
            cum = jnp.dot(cum_mat, lw_, preferred_element_type=f32, precision=lax.Precision.HIGHEST)
            cum_end = cum[C - 1:C, :]
            e_neg = jnp.exp(-cum)
            e_end = jnp.exp(cum_end - cum)
            a_t = -kk_ * jnp.exp(cum - lw_)
            r_t = r_ * jnp.exp(cum)
            b_t = b_ * e_neg
            k_t = k_ * e_neg
            b_h = b_ * e_end
            k_h = k_ * e_end
            w_end = jnp.exp(cum_end)
            lhs = jnp.concatenate([a_t, r_t], axis=0)
            ua, w2, rq, yin = [], [], [], []
            for h in range(2):
                mine = first if h == 0 else jnp.logical_not(first)
                lhs_h = jnp.where(mine, lhs, 0.0)
                ab = _dot_nt(lhs_h, b_t)
                ak = _dot_nt(lhs_h, k_t)
                a_ab = jnp.where(tri_strict, ab[:C], 0.0)
                a_ak = jnp.where(tri_strict, ak[:C], 0.0)
                a_rb = jnp.where(tri_incl, ab[C:], 0.0)
                a_rk = jnp.where(tri_incl, ak[C:], 0.0)
                tinv = eye + a_ab
                pw = a_ab
                for _ in range(5):
                    pw = _dot(pw, pw)
                    tinv = tinv + _dot(tinv, pw)
                x1 = _dot(jnp.concatenate([a_ak, a_rk], axis=0), v_)
                tx = _dot(tinv, jnp.concatenate([a_t, x1[:C]], axis=1))
                ex = _dot(a_rb, tx)
                ua.append(tx[:, :PAIR])
                w2.append(tx[:, PAIR:])
                rq.append(r_t + ex[:, :PAIR])
                yin.append(x1[C:] + ex[:, PAIR:])
            ua = jnp.where(first, ua[0], ua[1])
            w2 = jnp.where(first, w2[0], w2[1])
            rq = jnp.where(first, rq[0], rq[1])
            yin = jnp.where(first, yin[0], yin[1])
            s0 = s_scr[p]
            y = _dot_nt(rq, s0) + yin
            gq = _dot_tn(jnp.concatenate([ua, w2], axis=1), b_h)
            gmat = jnp.where(blockdiag, gq[:PAIR], 0.0)
            qmat = jnp.where(blockdiag, gq[PAIR:] + _dot_tn(v_, k_h), 0.0)
            s_scr[p] = s0 * w_end + _dot(s0, gmat) + qmat
            mu = _split_dot(y, head_mean) * (1.0 / HEAD_DIM)
            d = y - mu
            var = _split_dot(d * d, head_mean) * (1.0 / HEAD_DIM)
            yn = d * lax.rsqrt(var + GN_EPS) * lnw_ref[:, cols] + lnb_ref[:, cols]
            y_ref[0, rows, cols] = (yn + bv_ref[0, rows, cols]) * g_ref[0, rows, cols]
        return carry

    lax.fori_loop(0, tt // C, chunk, 0)


def _rwkv_scan(r, lw, k, v, kk, b, g, bv, lnw, lnb, tt):
    B, S, W = r.shape
    tok = pl.BlockSpec((1, tt, W), lambda bb, i: (bb, i, 0))
    const = pl.BlockSpec((1, W), lambda bb, i: (0, 0))
    return pl.pallas_call(
        functools.partial(_rwkv_scan_body, tt=tt),
        grid=(B, S // tt),
        in_specs=[tok] * 8 + [const, const],
        out_specs=tok,
        out_shape=jax.ShapeDtypeStruct((B, S, W), f32),
        scratch_shapes=[pltpu.VMEM((W // PAIR, PAIR, PAIR), f32)],
        compiler_params=pltpu.CompilerParams(
            dimension_semantics=("arbitrary", "arbitrary"), vmem_limit_bytes=VMEM_LIMIT),
        name="rwkv_scan",
    )(r, lw, k, v, kk, b, g, bv, lnw, lnb)


def _post_body(x_ref, yn_ref, yr_ref, p_ref, wo_ref, gpost_ref, gpre_ref, gmlp_ref, wup_ref, wdn_ref,
               wpg_ref, wple_ref, o_ref):
    y = jnp.concatenate([yn_ref[0], yr_ref[0]], axis=1).astype(bf16)
    mix = jnp.dot(y, wo_ref[...], preferred_element_type=f32)
    x1 = x_ref[0] + _rms(mix, gpost_ref[...])
    h = _rms(x1, gpre_ref[...]).astype(bf16)
    acc = None
    for c in range(D_FF // D_MODEL):
        cs = slice(c * D_MODEL, (c + 1) * D_MODEL)
        u = jnp.dot(h, wup_ref[:, cs], preferred_element_type=f32)
        u = jnp.square(jnp.maximum(u, 0.0)).astype(bf16)
        part = jnp.dot(u, wdn_ref[cs, :], preferred_element_type=f32)
        acc = part if acc is None else acc + part
    x2 = x1 + _rms(acc, gmlp_ref[...])
    gate = _sigmoid(jnp.dot(x2.astype(bf16), wpg_ref[...], preferred_element_type=f32))
    o_ref[0] = x2 + gate * jnp.dot(p_ref[0].astype(bf16), wple_ref[...], preferred_element_type=f32)


def _post(x, yn, yr, p, weights, tm):
    B, S, D = x.shape
    tok = lambda w: pl.BlockSpec((1, tm, w), lambda b, i: (b, i, 0))
    const = lambda a: pl.BlockSpec(a.shape, lambda b, i: (0,) * a.ndim, pipeline_mode=pl.Buffered(1))
    return pl.pallas_call(
        _post_body,
        grid=(B, S // tm),
        in_specs=[tok(D), tok(512), tok(512), tok(PLE_DIM)] + [const(a) for a in weights],
        out_specs=tok(D),
        out_shape=jax.ShapeDtypeStruct((B, S, D), f32),
        compiler_params=pltpu.CompilerParams(
            dimension_semantics=("arbitrary", "arbitrary"), vmem_limit_bytes=VMEM_LIMIT),
        name="post",
    )(x, yn, yr, p, *weights)


def _pack_inproj(w_in, gate_bias):
    wq = w_in[:, 0:512].reshape(D_MODEL, NSA_KV_HEADS, NSA_GROUP, 1, HEAD_DIM)
    onehot = jnp.eye(NSA_KV_HEADS, dtype=w_in.dtype).reshape(1, NSA_KV_HEADS, 1, NSA_KV_HEADS, 1)
    wq = (wq * onehot).reshape(D_MODEL, NSA_HEADS * PAIR)
    wkv = w_in[:, 512:1280]
    wg = jnp.pad(w_in[:, 1280:1304], ((0, 0), (0, 128 - 24)))
    wr = w_in[:, 1304:]
    wcat = jnp.concatenate([wq, wkv, wg, wr], axis=1).astype(bf16)
    return wcat, jnp.pad(gate_bias, (0, 128 - 24)).reshape(1, 128)


def _pack_compress(pe, w1, b1, w2):
    eye2 = jnp.eye(NSA_KV_HEADS, dtype=f32)
    w1r = w1.reshape(CMP_LEN, HEAD_DIM, CMP_HIDDEN)
    halves = []
    for part in (w1r[:CMP_STRIDE], w1r[CMP_STRIDE:]):
        halves.append(jnp.einsum("jdc,gh->jgdhc", part, eye2).reshape(CMP_STRIDE * PAIR, 2 * CMP_HIDDEN))
    w = jnp.stack(halves).astype(bf16)
    per = jnp.broadcast_to(pe.reshape(2, CMP_STRIDE, 1, HEAD_DIM), (2, CMP_STRIDE, NSA_KV_HEADS, HEAD_DIM))
    per = per.reshape(2, CMP_STRIDE * PAIR)
    b1p = jnp.tile(b1, NSA_KV_HEADS).reshape(1, 2 * CMP_HIDDEN)
    w2p = jnp.einsum("cd,gh->gchd", w2, eye2).reshape(2 * CMP_HIDDEN, PAIR).astype(bf16)
    return per, w, b1p, w2p


def _sel_map_t(seq):
    nc, ns = seq // CMP_STRIDE, seq // SEL_BLOCK
    c0 = np.arange(nc) * CMP_STRIDE
    s0 = np.arange(ns) * SEL_BLOCK
    ov = (np.minimum(c0[:, None] + CMP_LEN - 1, s0[None, :] + SEL_BLOCK - 1)
          - np.maximum(c0[:, None], s0[None, :]) + 1)
    m = np.clip(ov, 0, None).astype(np.float32) / CMP_STRIDE
    m[nc - 1] = 0.0
    return jnp.asarray(m.T, dtype=bf16)


def _row(a):
    return a.reshape(1, -1)


def _mixers(x, g_mix_pre, w_in, nsa_gate_bias, cmp_k, cmp_v, shift_mu, w0, w_lora_up, a0, a_lora_up, g_lora_up,
            k_k, k_a, r_k, lnx_w, lnx_b):
    B, S, _ = x.shape
    bd = jnp.asarray(np.kron(np.eye(RWKV_WIDTH // HEAD_DIM), np.ones((HEAD_DIM, HEAD_DIM))), dtype=bf16)
    wcat, gbias = _pack_inproj(w_in, nsa_gate_bias)
    q, kc, vc, ks, vs, kw, vw, gates, z = _inproj(x, _row(g_mix_pre), wcat, gbias, tm=512)
    chunks = (B, S // CMP_STRIDE, CMP_STRIDE * PAIR)
    kcmp, vcmp = _compress(kc.reshape(chunks), vc.reshape(chunks), _pack_compress(*cmp_k), _pack_compress(*cmp_v))
    y_nsa = _nsa(q, kcmp, vcmp, ks, vs, kw, vw, gates, _sel_map_t(S), tq=256, tk=512)

    wl = jnp.concatenate([w_lora_up, jnp.zeros((ICLR_LORA, RWKV_WIDTH), f32)], axis=0).astype(bf16)
    al = jnp.concatenate([jnp.zeros((DECAY_LORA, RWKV_WIDTH), f32), a_lora_up], axis=0).astype(bf16)
    prep_params = (_row(shift_mu), _row(w0), wl, _row(a0), al, g_lora_up.astype(bf16),
                   _row(k_k), _row(k_a), _row(r_k), bd)
    r, lw, k2, v, kk, b, g, bv = _rwkv_prep(z, prep_params, tm=256)
    y_rwkv = _rwkv_scan(r, lw, k2, v, kk, b, g, bv, _row(lnx_w), _row(lnx_b), tt=256)
    return z, y_nsa, y_rwkv


def kernel(x, p, g_mix_pre, g_mix_post, g_mlp_pre, g_mlp_post, w_in, nsa_gate_bias, cmp_pe_k, cmp_k_w1, cmp_k_b1, cmp_k_w2, cmp_pe_v, cmp_v_w1, cmp_v_b1, cmp_v_w2, shift_mu, w0, w_lora_up, a0, a_lora_up, g_lora_up, k_k, k_a, r_k, lnx_w, lnx_b, w_out, w_up, w_down, w_ple, w_ple_gate):
    D = x.shape[-1]
    for i in range(p.shape[0]):
        _, y_nsa, y_rwkv = _mixers(
            x, g_mix_pre[i], w_in[i], nsa_gate_bias[i],
            (cmp_pe_k[i], cmp_k_w1[i], cmp_k_b1[i], cmp_k_w2[i]), (cmp_pe_v[i], cmp_v_w1[i], cmp_v_b1[i], cmp_v_w2[i]),
            shift_mu[i], w0[i], w_lora_up[i], a0[i], a_lora_up[i], g_lora_up[i], k_k[i], k_a[i], r_k[i],
            lnx_w[i], lnx_b[i])
        wo_nsa = w_out[i][:512].reshape(NSA_KV_HEADS, NSA_GROUP, HEAD_DIM, D).transpose(1, 0, 2, 3).reshape(512, D)
        wo = jnp.concatenate([wo_nsa, w_out[i][512:]], axis=0).astype(bf16)
        weights = (wo, _row(g_mix_post[i]), _row(g_mlp_pre[i]), _row(g_mlp_post[i]), w_up[i].astype(bf16),
                   w_down[i].astype(bf16), w_ple_gate[i].astype(bf16), w_ple[i].astype(bf16))
        x = _post(x, y_nsa, y_rwkv, p[i], weights, tm=512)
    return x
```

```python
import functools

import jax
import jax.numpy as jnp
import numpy as np
from jax import lax
from jax.experimental import pallas as pl
from jax.experimental.pallas import tpu as pltpu

f32 = jnp.float32
bf16 = jnp.bfloat16

D_MODEL = 1024
HEAD_DIM = 64
NSA_HEADS = 8
NSA_KV_HEADS = 2
NSA_GROUP = NSA_HEADS // NSA_KV_HEADS
CMP_LEN = 32
CMP_STRIDE = 16
CMP_HIDDEN = 2 * HEAD_DIM
SEL_BLOCK = 64
SEL_TOPK = 16
WINDOW = 512
RWKV_WIDTH = 512
RWKV_COLS = 1792
DECAY_LORA = 64
ICLR_LORA = 64
GATE_LORA = 128
D_FF = 4 * D_MODEL
PLE_DIM = 256
NORM_EPS = 1e-6
GN_EPS = 64e-5
NEG_INF = -1e30
FORCE_SCORE = 1e4

PAIR = 2 * HEAD_DIM
CHUNK = 64
VMEM_LIMIT = 56 * 1024 * 1024

_Q0, _KV0, _GATE0, _RW0, _WCOLS = 0, 1024, 1792, 1920, 3712


def _dot(a, b):
    return jnp.dot(a.astype(bf16), b.astype(bf16), preferred_element_type=f32)


def _dot_nt(a, b):
    return lax.dot_general(a.astype(bf16), b.astype(bf16), (((1,), (1,)), ((), ())), preferred_element_type=f32)


def _dot_tn(a, b):
    return lax.dot_general(a.astype(bf16), b.astype(bf16), (((0,), (0,)), ((), ())), preferred_element_type=f32)


def _split_dot(x, w):
    hi = x.astype(bf16)
    lo = (x - hi.astype(f32)).astype(bf16)
    return jnp.dot(hi, w, preferred_element_type=f32) + jnp.dot(lo, w, preferred_element_type=f32)


def _rms(x, g):
    ms = jnp.mean(x * x, axis=-1, keepdims=True)
    return x * lax.rsqrt(ms + NORM_EPS) * g


def _sigmoid(x):
    return 1.0 / (1.0 + jnp.exp(-x))


def _inproj_body(x_ref, g_ref, w_ref, gb_ref, q_ref, kc_ref, vc_ref, ks_ref, vs_ref, kw_ref, vw_ref,
                 gate_ref, z_ref):
    h = _rms(x_ref[0], g_ref[...]).astype(bf16)
    q = jnp.dot(h, w_ref[:, _Q0:_KV0], preferred_element_type=f32) * (HEAD_DIM ** -0.5)
    for hd in range(NSA_HEADS):
        q_ref[0, hd] = q[:, hd * PAIR:(hd + 1) * PAIR].astype(bf16)
    kv = jnp.dot(h, w_ref[:, _KV0:_GATE0], preferred_element_type=f32)
    kc_ref[0] = kv[:, 0:128]
    vc_ref[0] = kv[:, 128:256]
    ks_ref[0] = kv[:, 256:384].astype(bf16)
    vs_ref[0] = kv[:, 384:512].astype(bf16)
    kw_ref[0] = kv[:, 512:640].astype(bf16)
    vw_ref[0] = kv[:, 640:768].astype(bf16)
    gl = jnp.dot(h, w_ref[:, _GATE0:_RW0], preferred_element_type=f32)
    gate_ref[0] = _sigmoid(gl + gb_ref[...])
    z_ref[0] = jnp.dot(h, w_ref[:, _RW0:_WCOLS], preferred_element_type=f32)


def _inproj(x, g, wcat, gbias, tm):
    B, S, D = x.shape
    tok = lambda w: pl.BlockSpec((1, tm, w), lambda b, i: (b, i, 0))
    const = lambda shp: pl.BlockSpec(shp, lambda b, i: (0,) * len(shp))
    return pl.pallas_call(
        _inproj_body,
        grid=(B, S // tm),
        in_specs=[tok(D), const((1, D)), const((D, _WCOLS)), const((1, 128))],
        out_specs=[pl.BlockSpec((1, NSA_HEADS, tm, PAIR), lambda b, i: (b, 0, i, 0))]
        + [tok(128)] * 7 + [tok(RWKV_COLS)],
        out_shape=[jax.ShapeDtypeStruct((B, NSA_HEADS, S, PAIR), bf16),
                   jax.ShapeDtypeStruct((B, S, 128), f32), jax.ShapeDtypeStruct((B, S, 128), f32)]
        + [jax.ShapeDtypeStruct((B, S, 128), bf16)] * 4
        + [jax.ShapeDtypeStruct((B, S, 128), f32), jax.ShapeDtypeStruct((B, S, RWKV_COLS), f32)],
        compiler_params=pltpu.CompilerParams(
            dimension_semantics=("arbitrary", "arbitrary"), vmem_limit_bytes=VMEM_LIMIT),
        name="inproj",
    )(x, g, wcat, gbias)


def _gelu_tanh(x):
    return x * (0.5 * (1.0 + jnp.tanh(np.sqrt(2.0 / np.pi) * (x + 0.044715 * (x * x * x)))))


def _compress_one(x, pe_ref, w_ref, b1_ref, w2_ref):
    n = x.shape[0]
    lo = _dot(x + pe_ref[0:1, :], w_ref[0])
    hi = _dot(x + pe_ref[1:2, :], w_ref[1])
    pre = lo + pltpu.roll(hi, n - 1, axis=0) + b1_ref[...]
    return _dot(_gelu_tanh(pre), w2_ref[...])


def _compress_body(xk_ref, xv_ref, pek_ref, wk_ref, bk_ref, w2k_ref, pev_ref, wv_ref, bv_ref, w2v_ref,
                   kc_ref, vc_ref):
    kc_ref[0] = _compress_one(xk_ref[0], pek_ref, wk_ref, bk_ref, w2k_ref).astype(bf16)
    vc_ref[0] = _compress_one(xv_ref[0], pev_ref, wv_ref, bv_ref, w2v_ref).astype(bf16)


def _compress(xk, xv, kparams, vparams):
    B, NC, W = xk.shape
    const = lambda a: pl.BlockSpec(a.shape, lambda b: (0,) * a.ndim)
    seq = pl.BlockSpec((1, NC, W), lambda b: (b, 0, 0))
    out = pl.BlockSpec((1, NC, PAIR), lambda b: (b, 0, 0))
    return pl.pallas_call(
        _compress_body,
        grid=(B,),
        in_specs=[seq, seq] + [const(a) for a in kparams] + [const(a) for a in vparams],
        out_specs=[out, out],
        out_shape=[jax.ShapeDtypeStruct((B, NC, PAIR), bf16)] * 2,
        compiler_params=pltpu.CompilerParams(dimension_semantics=("arbitrary",), vmem_limit_bytes=VMEM_LIMIT),
        name="compress",
    )(xk, xv, *kparams, *vparams)


def _nsa_body(q_ref, kc_ref, vc_ref, ks_ref, vs_ref, kw_ref, vw_ref, gate_ref, selT_ref, o_ref,
              m_scr, l_scr, acc_scr, *, tq, tk, seq):
    R = NSA_GROUP
    t0 = pl.program_id(1) * tq
    nc = seq // CMP_STRIDE
    ns = seq // SEL_BLOCK
    tok = (t0 + lax.broadcasted_iota(jnp.int32, (tq, 1), 0)).astype(f32)
    gate = gate_ref[0]
    lane = lax.broadcasted_iota(jnp.int32, (tq, PAIR), 1)

    def slope(g, r):
        return 2.0 ** (-(R * g + r + 1))

    def rows(a, r):
        return a[r * tq:(r + 1) * tq]

    outs = [None] * R
    for g in range(NSA_KV_HEADS):
        q4 = q_ref[0, R * g:R * (g + 1)].reshape(R * tq, PAIR)

        sc = _dot_nt(q4, kc_ref[0])
        cend = (lax.broadcasted_iota(jnp.int32, (1, nc), 1) * CMP_STRIDE + (CMP_LEN - 1)).astype(f32)
        dist_c = tok - cend
        vis_c = dist_c >= 0.0
        any_c = (tok >= float(CMP_LEN - 1)).astype(f32)
        vc = vc_ref[0]
        o_cmp = []
        psum = None
        for r in range(R):
            s = jnp.where(vis_c, rows(sc, r) - slope(g, r) * dist_c, NEG_INF)
            e = jnp.exp(s - jnp.max(s, axis=-1, keepdims=True))
            p = e * (any_c / jnp.sum(e, axis=-1, keepdims=True))
            o_cmp.append(_dot(p, vc))
            psum = p if psum is None else psum + p

        imp_t = lax.dot_general(selT_ref[...], psum.astype(bf16), (((1,), (1,)), ((), ())),
                                preferred_element_type=f32)
        p_lo = (psum - psum.astype(bf16).astype(f32)).astype(bf16)
        imp_t = imp_t + lax.dot_general(selT_ref[...], p_lo, (((1,), (1,)), ((), ())),
                                        preferred_element_type=f32)
        jrow = lax.broadcasted_iota(jnp.int32, (ns, tq), 0)
        cur = (t0 + lax.broadcasted_iota(jnp.int32, (ns, tq), 1)) // SEL_BLOCK
        forced = (jrow == 0) | (jrow == cur) | (jrow == cur - 1)
        score = jnp.where(forced, FORCE_SCORE, jnp.where(jrow <= cur, imp_t, -1.0))
        cnt = jnp.zeros((ns, tq), f32)
        for i in range(ns):
            si = score[i:i + 1, :]
            cnt = cnt + jnp.where(jrow > i, jnp.where(si >= score, 1.0, 0.0), jnp.where(si > score, 1.0, 0.0))
        bias_q = jnp.where(cnt < float(SEL_TOPK), 0.0, NEG_INF).T.astype(bf16)

        m_scr[...] = jnp.full(m_scr.shape, NEG_INF, f32)
        l_scr[...] = jnp.zeros(l_scr.shape, f32)
        acc_scr[...] = jnp.zeros(acc_scr.shape, f32)

        def sel_tile(kt, diag):
            k0 = pl.multiple_of(kt * tk, tk)
            kt_ = ks_ref[0, pl.ds(k0, tk), :]
            vt_ = vs_ref[0, pl.ds(k0, tk), :]
            s4 = _dot_nt(q4, kt_)
            kpos = k0 + lax.broadcasted_iota(jnp.int32, (1, tk), 1)
            dist = tok - kpos.astype(f32)
            blk = lax.broadcasted_iota(jnp.int32, (ns, tk), 0)
            expand = jnp.where(blk == (k0 + lax.broadcasted_iota(jnp.int32, (ns, tk), 1)) // SEL_BLOCK,
                               1.0, 0.0).astype(bf16)
            pre = jnp.dot(bias_q, expand, preferred_element_type=f32)
            if diag:
                pre = jnp.where(dist >= 0.0, pre, NEG_INF)
            for r in range(R):
                s = rows(s4, r) - slope(g, r) * dist + pre
                m_prev = m_scr[r][:, 0:1]
                m_new = jnp.maximum(m_prev, jnp.max(s, axis=-1, keepdims=True))
                p = jnp.exp(s - m_new)
                alpha = jnp.exp(m_prev - m_new)
                l_scr[r] = alpha * l_scr[r] + jnp.sum(p, axis=-1, keepdims=True)
                acc_scr[r] = alpha * acc_scr[r] + _dot(p, vt_)
                m_scr[r] = jnp.broadcast_to(m_new, (tq, PAIR))

        n_full = t0 // tk

        def full_step(kt, carry):
            sel_tile(kt, False)
            return carry

        lax.fori_loop(0, n_full, full_step, 0)
        sel_tile(n_full, True)
        o_sel = [acc_scr[r] / l_scr[r] for r in range(R)]

        w0 = pl.multiple_of(jnp.maximum(t0 - WINDOW, 0), tq)
        kwt = kw_ref[0, pl.ds(w0, WINDOW + tq), :]
        vwt = vw_ref[0, pl.ds(w0, WINDOW + tq), :]
        sw = _dot_nt(q4, kwt)
        kpos = w0 + lax.broadcasted_iota(jnp.int32, (1, WINDOW + tq), 1)
        dist_w = tok - kpos.astype(f32)
        vis_w = jnp.abs(dist_w - (WINDOW - 1) / 2.0) < WINDOW / 2.0
        for r in range(R):
            s = jnp.where(vis_w, rows(sw, r) - slope(g, r) * dist_w, NEG_INF)
            e = jnp.exp(s - jnp.max(s, axis=-1, keepdims=True))
            o_win = _dot(e, vwt) / jnp.sum(e, axis=-1, keepdims=True)
            c0 = 3 * (R * g + r)
            o = (gate[:, c0:c0 + 1] * o_cmp[r] + gate[:, c0 + 1:c0 + 2] * o_sel[r]
                 + gate[:, c0 + 2:c0 + 3] * o_win)
            outs[r] = o if g == 0 else jnp.where(lane < HEAD_DIM, outs[r], o)

    for r in range(R):
        o_ref[0, :, r * PAIR:(r + 1) * PAIR] = outs[r]


def _nsa(q, kc, vc, ks, vs, kw, vw, gates, sel_t, tq, tk):
    B, H, S, _ = q.shape
    nc = S // CMP_STRIDE
    full = lambda n: pl.BlockSpec((1, n, PAIR), lambda b, i: (b, 0, 0))
    body = functools.partial(_nsa_body, tq=tq, tk=tk, seq=S)
    return pl.pallas_call(
        body,
        grid=(B, S // tq),
        in_specs=[pl.BlockSpec((1, H, tq, PAIR), lambda b, i: (b, 0, i, 0)),
                  full(nc), full(nc), full(S), full(S), full(S), full(S),
                  pl.BlockSpec((1, tq, 128), lambda b, i: (b, i, 0)),
                  pl.BlockSpec(sel_t.shape, lambda b, i: (0, 0))],
        out_specs=pl.BlockSpec((1, tq, NSA_GROUP * PAIR), lambda b, i: (b, i, 0)),
        out_shape=jax.ShapeDtypeStruct((B, S, NSA_GROUP * PAIR), f32),
        scratch_shapes=[pltpu.VMEM((NSA_GROUP, tq, PAIR), f32)] * 3,
        compiler_params=pltpu.CompilerParams(
            dimension_semantics=("arbitrary", "arbitrary"), vmem_limit_bytes=VMEM_LIMIT),
        name="nsa",
    )(q, kc, vc, ks, vs, kw, vw, gates, sel_t)


def _alibi_key_columns(pos, ns, onehot):
    a = np.zeros((pos.shape[0], PAIR), np.float32)
    if onehot:
        a[np.arange(pos.shape[0]), pos // SEL_BLOCK] = 1.0
    a[:, ns] = -1.0
    a[:, ns + 1] = -1.0
    a[:, ns + 2] = pos // SEL_BLOCK
    a[:, ns + 3] = pos % SEL_BLOCK
    return jnp.asarray(a, dtype=bf16)


def _nsa2_body(q_ref, kc_ref, vc_ref, ks_ref, vs_ref, kw_ref, vw_ref, gate_ref, selT_ref, auxk_ref, auxc_ref,
               o_ref, m_scr, l_scr, acc_scr, *, tq, tk, seq):
    R = NSA_GROUP
    t0 = pl.program_id(1) * tq
    nc = seq // CMP_STRIDE
    ns = seq // SEL_BLOCK
    tok = (t0 + lax.broadcasted_iota(jnp.int32, (tq, 1), 0)).astype(f32)
    gate = gate_ref[0]
    lane = lax.broadcasted_iota(jnp.int32, (tq, PAIR), 1)
    lane1 = lax.broadcasted_iota(jnp.int32, (1, PAIR), 1)

    def slope(g, r):
        return 2.0 ** (-(R * g + r + 1))

    def rows(a, r):
        return a[r * tq:(r + 1) * tq]

    tl = t0 + lax.broadcasted_iota(jnp.int32, (8, tq), 1)
    rid = lax.broadcasted_iota(jnp.int32, (8, tq), 0)
    alibi_rows = jnp.where(rid == 0, ((tl // SEL_BLOCK) * SEL_BLOCK).astype(f32),
                           jnp.where(rid == 1, (tl % SEL_BLOCK).astype(f32),
                                     jnp.where(rid == 2, float(SEL_BLOCK), jnp.where(rid == 3, 1.0, 0.0))))
    aux0 = jnp.concatenate([jnp.zeros((ns, tq), f32), alibi_rows, jnp.zeros((PAIR - ns - 8, tq), f32)], axis=0).T

    def augment(q4, aux, g):
        aux4 = jnp.concatenate([aux * jnp.where(lane1 < ns, 1.0, slope(g, r)) for r in range(R)], axis=0)
        return jnp.concatenate([q4, aux4.astype(bf16)], axis=1)

    kc_aug = jnp.concatenate([kc_ref[0], auxc_ref[...]], axis=1)
    outs = [None] * R
    for g in range(NSA_KV_HEADS):
        q4 = q_ref[0, R * g:R * (g + 1)].reshape(R * tq, PAIR)
        qa = augment(q4, aux0, g)

        sc = _dot_nt(qa, kc_aug)
        cend = (lax.broadcasted_iota(jnp.int32, (1, nc), 1) * CMP_STRIDE + (CMP_LEN - 1)).astype(f32)
        vis_c = tok >= cend
        any_c = (tok >= float(CMP_LEN - 1)).astype(f32)
        vc = vc_ref[0]
        o_cmp = []
        psum = None
        for r in range(R):
            s = jnp.where(vis_c, rows(sc, r), NEG_INF)
            e = jnp.exp(s - jnp.max(s, axis=-1, keepdims=True))
            p = e * (any_c / jnp.sum(e, axis=-1, keepdims=True))
            o_cmp.append(_dot(p, vc))
            psum = p if psum is None else psum + p

        imp_t = lax.dot_general(selT_ref[...], psum.astype(bf16), (((1,), (1,)), ((), ())),
                                preferred_element_type=f32)
        p_lo = (psum - psum.astype(bf16).astype(f32)).astype(bf16)
        imp_t = imp_t + lax.dot_general(selT_ref[...], p_lo, (((1,), (1,)), ((), ())),
                                        preferred_element_type=f32)
        jrow = lax.broadcasted_iota(jnp.int32, (ns, tq), 0)
        cur = (t0 + lax.broadcasted_iota(jnp.int32, (ns, tq), 1)) // SEL_BLOCK
        forced = (jrow == 0) | (jrow == cur) | (jrow == cur - 1)
        score = jnp.where(forced, FORCE_SCORE, jnp.where(jrow <= cur, imp_t, -1.0))
        cnt = jnp.zeros((ns, tq), f32)
        for i in range(ns):
            si = score[i:i + 1, :]
            cnt = cnt + jnp.where(jrow > i, jnp.where(si >= score, 1.0, 0.0), jnp.where(si > score, 1.0, 0.0))
        bias_t = jnp.where(cnt < float(SEL_TOPK), 0.0, NEG_INF)
        qs = augment(q4, aux0 + jnp.concatenate([bias_t, jnp.zeros((PAIR - ns, tq), f32)], axis=0).T, g)

        m_scr[...] = jnp.full(m_scr.shape, NEG_INF, f32)
        l_scr[...] = jnp.zeros(l_scr.shape, f32)
        acc_scr[...] = jnp.zeros(acc_scr.shape, f32)

        def sel_tile(kt, diag):
            k0 = pl.multiple_of(kt * tk, tk)
            k_aug = jnp.concatenate([ks_ref[0, pl.ds(k0, tk), :], auxk_ref[pl.ds(k0, tk), :]], axis=1)
            vt_ = vs_ref[0, pl.ds(k0, tk), :]
            if diag:
                kpos = (k0 + lax.broadcasted_iota(jnp.int32, (1, tk), 1)).astype(f32)
                causal = tok >= kpos
            for r in range(R):
                s = _dot_nt(rows(qs, r), k_aug)
                if diag:
                    s = jnp.where(causal, s, NEG_INF)
                m_prev = m_scr[r][:, 0:1]
                m_new = jnp.maximum(m_prev, jnp.max(s, axis=-1, keepdims=True))
                p = jnp.exp(s - m_new)
                alpha = jnp.exp(m_prev - m_new)
                l_scr[r] = alpha * l_scr[r] + jnp.sum(p, axis=-1, keepdims=True)
                acc_scr[r] = alpha * acc_scr[r] + _dot(p, vt_)
                m_scr[r] = jnp.broadcast_to(m_new, (tq, PAIR))

        n_full = t0 // tk

        def full_step(kt, carry):
            sel_tile(kt, False)
            return carry

        lax.fori_loop(0, n_full, full_step, 0)
        sel_tile(n_full, True)
        o_sel = [acc_scr[r] / l_scr[r] for r in range(R)]

        w0 = pl.multiple_of(jnp.maximum(t0 - WINDOW, 0), tq)
        wrows = pl.ds(w0, WINDOW + tq)
        kw_aug = jnp.concatenate([kw_ref[0, wrows, :], auxk_ref[wrows, :]], axis=1)
        vwt = vw_ref[0, wrows, :]
        sw = _dot_nt(qa, kw_aug)
        kpos = w0 + lax.broadcasted_iota(jnp.int32, (1, WINDOW + tq), 1)
        dist_w = tok - kpos.astype(f32)
        vis_w = jnp.abs(dist_w - (WINDOW - 1) / 2.0) < WINDOW / 2.0
        for r in range(R):
            s = jnp.where(vis_w, rows(sw, r), NEG_INF)
            e = jnp.exp(s - jnp.max(s, axis=-1, keepdims=True))
            o_win = _dot(e, vwt) / jnp.sum(e, axis=-1, keepdims=True)
            c0 = 3 * (R * g + r)
            o = (gate[:, c0:c0 + 1] * o_cmp[r] + gate[:, c0 + 1:c0 + 2] * o_sel[r]
                 + gate[:, c0 + 2:c0 + 3] * o_win)
            outs[r] = o if g == 0 else jnp.where(lane < HEAD_DIM, outs[r], o)

    for r in range(R):
        o_ref[0, :, r * PAIR:(r + 1) * PAIR] = outs[r]


def _nsa2(q, kc, vc, ks, vs, kw, vw, gates, sel_t, tq, tk):
    B, H, S, _ = q.shape
    nc, ns = S // CMP_STRIDE, S // SEL_BLOCK
    aux_k = _alibi_key_columns(np.arange(S), ns, onehot=True)
    aux_c = _alibi_key_columns(np.arange(nc) * CMP_STRIDE + (CMP_LEN - 1), ns, onehot=False)
    full = lambda n: pl.BlockSpec((1, n, PAIR), lambda b, i: (b, 0, 0))
    const = lambda a: pl.BlockSpec(a.shape, lambda b, i: (0, 0))
    body = functools.partial(_nsa2_body, tq=tq, tk=tk, seq=S)
    return pl.pallas_call(
        body,
        grid=(B, S // tq),
        in_specs=[pl.BlockSpec((1, H, tq, PAIR), lambda b, i: (b, 0, i, 0)),
                  full(nc), full(nc), full(S), full(S), full(S), full(S),
                  pl.BlockSpec((1, tq, 128), lambda b, i: (b, i, 0)),
                  const(sel_t), const(aux_k), const(aux_c)],
        out_specs=pl.BlockSpec((1, tq, NSA_GROUP * PAIR), lambda b, i: (b, i, 0)),
        out_shape=jax.ShapeDtypeStruct((B, S, NSA_GROUP * PAIR), f32),
        scratch_shapes=[pltpu.VMEM((NSA_GROUP, tq, PAIR), f32)] * 3,
        compiler_params=pltpu.CompilerParams(
            dimension_semantics=("arbitrary", "arbitrary"), vmem_limit_bytes=VMEM_LIMIT),
        name="nsa",
    )(q, kc, vc, ks, vs, kw, vw, gates, sel_t, aux_k, aux_c)


def _nsa3_body(q_ref, kc_ref, vc_ref, ks_ref, vs_ref, kw_ref, vw_ref, gate_ref, selT_ref, auxk_ref, auxc_ref,
               o_ref, m_scr, acc_scr, *, tq, tk, seq):
    R, G, H = NSA_GROUP, NSA_KV_HEADS, NSA_HEADS
    t0 = pl.program_id(1) * tq
    nc = seq // CMP_STRIDE
    ns = seq // SEL_BLOCK
    gm = R * tq
    tok = (t0 + lax.broadcasted_iota(jnp.int32, (tq, 1), 0)).astype(f32)
    gate = gate_ref[0]
    lane = lax.broadcasted_iota(jnp.int32, (1, PAIR), 1)
    first = lane < HEAD_DIM

    tl = t0 + lax.broadcasted_iota(jnp.int32, (8, tq), 1)
    rid = lax.broadcasted_iota(jnp.int32, (8, tq), 0)
    alibi_rows = jnp.where(rid == 0, ((tl // SEL_BLOCK) * SEL_BLOCK).astype(f32),
                           jnp.where(rid == 1, (tl % SEL_BLOCK).astype(f32),
                                     jnp.where(rid == 2, float(SEL_BLOCK), jnp.where(rid == 3, 1.0, 0.0))))
    aux0 = jnp.concatenate([jnp.zeros((ns, tq), f32), alibi_rows, jnp.zeros((PAIR - ns - 8, tq), f32)], axis=0).T

    def augment(q8, aux_by_group):
        aux8 = jnp.concatenate([aux_by_group[h // R] * jnp.where(lane < ns, 1.0, 2.0 ** (-(h + 1)))
                                for h in range(H)], axis=0)
        return jnp.concatenate([q8, aux8.astype(bf16)], axis=1)

    def per_head(a):
        return a.reshape(H, tq, a.shape[-1])

    def with_ones(v):
        one = jnp.ones((), v.dtype)
        return jnp.where(first, v, one), jnp.where(first, one, v)

    def split_sum(acc):
        top, bot = acc[:gm], acc[gm:]
        return jnp.concatenate([top / top[:, HEAD_DIM:HEAD_DIM + 1], bot / bot[:, 0:1]], axis=0)

    q8 = q_ref[0].reshape(H * tq, PAIR)
    qa = augment(q8, [aux0, aux0])

    kc_aug = jnp.concatenate([kc_ref[0], auxc_ref[...]], axis=1)
    sc = per_head(_dot_nt(qa, kc_aug))
    cend = (lax.broadcasted_iota(jnp.int32, (1, nc), 1) * CMP_STRIDE + (CMP_LEN - 1)).astype(f32)
    vis_c = tok >= cend
    any_c = (tok >= float(CMP_LEN - 1)).astype(f32)
    s = jnp.where(vis_c[None], sc, NEG_INF)
    e = jnp.exp(s - jnp.max(s, axis=-1, keepdims=True))
    p_cmp = e * (any_c[None] / jnp.sum(e, axis=-1, keepdims=True))
    o_cmp = _dot(p_cmp.reshape(H * tq, nc), vc_ref[0])

    jrow = lax.broadcasted_iota(jnp.int32, (ns, tq), 0)
    cur = (t0 + lax.broadcasted_iota(jnp.int32, (ns, tq), 1)) // SEL_BLOCK
    forced = (jrow == 0) | (jrow == cur) | (jrow == cur - 1)
    aux_sel = []
    for g in range(G):
        psum = p_cmp[R * g]
        for r in range(1, R):
            psum = psum + p_cmp[R * g + r]
        imp_t = lax.dot_general(selT_ref[...], psum.astype(bf16), (((1,), (1,)), ((), ())),
                                preferred_element_type=f32)
        p_lo = (psum - psum.astype(bf16).astype(f32)).astype(bf16)
        imp_t = imp_t + lax.dot_general(selT_ref[...], p_lo, (((1,), (1,)), ((), ())),
                                        preferred_element_type=f32)
        score = jnp.where(forced, FORCE_SCORE, jnp.where(jrow <= cur, imp_t, -1.0))
        cnt = jnp.zeros((ns, tq), f32)
        for i in range(ns):
            si = score[i:i + 1, :]
            cnt = cnt + jnp.where(jrow > i, jnp.where(si >= score, 1.0, 0.0), jnp.where(si > score, 1.0, 0.0))
        bias_t = jnp.where(cnt < float(SEL_TOPK), 0.0, NEG_INF)
        aux_sel.append(aux0 + jnp.concatenate([bias_t, jnp.zeros((PAIR - ns, tq), f32)], axis=0).T)
    qs = augment(q8, aux_sel)

    m_scr[...] = jnp.full(m_scr.shape, NEG_INF, f32)
    acc_scr[...] = jnp.zeros(acc_scr.shape, f32)

    def sel_tile(kt, diag):
        k0 = pl.multiple_of(kt * tk, tk)
        k_aug = jnp.concatenate([ks_ref[0, pl.ds(k0, tk), :], auxk_ref[pl.ds(k0, tk), :]], axis=1)
        v0, v1 = with_ones(vs_ref[0, pl.ds(k0, tk), :])
        s8 = _dot_nt(qs, k_aug)
        if diag:
            kpos = (k0 + lax.broadcasted_iota(jnp.int32, (1, tk), 1)).astype(f32)
            s8 = jnp.where((tok >= kpos)[None], per_head(s8), NEG_INF).reshape(H * tq, tk)
        m_prev = m_scr[...][:, 0:1]
        m_new = jnp.maximum(m_prev, jnp.max(s8, axis=-1, keepdims=True))
        p = jnp.exp(s8 - m_new).astype(bf16)
        pv = jnp.concatenate([jnp.dot(p[:gm], v0, preferred_element_type=f32),
                              jnp.dot(p[gm:], v1, preferred_element_type=f32)], axis=0)
        acc_scr[...] = jnp.exp(m_prev - m_new) * acc_scr[...] + pv
        m_scr[...] = jnp.broadcast_to(m_new, m_scr.shape)

    n_full = t0 // tk

    def full_step(kt, carry):
        sel_tile(kt, False)
        return carry

    lax.fori_loop(0, n_full, full_step, 0)
    sel_tile(n_full, True)
    o_sel = split_sum(acc_scr[...])

    w0 = pl.multiple_of(jnp.maximum(t0 - WINDOW, 0), tq)
    wrows = pl.ds(w0, WINDOW + tq)
    kw_aug = jnp.concatenate([kw_ref[0, wrows, :], auxk_ref[wrows, :]], axis=1)
    v0, v1 = with_ones(vw_ref[0, wrows, :])
    sw = per_head(_dot_nt(qa, kw_aug))
    kpos = w0 + lax.broadcasted_iota(jnp.int32, (1, WINDOW + tq), 1)
    dist_w = tok - kpos.astype(f32)
    vis_w = jnp.abs(dist_w - (WINDOW - 1) / 2.0) < WINDOW / 2.0
    s = jnp.where(vis_w[None], sw, NEG_INF)
    e = jnp.exp(s - jnp.max(s, axis=-1, keepdims=True)).astype(bf16).reshape(H * tq, WINDOW + tq)
    o_win = split_sum(jnp.concatenate([jnp.dot(e[:gm], v0, preferred_element_type=f32),
                                       jnp.dot(e[gm:], v1, preferred_element_type=f32)], axis=0))

    o_cmp, o_sel, o_win = per_head(o_cmp), per_head(o_sel), per_head(o_win)
    for r in range(R):
        pair = []
        for g in range(G):
            h = R * g + r
            pair.append(gate[:, 3 * h:3 * h + 1] * o_cmp[h] + gate[:, 3 * h + 1:3 * h + 2] * o_sel[h]
                        + gate[:, 3 * h + 2:3 * h + 3] * o_win[h])
        o_ref[0, :, r * PAIR:(r + 1) * PAIR] = jnp.where(first, pair[0], pair[1])


def _nsa3(q, kc, vc, ks, vs, kw, vw, gates, sel_t, tq, tk):
    B, H, S, _ = q.shape
    nc, ns = S // CMP_STRIDE, S // SEL_BLOCK
    aux_k = _alibi_key_columns(np.arange(S), ns, onehot=True)
    aux_c = _alibi_key_columns(np.arange(nc) * CMP_STRIDE + (CMP_LEN - 1), ns, onehot=False)
    full = lambda n: pl.BlockSpec((1, n, PAIR), lambda b, i: (b, 0, 0))
    const = lambda a: pl.BlockSpec(a.shape, lambda b, i: (0, 0))
    body = functools.partial(_nsa3_body, tq=tq, tk=tk, seq=S)
    return pl.pallas_call(
        body,
        grid=(B, S // tq),
        in_specs=[pl.BlockSpec((1, H, tq, PAIR), lambda b, i: (b, 0, i, 0)),
                  full(nc), full(nc), full(S), full(S), full(S), full(S),
                  pl.BlockSpec((1, tq, 128), lambda b, i: (b, i, 0)),
                  const(sel_t), const(aux_k), const(aux_c)],
        out_specs=pl.BlockSpec((1, tq, NSA_GROUP * PAIR), lambda b, i: (b, i, 0)),
        out_shape=jax.ShapeDtypeStruct((B, S, NSA_GROUP * PAIR), f32),
        scratch_shapes=[pltpu.VMEM((H * tq, PAIR), f32)] * 2,
        compiler_params=pltpu.CompilerParams(
            dimension_semantics=("arbitrary", "arbitrary"), vmem_limit_bytes=VMEM_LIMIT),
        name="nsa",
    )(q, kc, vc, ks, vs, kw, vw, gates, sel_t, aux_k, aux_c)


def _rwkv_prep_body(z_ref, zp_ref, mu_ref, w0_ref, wl_ref, a0_ref, al_ref, gl_ref, kk_ref, ka_ref, rk_ref,
                    bd_ref, r_out, lw_out, k_out, v_out, kk_out, b_out, g_out, bv_out):
    i = pl.program_id(1)
    z = z_ref[0]
    tm = z.shape[0]
    prev_row = jnp.where(i > 0, zp_ref[0, 7:8, :], 0.0)
    row = lax.broadcasted_iota(jnp.int32, (tm, 1), 0)
    z_prev = jnp.where(row == 0, prev_row, pltpu.roll(z, 1, axis=0))
    zs = z + (z_prev - z) * mu_ref[...]
    r = zs[:, 0:512]
    k = zs[:, 512:1024]
    v = zs[:, 1024:1536]
    lora = zs[:, 1536:1664]
    gd = zs[:, 1664:1792]
    wlog = w0_ref[...] + _dot(jnp.tanh(lora), wl_ref[...])
    sp = jnp.maximum(-wlog, 0.0) + jnp.log(1.0 + jnp.exp(-jnp.abs(wlog)))
    lw = -jnp.exp(-sp - 0.5)
    a = _sigmoid(a0_ref[...] + _dot(lora, al_ref[...]))
    g = _dot(_sigmoid(gd), gl_ref[...])
    kk = k * kk_ref[...]
    ss = _split_dot(kk * kk, bd_ref[...])
    kk = kk * lax.rsqrt(jnp.maximum(ss, 1e-24))
    k2 = k * (1.0 + (a - 1.0) * ka_ref[...])
    bonus = _split_dot(r * k2 * rk_ref[...], bd_ref[...])
    r_out[0] = r
    lw_out[0] = lw
    k_out[0] = k2
    v_out[0] = v
    kk_out[0] = kk
    b_out[0] = kk * a
    g_out[0] = g
    bv_out[0] = bonus * v


def _rwkv_prep(z, params, tm):
    B, S, W = z.shape
    tok = lambda w: pl.BlockSpec((1, tm, w), lambda b, i: (b, i, 0))
    const = lambda a: pl.BlockSpec(a.shape, lambda b, i: (0,) * a.ndim)
    prev = pl.BlockSpec((1, 8, W), lambda b, i: (b, jnp.maximum(i * (tm // 8) - 1, 0), 0))
    return pl.pallas_call(
        _rwkv_prep_body,
        grid=(B, S // tm),
        in_specs=[tok(W), prev] + [const(a) for a in params],
        out_specs=[tok(RWKV_WIDTH)] * 8,
        out_shape=[jax.ShapeDtypeStruct((B, S, RWKV_WIDTH), f32)] * 8,
        compiler_params=pltpu.CompilerParams(
            dimension_semantics=("arbitrary", "arbitrary"), vmem_limit_bytes=VMEM_LIMIT),
        name="rwkv_prep",
    )(z, z, *params)


def _rwkv_scan_body(r_ref, lw_ref, k_ref, v_ref, kk_ref, b_ref, g_ref, bv_ref, lnw_ref, lnb_ref, y_ref,
                    s_scr, *, tt):
    C = CHUNK

    @pl.when(pl.program_id(1) == 0)
    def _():
        s_scr[...] = jnp.zeros(s_scr.shape, f32)

    ri = lax.broadcasted_iota(jnp.int32, (C, C), 0)
    ci = lax.broadcasted_iota(jnp.int32, (C, C), 1)
    tri_incl = ri >= ci
    tri_strict = ri > ci
    cum_mat = jnp.where(tri_incl, 1.0, 0.0)
    eye = jnp.where(ri == ci, 1.0, 0.0)
    lane = lax.broadcasted_iota(jnp.int32, (1, PAIR), 1)
    pr = lax.broadcasted_iota(jnp.int32, (PAIR, PAIR), 0) // HEAD_DIM
    pc = lax.broadcasted_iota(jnp.int32, (PAIR, PAIR), 1) // HEAD_DIM
    blockdiag = pr == pc
    head_mean = jnp.where(blockdiag, 1.0, 0.0).astype(bf16)
    first = lane < HEAD_DIM

    def chunk(c, carry):
        rows = pl.ds(pl.multiple_of(c * C, C), C)
        for p in range(RWKV_WIDTH // PAIR):
            cols = slice(p * PAIR, (p + 1) * PAIR)
            r_ = r_ref[0, rows, cols]
            lw_ = lw_ref[0, rows, cols]
            k_ = k_ref[0, rows, cols]
            v_ = v_ref[0, rows, cols]
            kk_ = kk_ref[0, rows, cols]
            b_ = b_ref[0, rows, cols]
            cum = jnp.dot(cum_mat, lw_, preferred_element_type=f32, precision=lax.Precision.HIGHEST)
            cum_end = cum[C - 1:C, :]
            e_neg = jnp.exp(-cum)
            e_end = jnp.exp(cum_end - cum)
            a_t = -kk_ * jnp.exp(cum - lw_)
            r_t = r_ * jnp.exp(cum)
            b_t = b_ * e_neg
            k_t = k_ * e_neg
            b_h = b_ * e_end
            k_h = k_ * e_end
            w_end = jnp.exp(cum_end)
            lhs = jnp.concatenate([a_t, r_t], axis=0)
            ua, w2, rq, yin = [], [], [], []
            for h in range(2):
                mine = first if h == 0 else jnp.logical_not(first)
                lhs_h = jnp.where(mine, lhs, 0.0)
                ab = _dot_nt(lhs_h, b_t)
                ak = _dot_nt(lhs_h, k_t)
                a_ab = jnp.where(tri_strict, ab[:C], 0.0)
                a_ak = jnp.where(tri_strict, ak[:C], 0.0)
                a_rb = jnp.where(tri_incl, ab[C:], 0.0)
                a_rk = jnp.where(tri_incl, ak[C:], 0.0)
                tinv = eye + a_ab
                pw = a_ab
                for _ in range(5):
                    pw = _dot(pw, pw)
                    tinv = tinv + _dot(tinv, pw)
                x1 = _dot(jnp.concatenate([a_ak, a_rk], axis=0), v_)
                tx = _dot(tinv, jnp.concatenate([a_t, x1[:C]], axis=1))
                ex = _dot(a_rb, tx)
                ua.append(tx[:, :PAIR])
                w2.append(tx[:, PAIR:])
                rq.append(r_t + ex[:, :PAIR])
                yin.append(x1[C:] + ex[:, PAIR:])
            ua = jnp.where(first, ua[0], ua[1])
            w2 = jnp.where(first, w2[0], w2[1])
            rq = jnp.where(first, rq[0], rq[1])
            yin = jnp.where(first, yin[0], yin[1])
            s0 = s_scr[p]
            y = _dot_nt(rq, s0) + yin
            gq = _dot_tn(jnp.concatenate([ua, w2], axis=1), b_h)
            gmat = jnp.where(blockdiag, gq[:PAIR], 0.0)
            qmat = jnp.where(blockdiag, gq[PAIR:] + _dot_tn(v_, k_h), 0.0)
            s_scr[p] = s0 * w_end + _dot(s0, gmat) + qmat
            mu = _split_dot(y, head_mean) * (1.0 / HEAD_DIM)
            d = y - mu
            var = _split_dot(d * d, head_mean) * (1.0 / HEAD_DIM)
            yn = d * lax.rsqrt(var + GN_EPS) * lnw_ref[:, cols] + lnb_ref[:, cols]
            y_ref[0, rows, cols] = (yn + bv_ref[0, rows, cols]) * g_ref[0, rows, cols]
        return carry

    lax.fori_loop(0, tt // C, chunk, 0)


def _rwkv_scan(r, lw, k, v, kk, b, g, bv, lnw, lnb, tt):
    B, S, W = r.shape
    tok = pl.BlockSpec((1, tt, W), lambda bb, i: (bb, i, 0))
    const = pl.BlockSpec((1, W), lambda bb, i: (0, 0))
    return pl.pallas_call(
        functools.partial(_rwkv_scan_body, tt=tt),
        grid=(B, S // tt),
        in_specs=[tok] * 8 + [const, const],
        out_specs=tok,
        out_shape=jax.ShapeDtypeStruct((B, S, W), f32),
        scratch_shapes=[pltpu.VMEM((W // PAIR, PAIR, PAIR), f32)],
        compiler_params=pltpu.CompilerParams(
            dimension_semantics=("arbitrary", "arbitrary"), vmem_limit_bytes=VMEM_LIMIT),
        name="rwkv_scan",
    )(r, lw, k, v, kk, b, g, bv, lnw, lnb)


SCAN_GROUP = 2


def _rwkv_body(z_ref, zp_ref, mu_ref, w0_ref, wl_ref, a0_ref, al_ref, gl_ref, kkw_ref, ka_ref, rk_ref, bd_ref,
               lnw_ref, lnb_ref, y_ref, s_scr, r_s, lw_s, k_s, v_s, kk_s, b_s, g_s, bv_s, *, tt):
    C = CHUNK
    n_pairs = RWKV_WIDTH // PAIR
    i = pl.program_id(1)

    @pl.when(i == 0)
    def _():
        s_scr[...] = jnp.zeros(s_scr.shape, f32)

    z = z_ref[0]
    prev_row = jnp.where(i > 0, zp_ref[0, 7:8, :], 0.0)
    row = lax.broadcasted_iota(jnp.int32, (tt, 1), 0)
    z_prev = jnp.where(row == 0, prev_row, pltpu.roll(z, 1, axis=0))
    zs = z + (z_prev - z) * mu_ref[...]
    r = zs[:, 0:512]
    k = zs[:, 512:1024]
    v = zs[:, 1024:1536]
    lora = zs[:, 1536:1664]
    gd = zs[:, 1664:1792]
    wlog = w0_ref[...] + _dot(jnp.tanh(lora), wl_ref[...])
    sp = jnp.maximum(-wlog, 0.0) + jnp.log(1.0 + jnp.exp(-jnp.abs(wlog)))
    a = _sigmoid(a0_ref[...] + _dot(lora, al_ref[...]))
    kk = k * kkw_ref[...]
    kk = kk * lax.rsqrt(jnp.maximum(_split_dot(kk * kk, bd_ref[...]), 1e-24))
    k2 = k * (1.0 + (a - 1.0) * ka_ref[...])
    r_s[...] = r
    lw_s[...] = -jnp.exp(-sp - 0.5)
    k_s[...] = k2
    v_s[...] = v
    kk_s[...] = kk
    b_s[...] = kk * a
    g_s[...] = _dot(_sigmoid(gd), gl_ref[...])
    bv_s[...] = _split_dot(r * k2 * rk_ref[...], bd_ref[...]) * v

    ri = lax.broadcasted_iota(jnp.int32, (C, C), 0)
    ci = lax.broadcasted_iota(jnp.int32, (C, C), 1)
    tri_incl = ri >= ci
    tri_strict = ri > ci
    cum_mat = jnp.where(tri_incl, 1.0, 0.0)
    lane = lax.broadcasted_iota(jnp.int32, (1, PAIR), 1)
    first = lane < HEAD_DIM
    mine = (first, jnp.logical_not(first))
    blockdiag = (lax.broadcasted_iota(jnp.int32, (PAIR, PAIR), 0) // HEAD_DIM
                 == lax.broadcasted_iota(jnp.int32, (PAIR, PAIR), 1) // HEAD_DIM)
    head_mean = jnp.where(blockdiag, 1.0, 0.0).astype(bf16)

    def swap_heads(x):
        return pltpu.roll(x, HEAD_DIM, axis=1)

    def group(gi, carry):
        units = []
        for cc in range(SCAN_GROUP):
            rows = pl.ds(pl.multiple_of((gi * SCAN_GROUP + cc) * C, C), C)
            r_, lw_, k_, v_, kk_, b_ = (s[rows, :] for s in (r_s, lw_s, k_s, v_s, kk_s, b_s))
            cum = jnp.dot(cum_mat, lw_, preferred_element_type=f32, precision=lax.Precision.HIGHEST)
            cum_end = cum[C - 1:C, :]
            e_neg = jnp.exp(-cum)
            e_end = jnp.exp(cum_end - cum)
            a_t = -kk_ * jnp.exp(cum - lw_)
            r_t = r_ * jnp.exp(cum)
            b_t = b_ * e_neg
            k_t = k_ * e_neg
            b_h = b_ * e_end
            k_h = k_ * e_end
            w_end = jnp.exp(cum_end)
            for p in range(n_pairs):
                pc = slice(p * PAIR, (p + 1) * PAIR)
                units.append(dict(cc=cc, p=p, rows=rows, pc=pc, a_t=a_t[:, pc], r_t=r_t[:, pc], b_t=b_t[:, pc],
                                  k_t=k_t[:, pc], b_h=b_h[:, pc], k_h=k_h[:, pc], v=v_[:, pc], w_end=w_end[:, pc]))
        heads = [(u, h) for u in units for h in range(2)]

        for u in units:
            lhs = jnp.concatenate([u["a_t"], u["r_t"]], axis=0)
            u["ab"], u["ak"] = [], []
            for h in range(2):
                lhs_h = jnp.where(mine[h], lhs, 0.0)
                u["ab"].append(_dot_nt(lhs_h, u["b_t"]))
                u["ak"].append(_dot_nt(lhs_h, u["k_t"]))
            u["kv"] = _dot_tn(u["v"], u["k_h"])
            u["v_sw"] = swap_heads(u["v"])
        st = {}
        for u, h in heads:
            key = (u["cc"], u["p"], h)
            ab, ak = u["ab"][h], u["ak"][h]
            st[key] = dict(a=jnp.where(tri_strict, ab[:C], 0.0), a_rb=jnp.where(tri_incl, ab[C:], 0.0),
                           akrk=jnp.concatenate([jnp.where(tri_strict, ak[:C], 0.0),
                                                 jnp.where(tri_incl, ak[C:], 0.0)], axis=0))
        for u, h in heads:
            s = st[(u["cc"], u["p"], h)]
            x1 = _dot(s["akrk"], u["v_sw"])
            s["x"] = jnp.where(mine[h], u["a_t"], x1[:C])
            s["arkv"] = x1[C:]
            s["pows"] = [s["a"], _dot(s["a"], s["a"])]
        for j in range(6):
            for u, h in heads:
                s = st[(u["cc"], u["p"], h)]
                s["x"] = s["x"] + _dot(s["pows"][j], s["x"])
            if j + 2 <= 5:
                for u, h in heads:
                    s = st[(u["cc"], u["p"], h)]
                    s["pows"].append(_dot(s["pows"][j + 1], s["pows"][j + 1]))
        for u, h in heads:
            s = st[(u["cc"], u["p"], h)]
            s["ex"] = _dot(s["a_rb"], s["x"])
        for u in units:
            s0, s1 = st[(u["cc"], u["p"], 0)], st[(u["cc"], u["p"], 1)]
            ua = jnp.where(first, s0["x"], s1["x"])
            w2 = swap_heads(jnp.where(first, s1["x"], s0["x"]))
            u["rq"] = u["r_t"] + jnp.where(first, s0["ex"], s1["ex"])
            u["yin"] = swap_heads(jnp.where(first, s1["arkv"] + s1["ex"], s0["arkv"] + s0["ex"]))
            gq = _dot_tn(jnp.concatenate([ua, w2], axis=1), u["b_h"])
            u["gmat"] = jnp.where(blockdiag, gq[:PAIR], 0.0)
            u["qmat"] = jnp.where(blockdiag, gq[PAIR:] + u["kv"], 0.0)
        state = [s_scr[p] for p in range(n_pairs)]
        for u in units:
            s0 = state[u["p"]]
            u["y"] = _dot_nt(u["rq"], s0) + u["yin"]
            state[u["p"]] = s0 * u["w_end"] + _dot(s0, u["gmat"]) + u["qmat"]
        for p in range(n_pairs):
            s_scr[p] = state[p]
        for u in units:
            y = u["y"]
            mu = _split_dot(y, head_mean) * (1.0 / HEAD_DIM)
            u["d"] = y - mu
        for u in units:
            d = u["d"]
            var = _split_dot(d * d, head_mean) * (1.0 / HEAD_DIM)
            yn = d * lax.rsqrt(var + GN_EPS) * lnw_ref[:, u["pc"]] + lnb_ref[:, u["pc"]]
            y_ref[0, u["rows"], u["pc"]] = (yn + bv_s[u["rows"], u["pc"]]) * g_s[u["rows"], u["pc"]]
        return carry

    lax.fori_loop(0, tt // (C * SCAN_GROUP), group, 0)


def _rwkv(z, params, tt):
    B, S, W = z.shape
    const = lambda a: pl.BlockSpec(a.shape, lambda b, i: (0,) * a.ndim)
    prev = pl.BlockSpec((1, 8, W), lambda b, i: (b, jnp.maximum(i * (tt // 8) - 1, 0), 0))
    return pl.pallas_call(
        functools.partial(_rwkv_body, tt=tt),
        grid=(B, S // tt),
        in_specs=[pl.BlockSpec((1, tt, W), lambda b, i: (b, i, 0)), prev] + [const(a) for a in params],
        out_specs=pl.BlockSpec((1, tt, RWKV_WIDTH), lambda b, i: (b, i, 0)),
        out_shape=jax.ShapeDtypeStruct((B, S, RWKV_WIDTH), f32),
        scratch_shapes=[pltpu.VMEM((RWKV_WIDTH // PAIR, PAIR, PAIR), f32)]
        + [pltpu.VMEM((tt, RWKV_WIDTH), f32)] * 8,
        compiler_params=pltpu.CompilerParams(
            dimension_semantics=("arbitrary", "arbitrary"), vmem_limit_bytes=VMEM_LIMIT),
        name="rwkv",
    )(z, z, *params)


def _post_body(x_ref, yn_ref, yr_ref, p_ref, wo_ref, gpost_ref, gpre_ref, gmlp_ref, wup_ref, wdn_ref,
               wpg_ref, wple_ref, o_ref):
    y = jnp.concatenate([yn_ref[0], yr_ref[0]], axis=1).astype(bf16)
    mix = jnp.dot(y, wo_ref[...], preferred_element_type=f32)
    x1 = x_ref[0] + _rms(mix, gpost_ref[...])
    h = _rms(x1, gpre_ref[...]).astype(bf16)
    acc = None
    for c in range(D_FF // D_MODEL):
        cs = slice(c * D_MODEL, (c + 1) * D_MODEL)
        u = jnp.dot(h, wup_ref[:, cs], preferred_element_type=f32)
        u = jnp.square(jnp.maximum(u, 0.0)).astype(bf16)
        part = jnp.dot(u, wdn_ref[cs, :], preferred_element_type=f32)
        acc = part if acc is None else acc + part
    x2 = x1 + _rms(acc, gmlp_ref[...])
    gate = _sigmoid(jnp.dot(x2.astype(bf16), wpg_ref[...], preferred_element_type=f32))
    o_ref[0] = x2 + gate * jnp.dot(p_ref[0].astype(bf16), wple_ref[...], preferred_element_type=f32)


def _post(x, yn, yr, p, weights, tm):
    B, S, D = x.shape
    tok = lambda w: pl.BlockSpec((1, tm, w), lambda b, i: (b, i, 0))
    const = lambda a: pl.BlockSpec(a.shape, lambda b, i: (0,) * a.ndim, pipeline_mode=pl.Buffered(1))
    return pl.pallas_call(
        _post_body,
        grid=(B, S // tm),
        in_specs=[tok(D), tok(512), tok(512), tok(PLE_DIM)] + [const(a) for a in weights],
        out_specs=tok(D),
        out_shape=jax.ShapeDtypeStruct((B, S, D), f32),
        compiler_params=pltpu.CompilerParams(
            dimension_semantics=("arbitrary", "arbitrary"), vmem_limit_bytes=VMEM_LIMIT),
        name="post",
    )(x, yn, yr, p, *weights)


def _pack_inproj(w_in, gate_bias):
    wq = w_in[:, 0:512].reshape(D_MODEL, NSA_KV_HEADS, NSA_GROUP, 1, HEAD_DIM)
    onehot = jnp.eye(NSA_KV_HEADS, dtype=w_in.dtype).reshape(1, NSA_KV_HEADS, 1, NSA_KV_HEADS, 1)
    wq = (wq * onehot).reshape(D_MODEL, NSA_HEADS * PAIR)
    wkv = w_in[:, 512:1280]
    wg = jnp.pad(w_in[:, 1280:1304], ((0, 0), (0, 128 - 24)))
    wr = w_in[:, 1304:]
    wcat = jnp.concatenate([wq, wkv, wg, wr], axis=1).astype(bf16)
    return wcat, jnp.pad(gate_bias, (0, 128 - 24)).reshape(1, 128)


def _pack_compress(pe, w1, b1, w2):
    eye2 = jnp.eye(NSA_KV_HEADS, dtype=f32)
    w1r = w1.reshape(CMP_LEN, HEAD_DIM, CMP_HIDDEN)
    halves = []
    for part in (w1r[:CMP_STRIDE], w1r[CMP_STRIDE:]):
        halves.append(jnp.einsum("jdc,gh->jgdhc", part, eye2).reshape(CMP_STRIDE * PAIR, 2 * CMP_HIDDEN))
    w = jnp.stack(halves).astype(bf16)
    per = jnp.broadcast_to(pe.reshape(2, CMP_STRIDE, 1, HEAD_DIM), (2, CMP_STRIDE, NSA_KV_HEADS, HEAD_DIM))
    per = per.reshape(2, CMP_STRIDE * PAIR)
    b1p = jnp.tile(b1, NSA_KV_HEADS).reshape(1, 2 * CMP_HIDDEN)
    w2p = jnp.einsum("cd,gh->gchd", w2, eye2).reshape(2 * CMP_HIDDEN, PAIR).astype(bf16)
    return per, w, b1p, w2p


def _sel_map_t(seq):
    nc, ns = seq // CMP_STRIDE, seq // SEL_BLOCK
    c0 = np.arange(nc) * CMP_STRIDE
    s0 = np.arange(ns) * SEL_BLOCK
    ov = (np.minimum(c0[:, None] + CMP_LEN - 1, s0[None, :] + SEL_BLOCK - 1)
          - np.maximum(c0[:, None], s0[None, :]) + 1)
    m = np.clip(ov, 0, None).astype(np.float32) / CMP_STRIDE
    m[nc - 1] = 0.0
    return jnp.asarray(m.T, dtype=bf16)


def _row(a):
    return a.reshape(1, -1)


def _mixers(x, g_mix_pre, w_in, nsa_gate_bias, cmp_k, cmp_v, shift_mu, w0, w_lora_up, a0, a_lora_up, g_lora_up,
            k_k, k_a, r_k, lnx_w, lnx_b):
    B, S, _ = x.shape
    bd = jnp.asarray(np.kron(np.eye(RWKV_WIDTH // HEAD_DIM), np.ones((HEAD_DIM, HEAD_DIM))), dtype=bf16)
    wcat, gbias = _pack_inproj(w_in, nsa_gate_bias)
    q, kc, vc, ks, vs, kw, vw, gates, z = _inproj(x, _row(g_mix_pre), wcat, gbias, tm=512)
    chunks = (B, S // CMP_STRIDE, CMP_STRIDE * PAIR)
    kcmp, vcmp = _compress(kc.reshape(chunks), vc.reshape(chunks), _pack_compress(*cmp_k), _pack_compress(*cmp_v))
    y_nsa = _nsa(q, kcmp, vcmp, ks, vs, kw, vw, gates, _sel_map_t(S), tq=256, tk=512)

    wl = jnp.concatenate([w_lora_up, jnp.zeros((ICLR_LORA, RWKV_WIDTH), f32)], axis=0).astype(bf16)
    al = jnp.concatenate([jnp.zeros((DECAY_LORA, RWKV_WIDTH), f32), a_lora_up], axis=0).astype(bf16)
    rwkv_params = (_row(shift_mu), _row(w0), wl, _row(a0), al, g_lora_up.astype(bf16),
                   _row(k_k), _row(k_a), _row(r_k), bd, _row(lnx_w), _row(lnx_b))
    y_rwkv = _rwkv(z, rwkv_params, tt=256)
    return z, y_nsa, y_rwkv


def kernel(x, p, g_mix_pre, g_mix_post, g_mlp_pre, g_mlp_post, w_in, nsa_gate_bias, cmp_pe_k, cmp_k_w1, cmp_k_b1, cmp_k_w2, cmp_pe_v, cmp_v_w1, cmp_v_b1, cmp_v_w2, shift_mu, w0, w_lora_up, a0, a_lora_up, g_lora_up, k_k, k_a, r_k, lnx_w, lnx_b, w_out, w_up, w_down, w_ple, w_ple_gate):
    D = x.shape[-1]
    for i in range(p.shape[0]):
        _, y_nsa, y_rwkv = _mixers(
            x, g_mix_pre[i], w_in[i], nsa_gate_bias[i],
            (cmp_pe_k[i], cmp_k_w1[i], cmp_k_b1[i], cmp_k_w2[i]), (cmp_pe_v[i], cmp_v_w1[i], cmp_v_b1[i], cmp_v_w2[i]),
            shift_mu[i], w0[i], w_lora_up[i], a0[i], a_lora_up[i], g_lora_up[i], k_k[i], k_a[i], r_k[i],
            lnx_w[i], lnx_b[i])
        wo_nsa = w_out[i][:512].reshape(NSA_KV_HEADS, NSA_GROUP, HEAD_DIM, D).transpose(1, 0, 2, 3).reshape(512, D)
        wo = jnp.concatenate([wo_nsa, w_out[i][512:]], axis=0).astype(bf16)
        weights = (wo, _row(g_mix_post[i]), _row(g_mlp_pre[i]), _row(g_mlp_post[i]), w_up[i].astype(bf16),
                   w_down[i].astype(bf16), w_ple_gate[i].astype(bf16), w_ple[i].astype(bf16))
        x = _post(x, y_nsa, y_rwkv, p[i], weights, tm=512)
    return x
```

```python
import functools

import jax
import jax.numpy as jnp
import numpy as np
from jax import lax
from jax.experimental import pallas as pl
from jax.experimental.pallas import tpu as pltpu

f32 = jnp.float32
bf16 = jnp.bfloat16

D_MODEL = 1024
HEAD_DIM = 64
NSA_HEADS = 8
NSA_KV_HEADS = 2
NSA_GROUP = NSA_HEADS // NSA_KV_HEADS
CMP_LEN = 32
CMP_STRIDE = 16
CMP_HIDDEN = 2 * HEAD_DIM
SEL_BLOCK = 64
SEL_TOPK = 16
WINDOW = 512
RWKV_WIDTH = 512
RWKV_COLS = 1792
DECAY_LORA = 64
ICLR_LORA = 64
GATE_LORA = 128
D_FF = 4 * D_MODEL
PLE_DIM = 256
NORM_EPS = 1e-6
GN_EPS = 64e-5
NEG_INF = -1e30
FORCE_SCORE = 1e4

PAIR = 2 * HEAD_DIM
CHUNK = 64
VMEM_LIMIT = 56 * 1024 * 1024

_Q0, _KV0, _GATE0, _RW0, _WCOLS = 0, 1024, 1792, 1920, 3712


def _dot(a, b):
    return jnp.dot(a.astype(bf16), b.astype(bf16), preferred_element_type=f32)


def _dot_nt(a, b):
    return lax.dot_general(a.astype(bf16), b.astype(bf16), (((1,), (1,)), ((), ())), preferred_element_type=f32)


def _dot_tn(a, b):
    return lax.dot_general(a.astype(bf16), b.astype(bf16), (((0,), (0,)), ((), ())), preferred_element_type=f32)


def _split_dot(x, w):
    hi = x.astype(bf16)
    lo = (x - hi.astype(f32)).astype(bf16)
    return jnp.dot(hi, w, preferred_element_type=f32) + jnp.dot(lo, w, preferred_element_type=f32)


def _rms(x, g):
    ms = jnp.mean(x * x, axis=-1, keepdims=True)
    return x * lax.rsqrt(ms + NORM_EPS) * g


def _sigmoid(x):
    return 1.0 / (1.0 + jnp.exp(-x))


def _inproj_body(x_ref, g_ref, w_ref, gb_ref, q_ref, kc_ref, vc_ref, ks_ref, vs_ref, kw_ref, vw_ref,
                 gate_ref, z_ref):
    h = _rms(x_ref[0], g_ref[...]).astype(bf16)
    q = jnp.dot(h, w_ref[:, _Q0:_KV0], preferred_element_type=f32) * (HEAD_DIM ** -0.5)
    for hd in range(NSA_HEADS):
        q_ref[0, hd] = q[:, hd * PAIR:(hd + 1) * PAIR].astype(bf16)
    kv = jnp.dot(h, w_ref[:, _KV0:_GATE0], preferred_element_type=f32)
    kc_ref[0] = kv[:, 0:128]
    vc_ref[0] = kv[:, 128:256]
    ks_ref[0] = kv[:, 256:384].astype(bf16)
    vs_ref[0] = kv[:, 384:512].astype(bf16)
    kw_ref[0] = kv[:, 512:640].astype(bf16)
    vw_ref[0] = kv[:, 640:768].astype(bf16)
    gl = jnp.dot(h, w_ref[:, _GATE0:_RW0], preferred_element_type=f32)
    gate_ref[0] = _sigmoid(gl + gb_ref[...])
    z_ref[0] = jnp.dot(h, w_ref[:, _RW0:_WCOLS], preferred_element_type=f32)


def _inproj(x, g, wcat, gbias, tm):
    B, S, D = x.shape
    tok = lambda w: pl.BlockSpec((1, tm, w), lambda b, i: (b, i, 0))
    const = lambda shp: pl.BlockSpec(shp, lambda b, i: (0,) * len(shp))
    return pl.pallas_call(
        _inproj_body,
        grid=(B, S // tm),
        in_specs=[tok(D), const((1, D)), const((D, _WCOLS)), const((1, 128))],
        out_specs=[pl.BlockSpec((1, NSA_HEADS, tm, PAIR), lambda b, i: (b, 0, i, 0))]
        + [tok(128)] * 7 + [tok(RWKV_COLS)],
        out_shape=[jax.ShapeDtypeStruct((B, NSA_HEADS, S, PAIR), bf16),
                   jax.ShapeDtypeStruct((B, S, 128), f32), jax.ShapeDtypeStruct((B, S, 128), f32)]
        + [jax.ShapeDtypeStruct((B, S, 128), bf16)] * 4
        + [jax.ShapeDtypeStruct((B, S, 128), f32), jax.ShapeDtypeStruct((B, S, RWKV_COLS), f32)],
        compiler_params=pltpu.CompilerParams(
            dimension_semantics=("arbitrary", "arbitrary"), vmem_limit_bytes=VMEM_LIMIT),
        name="inproj",
    )(x, g, wcat, gbias)


def _gelu_tanh(x):
    return x * (0.5 * (1.0 + jnp.tanh(np.sqrt(2.0 / np.pi) * (x + 0.044715 * (x * x * x)))))


def _compress_one(x, pe_ref, w_ref, b1_ref, w2_ref):
    n = x.shape[0]
    lo = _dot(x + pe_ref[0:1, :], w_ref[0])
    hi = _dot(x + pe_ref[1:2, :], w_ref[1])
    pre = lo + pltpu.roll(hi, n - 1, axis=0) + b1_ref[...]
    return _dot(_gelu_tanh(pre), w2_ref[...])


def _compress_body(xk_ref, xv_ref, pek_ref, wk_ref, bk_ref, w2k_ref, pev_ref, wv_ref, bv_ref, w2v_ref,
                   kc_ref, vc_ref):
    kc_ref[0] = _compress_one(xk_ref[0], pek_ref, wk_ref, bk_ref, w2k_ref).astype(bf16)
    vc_ref[0] = _compress_one(xv_ref[0], pev_ref, wv_ref, bv_ref, w2v_ref).astype(bf16)


def _compress(xk, xv, kparams, vparams):
    B, NC, W = xk.shape
    const = lambda a: pl.BlockSpec(a.shape, lambda b: (0,) * a.ndim)
    seq = pl.BlockSpec((1, NC, W), lambda b: (b, 0, 0))
    out = pl.BlockSpec((1, NC, PAIR), lambda b: (b, 0, 0))
    return pl.pallas_call(
        _compress_body,
        grid=(B,),
        in_specs=[seq, seq] + [const(a) for a in kparams] + [const(a) for a in vparams],
        out_specs=[out, out],
        out_shape=[jax.ShapeDtypeStruct((B, NC, PAIR), bf16)] * 2,
        compiler_params=pltpu.CompilerParams(dimension_semantics=("arbitrary",), vmem_limit_bytes=VMEM_LIMIT),
        name="compress",
    )(xk, xv, *kparams, *vparams)


def _nsa_body(q_ref, kc_ref, vc_ref, ks_ref, vs_ref, kw_ref, vw_ref, gate_ref, selT_ref, o_ref,
              m_scr, l_scr, acc_scr, *, tq, tk, seq):
    R = NSA_GROUP
    t0 = pl.program_id(1) * tq
    nc = seq // CMP_STRIDE
    ns = seq // SEL_BLOCK
    tok = (t0 + lax.broadcasted_iota(jnp.int32, (tq, 1), 0)).astype(f32)
    gate = gate_ref[0]
    lane = lax.broadcasted_iota(jnp.int32, (tq, PAIR), 1)

    def slope(g, r):
        return 2.0 ** (-(R * g + r + 1))

    def rows(a, r):
        return a[r * tq:(r + 1) * tq]

    outs = [None] * R
    for g in range(NSA_KV_HEADS):
        q4 = q_ref[0, R * g:R * (g + 1)].reshape(R * tq, PAIR)

        sc = _dot_nt(q4, kc_ref[0])
        cend = (lax.broadcasted_iota(jnp.int32, (1, nc), 1) * CMP_STRIDE + (CMP_LEN - 1)).astype(f32)
        dist_c = tok - cend
        vis_c = dist_c >= 0.0
        any_c = (tok >= float(CMP_LEN - 1)).astype(f32)
        vc = vc_ref[0]
        o_cmp = []
        psum = None
        for r in range(R):
            s = jnp.where(vis_c, rows(sc, r) - slope(g, r) * dist_c, NEG_INF)
            e = jnp.exp(s - jnp.max(s, axis=-1, keepdims=True))
            p = e * (any_c / jnp.sum(e, axis=-1, keepdims=True))
            o_cmp.append(_dot(p, vc))
            psum = p if psum is None else psum + p

        imp_t = lax.dot_general(selT_ref[...], psum.astype(bf16), (((1,), (1,)), ((), ())),
                                preferred_element_type=f32)
        p_lo = (psum - psum.astype(bf16).astype(f32)).astype(bf16)
        imp_t = imp_t + lax.dot_general(selT_ref[...], p_lo, (((1,), (1,)), ((), ())),
                                        preferred_element_type=f32)
        jrow = lax.broadcasted_iota(jnp.int32, (ns, tq), 0)
        cur = (t0 + lax.broadcasted_iota(jnp.int32, (ns, tq), 1)) // SEL_BLOCK
        forced = (jrow == 0) | (jrow == cur) | (jrow == cur - 1)
        score = jnp.where(forced, FORCE_SCORE, jnp.where(jrow <= cur, imp_t, -1.0))
        cnt = jnp.zeros((ns, tq), f32)
        for i in range(ns):
            si = score[i:i + 1, :]
            cnt = cnt + jnp.where(jrow > i, jnp.where(si >= score, 1.0, 0.0), jnp.where(si > score, 1.0, 0.0))
        bias_q = jnp.where(cnt < float(SEL_TOPK), 0.0, NEG_INF).T.astype(bf16)

        m_scr[...] = jnp.full(m_scr.shape, NEG_INF, f32)
        l_scr[...] = jnp.zeros(l_scr.shape, f32)
        acc_scr[...] = jnp.zeros(acc_scr.shape, f32)

        def sel_tile(kt, diag):
            k0 = pl.multiple_of(kt * tk, tk)
            kt_ = ks_ref[0, pl.ds(k0, tk), :]
            vt_ = vs_ref[0, pl.ds(k0, tk), :]
            s4 = _dot_nt(q4, kt_)
            kpos = k0 + lax.broadcasted_iota(jnp.int32, (1, tk), 1)
            dist = tok - kpos.astype(f32)
            blk = lax.broadcasted_iota(jnp.int32, (ns, tk), 0)
            expand = jnp.where(blk == (k0 + lax.broadcasted_iota(jnp.int32, (ns, tk), 1)) // SEL_BLOCK,
                               1.0, 0.0).astype(bf16)
            pre = jnp.dot(bias_q, expand, preferred_element_type=f32)
            if diag:
                pre = jnp.where(dist >= 0.0, pre, NEG_INF)
            for r in range(R):
                s = rows(s4, r) - slope(g, r) * dist + pre
                m_prev = m_scr[r][:, 0:1]
                m_new = jnp.maximum(m_prev, jnp.max(s, axis=-1, keepdims=True))
                p = jnp.exp(s - m_new)
                alpha = jnp.exp(m_prev - m_new)
                l_scr[r] = alpha * l_scr[r] + jnp.sum(p, axis=-1, keepdims=True)
                acc_scr[r] = alpha * acc_scr[r] + _dot(p, vt_)
                m_scr[r] = jnp.broadcast_to(m_new, (tq, PAIR))

        n_full = t0 // tk

        def full_step(kt, carry):
            sel_tile(kt, False)
            return carry

        lax.fori_loop(0, n_full, full_step, 0)
        sel_tile(n_full, True)
        o_sel = [acc_scr[r] / l_scr[r] for r in range(R)]

        w0 = pl.multiple_of(jnp.maximum(t0 - WINDOW, 0), tq)
        kwt = kw_ref[0, pl.ds(w0, WINDOW + tq), :]
        vwt = vw_ref[0, pl.ds(w0, WINDOW + tq), :]
        sw = _dot_nt(q4, kwt)
        kpos = w0 + lax.broadcasted_iota(jnp.int32, (1, WINDOW + tq), 1)
        dist_w = tok - kpos.astype(f32)
        vis_w = jnp.abs(dist_w - (WINDOW - 1) / 2.0) < WINDOW / 2.0
        for r in range(R):
            s = jnp.where(vis_w, rows(sw, r) - slope(g, r) * dist_w, NEG_INF)
            e = jnp.exp(s - jnp.max(s, axis=-1, keepdims=True))
            o_win = _dot(e, vwt) / jnp.sum(e, axis=-1, keepdims=True)
            c0 = 3 * (R * g + r)
            o = (gate[:, c0:c0 + 1] * o_cmp[r] + gate[:, c0 + 1:c0 + 2] * o_sel[r]
                 + gate[:, c0 + 2:c0 + 3] * o_win)
            outs[r] = o if g == 0 else jnp.where(lane < HEAD_DIM, outs[r], o)

    for r in range(R):
        o_ref[0, :, r * PAIR:(r + 1) * PAIR] = outs[r]


def _nsa(q, kc, vc, ks, vs, kw, vw, gates, sel_t, tq, tk):
    B, H, S, _ = q.shape
    nc = S // CMP_STRIDE
    full = lambda n: pl.BlockSpec((1, n, PAIR), lambda b, i: (b, 0, 0))
    body = functools.partial(_nsa_body, tq=tq, tk=tk, seq=S)
    return pl.pallas_call(
        body,
        grid=(B, S // tq),
        in_specs=[pl.BlockSpec((1, H, tq, PAIR), lambda b, i: (b, 0, i, 0)),
                  full(nc), full(nc), full(S), full(S), full(S), full(S),
                  pl.BlockSpec((1, tq, 128), lambda b, i: (b, i, 0)),
                  pl.BlockSpec(sel_t.shape, lambda b, i: (0, 0))],
        out_specs=pl.BlockSpec((1, tq, NSA_GROUP * PAIR), lambda b, i: (b, i, 0)),
        out_shape=jax.ShapeDtypeStruct((B, S, NSA_GROUP * PAIR), f32),
        scratch_shapes=[pltpu.VMEM((NSA_GROUP, tq, PAIR), f32)] * 3,
        compiler_params=pltpu.CompilerParams(
            dimension_semantics=("arbitrary", "arbitrary"), vmem_limit_bytes=VMEM_LIMIT),
        name="nsa",
    )(q, kc, vc, ks, vs, kw, vw, gates, sel_t)


def _alibi_key_columns(pos, ns, onehot):
    a = np.zeros((pos.shape[0], PAIR), np.float32)
    if onehot:
        a[np.arange(pos.shape[0]), pos // SEL_BLOCK] = 1.0
    a[:, ns] = -1.0
    a[:, ns + 1] = -1.0
    a[:, ns + 2] = pos // SEL_BLOCK
    a[:, ns + 3] = pos % SEL_BLOCK
    return jnp.asarray(a, dtype=bf16)


def _nsa2_body(q_ref, kc_ref, vc_ref, ks_ref, vs_ref, kw_ref, vw_ref, gate_ref, selT_ref, auxk_ref, auxc_ref,
               o_ref, m_scr, l_scr, acc_scr, *, tq, tk, seq):
    R = NSA_GROUP
    t0 = pl.program_id(1) * tq
    nc = seq // CMP_STRIDE
    ns = seq // SEL_BLOCK
    tok = (t0 + lax.broadcasted_iota(jnp.int32, (tq, 1), 0)).astype(f32)
    gate = gate_ref[0]
    lane = lax.broadcasted_iota(jnp.int32, (tq, PAIR), 1)
    lane1 = lax.broadcasted_iota(jnp.int32, (1, PAIR), 1)

    def slope(g, r):
        return 2.0 ** (-(R * g + r + 1))

    def rows(a, r):
        return a[r * tq:(r + 1) * tq]

    tl = t0 + lax.broadcasted_iota(jnp.int32, (8, tq), 1)
    rid = lax.broadcasted_iota(jnp.int32, (8, tq), 0)
    alibi_rows = jnp.where(rid == 0, ((tl // SEL_BLOCK) * SEL_BLOCK).astype(f32),
                           jnp.where(rid == 1, (tl % SEL_BLOCK).astype(f32),
                                     jnp.where(rid == 2, float(SEL_BLOCK), jnp.where(rid == 3, 1.0, 0.0))))
    aux0 = jnp.concatenate([jnp.zeros((ns, tq), f32), alibi_rows, jnp.zeros((PAIR - ns - 8, tq), f32)], axis=0).T

    def augment(q4, aux, g):
        aux4 = jnp.concatenate([aux * jnp.where(lane1 < ns, 1.0, slope(g, r)) for r in range(R)], axis=0)
        return jnp.concatenate([q4, aux4.astype(bf16)], axis=1)

    kc_aug = jnp.concatenate([kc_ref[0], auxc_ref[...]], axis=1)
    outs = [None] * R
    for g in range(NSA_KV_HEADS):
        q4 = q_ref[0, R * g:R * (g + 1)].reshape(R * tq, PAIR)
        qa = augment(q4, aux0, g)

        sc = _dot_nt(qa, kc_aug)
        cend = (lax.broadcasted_iota(jnp.int32, (1, nc), 1) * CMP_STRIDE + (CMP_LEN - 1)).astype(f32)
        vis_c = tok >= cend
        any_c = (tok >= float(CMP_LEN - 1)).astype(f32)
        vc = vc_ref[0]
        o_cmp = []
        psum = None
        for r in range(R):
            s = jnp.where(vis_c, rows(sc, r), NEG_INF)
            e = jnp.exp(s - jnp.max(s, axis=-1, keepdims=True))
            p = e * (any_c / jnp.sum(e, axis=-1, keepdims=True))
            o_cmp.append(_dot(p, vc))
            psum = p if psum is None else psum + p

        imp_t = lax.dot_general(selT_ref[...], psum.astype(bf16), (((1,), (1,)), ((), ())),
                                preferred_element_type=f32)
        p_lo = (psum - psum.astype(bf16).astype(f32)).astype(bf16)
        imp_t = imp_t + lax.dot_general(selT_ref[...], p_lo, (((1,), (1,)), ((), ())),
                                        preferred_element_type=f32)
        jrow = lax.broadcasted_iota(jnp.int32, (ns, tq), 0)
        cur = (t0 + lax.broadcasted_iota(jnp.int32, (ns, tq), 1)) // SEL_BLOCK
        forced = (jrow == 0) | (jrow == cur) | (jrow == cur - 1)
        score = jnp.where(forced, FORCE_SCORE, jnp.where(jrow <= cur, imp_t, -1.0))
        cnt = jnp.zeros((ns, tq), f32)
        for i in range(ns):
            si = score[i:i + 1, :]
            cnt = cnt + jnp.where(jrow > i, jnp.where(si >= score, 1.0, 0.0), jnp.where(si > score, 1.0, 0.0))
        bias_t = jnp.where(cnt < float(SEL_TOPK), 0.0, NEG_INF)
        qs = augment(q4, aux0 + jnp.concatenate([bias_t, jnp.zeros((PAIR - ns, tq), f32)], axis=0).T, g)

        m_scr[...] = jnp.full(m_scr.shape, NEG_INF, f32)
        l_scr[...] = jnp.zeros(l_scr.shape, f32)
        acc_scr[...] = jnp.zeros(acc_scr.shape, f32)

        def sel_tile(kt, diag):
            k0 = pl.multiple_of(kt * tk, tk)
            k_aug = jnp.concatenate([ks_ref[0, pl.ds(k0, tk), :], auxk_ref[pl.ds(k0, tk), :]], axis=1)
            vt_ = vs_ref[0, pl.ds(k0, tk), :]
            if diag:
                kpos = (k0 + lax.broadcasted_iota(jnp.int32, (1, tk), 1)).astype(f32)
                causal = tok >= kpos
            for r in range(R):
                s = _dot_nt(rows(qs, r), k_aug)
                if diag:
                    s = jnp.where(causal, s, NEG_INF)
                m_prev = m_scr[r][:, 0:1]
                m_new = jnp.maximum(m_prev, jnp.max(s, axis=-1, keepdims=True))
                p = jnp.exp(s - m_new)
                alpha = jnp.exp(m_prev - m_new)
                l_scr[r] = alpha * l_scr[r] + jnp.sum(p, axis=-1, keepdims=True)
                acc_scr[r] = alpha * acc_scr[r] + _dot(p, vt_)
                m_scr[r] = jnp.broadcast_to(m_new, (tq, PAIR))

        n_full = t0 // tk

        def full_step(kt, carry):
            sel_tile(kt, False)
            return carry

        lax.fori_loop(0, n_full, full_step, 0)
        sel_tile(n_full, True)
        o_sel = [acc_scr[r] / l_scr[r] for r in range(R)]

        w0 = pl.multiple_of(jnp.maximum(t0 - WINDOW, 0), tq)
        wrows = pl.ds(w0, WINDOW + tq)
        kw_aug = jnp.concatenate([kw_ref[0, wrows, :], auxk_ref[wrows, :]], axis=1)
        vwt = vw_ref[0, wrows, :]
        sw = _dot_nt(qa, kw_aug)
        kpos = w0 + lax.broadcasted_iota(jnp.int32, (1, WINDOW + tq), 1)
        dist_w = tok - kpos.astype(f32)
        vis_w = jnp.abs(dist_w - (WINDOW - 1) / 2.0) < WINDOW / 2.0
        for r in range(R):
            s = jnp.where(vis_w, rows(sw, r), NEG_INF)
            e = jnp.exp(s - jnp.max(s, axis=-1, keepdims=True))
            o_win = _dot(e, vwt) / jnp.sum(e, axis=-1, keepdims=True)
            c0 = 3 * (R * g + r)
            o = (gate[:, c0:c0 + 1] * o_cmp[r] + gate[:, c0 + 1:c0 + 2] * o_sel[r]
                 + gate[:, c0 + 2:c0 + 3] * o_win)
            outs[r] = o if g == 0 else jnp.where(lane < HEAD_DIM, outs[r], o)

    for r in range(R):
        o_ref[0, :, r * PAIR:(r + 1) * PAIR] = outs[r]


def _nsa2(q, kc, vc, ks, vs, kw, vw, gates, sel_t, tq, tk):
    B, H, S, _ = q.shape
    nc, ns = S // CMP_STRIDE, S // SEL_BLOCK
    aux_k = _alibi_key_columns(np.arange(S), ns, onehot=True)
    aux_c = _alibi_key_columns(np.arange(nc) * CMP_STRIDE + (CMP_LEN - 1), ns, onehot=False)
    full = lambda n: pl.BlockSpec((1, n, PAIR), lambda b, i: (b, 0, 0))
    const = lambda a: pl.BlockSpec(a.shape, lambda b, i: (0, 0))
    body = functools.partial(_nsa2_body, tq=tq, tk=tk, seq=S)
    return pl.pallas_call(
        body,
        grid=(B, S // tq),
        in_specs=[pl.BlockSpec((1, H, tq, PAIR), lambda b, i: (b, 0, i, 0)),
                  full(nc), full(nc), full(S), full(S), full(S), full(S),
                  pl.BlockSpec((1, tq, 128), lambda b, i: (b, i, 0)),
                  const(sel_t), const(aux_k), const(aux_c)],
        out_specs=pl.BlockSpec((1, tq, NSA_GROUP * PAIR), lambda b, i: (b, i, 0)),
        out_shape=jax.ShapeDtypeStruct((B, S, NSA_GROUP * PAIR), f32),
        scratch_shapes=[pltpu.VMEM((NSA_GROUP, tq, PAIR), f32)] * 3,
        compiler_params=pltpu.CompilerParams(
            dimension_semantics=("arbitrary", "arbitrary"), vmem_limit_bytes=VMEM_LIMIT),
        name="nsa",
    )(q, kc, vc, ks, vs, kw, vw, gates, sel_t, aux_k, aux_c)


def _nsa3_body(q_ref, kc_ref, vc_ref, ks_ref, vs_ref, kw_ref, vw_ref, gate_ref, selT_ref, auxk_ref, auxc_ref,
               o_ref, m_scr, acc_scr, *, tq, tk, seq):
    R, G, H = NSA_GROUP, NSA_KV_HEADS, NSA_HEADS
    t0 = pl.program_id(1) * tq
    nc = seq // CMP_STRIDE
    ns = seq // SEL_BLOCK
    gm = R * tq
    tok = (t0 + lax.broadcasted_iota(jnp.int32, (tq, 1), 0)).astype(f32)
    gate = gate_ref[0]
    lane = lax.broadcasted_iota(jnp.int32, (1, PAIR), 1)
    first = lane < HEAD_DIM

    tl = t0 + lax.broadcasted_iota(jnp.int32, (8, tq), 1)
    rid = lax.broadcasted_iota(jnp.int32, (8, tq), 0)
    alibi_rows = jnp.where(rid == 0, ((tl // SEL_BLOCK) * SEL_BLOCK).astype(f32),
                           jnp.where(rid == 1, (tl % SEL_BLOCK).astype(f32),
                                     jnp.where(rid == 2, float(SEL_BLOCK), jnp.where(rid == 3, 1.0, 0.0))))
    aux0 = jnp.concatenate([jnp.zeros((ns, tq), f32), alibi_rows, jnp.zeros((PAIR - ns - 8, tq), f32)], axis=0).T

    def augment(q8, aux_by_group):
        aux8 = jnp.concatenate([aux_by_group[h // R] * jnp.where(lane < ns, 1.0, 2.0 ** (-(h + 1)))
                                for h in range(H)], axis=0)
        return jnp.concatenate([q8, aux8.astype(bf16)], axis=1)

    def per_head(a):
        return a.reshape(H, tq, a.shape[-1])

    def with_ones(v):
        one = jnp.ones((), v.dtype)
        return jnp.where(first, v, one), jnp.where(first, one, v)

    def split_sum(acc):
        top, bot = acc[:gm], acc[gm:]
        return jnp.concatenate([top / top[:, HEAD_DIM:HEAD_DIM + 1], bot / bot[:, 0:1]], axis=0)

    q8 = q_ref[0].reshape(H * tq, PAIR)
    qa = augment(q8, [aux0, aux0])

    kc_aug = jnp.concatenate([kc_ref[0], auxc_ref[...]], axis=1)
    sc = per_head(_dot_nt(qa, kc_aug))
    cend = (lax.broadcasted_iota(jnp.int32, (1, nc), 1) * CMP_STRIDE + (CMP_LEN - 1)).astype(f32)
    vis_c = tok >= cend
    any_c = (tok >= float(CMP_LEN - 1)).astype(f32)
    s = jnp.where(vis_c[None], sc, NEG_INF)
    e = jnp.exp(s - jnp.max(s, axis=-1, keepdims=True))
    p_cmp = e * (any_c[None] / jnp.sum(e, axis=-1, keepdims=True))
    o_cmp = _dot(p_cmp.reshape(H * tq, nc), vc_ref[0])

    jrow = lax.broadcasted_iota(jnp.int32, (ns, tq), 0)
    cur = (t0 + lax.broadcasted_iota(jnp.int32, (ns, tq), 1)) // SEL_BLOCK
    forced = (jrow == 0) | (jrow == cur) | (jrow == cur - 1)
    aux_sel = []
    for g in range(G):
        psum = p_cmp[R * g]
        for r in range(1, R):
            psum = psum + p_cmp[R * g + r]
        imp_t = lax.dot_general(selT_ref[...], psum.astype(bf16), (((1,), (1,)), ((), ())),
                                preferred_element_type=f32)
        p_lo = (psum - psum.astype(bf16).astype(f32)).astype(bf16)
        imp_t = imp_t + lax.dot_general(selT_ref[...], p_lo, (((1,), (1,)), ((), ())),
                                        preferred_element_type=f32)
        score = jnp.where(forced, FORCE_SCORE, jnp.where(jrow <= cur, imp_t, -1.0))
        cnt = jnp.zeros((ns, tq), f32)
        for i in range(ns):
            si = score[i:i + 1, :]
            cnt = cnt + jnp.where(jrow > i, jnp.where(si >= score, 1.0, 0.0), jnp.where(si > score, 1.0, 0.0))
        bias_t = jnp.where(cnt < float(SEL_TOPK), 0.0, NEG_INF)
        aux_sel.append(aux0 + jnp.concatenate([bias_t, jnp.zeros((PAIR - ns, tq), f32)], axis=0).T)
    qs = augment(q8, aux_sel)

    m_scr[...] = jnp.full(m_scr.shape, NEG_INF, f32)
    acc_scr[...] = jnp.zeros(acc_scr.shape, f32)

    def sel_tile(kt, diag):
        k0 = pl.multiple_of(kt * tk, tk)
        k_aug = jnp.concatenate([ks_ref[0, pl.ds(k0, tk), :], auxk_ref[pl.ds(k0, tk), :]], axis=1)
        v0, v1 = with_ones(vs_ref[0, pl.ds(k0, tk), :])
        s8 = _dot_nt(qs, k_aug)
        if diag:
            kpos = (k0 + lax.broadcasted_iota(jnp.int32, (1, tk), 1)).astype(f32)
            s8 = jnp.where((tok >= kpos)[None], per_head(s8), NEG_INF).reshape(H * tq, tk)
        m_prev = m_scr[...][:, 0:1]
        m_new = jnp.maximum(m_prev, jnp.max(s8, axis=-1, keepdims=True))
        p = jnp.exp(s8 - m_new).astype(bf16)
        pv = jnp.concatenate([jnp.dot(p[:gm], v0, preferred_element_type=f32),
                              jnp.dot(p[gm:], v1, preferred_element_type=f32)], axis=0)
        acc_scr[...] = jnp.exp(m_prev - m_new) * acc_scr[...] + pv
        m_scr[...] = jnp.broadcast_to(m_new, m_scr.shape)

    n_full = t0 // tk

    def full_step(kt, carry):
        sel_tile(kt, False)
        return carry

    lax.fori_loop(0, n_full, full_step, 0)
    sel_tile(n_full, True)
    o_sel = split_sum(acc_scr[...])

    w0 = pl.multiple_of(jnp.maximum(t0 - WINDOW, 0), tq)
    wrows = pl.ds(w0, WINDOW + tq)
    kw_aug = jnp.concatenate([kw_ref[0, wrows, :], auxk_ref[wrows, :]], axis=1)
    v0, v1 = with_ones(vw_ref[0, wrows, :])
    sw = per_head(_dot_nt(qa, kw_aug))
    kpos = w0 + lax.broadcasted_iota(jnp.int32, (1, WINDOW + tq), 1)
    dist_w = tok - kpos.astype(f32)
    vis_w = jnp.abs(dist_w - (WINDOW - 1) / 2.0) < WINDOW / 2.0
    s = jnp.where(vis_w[None], sw, NEG_INF)
    e = jnp.exp(s - jnp.max(s, axis=-1, keepdims=True)).astype(bf16).reshape(H * tq, WINDOW + tq)
    o_win = split_sum(jnp.concatenate([jnp.dot(e[:gm], v0, preferred_element_type=f32),
                                       jnp.dot(e[gm:], v1, preferred_element_type=f32)], axis=0))

    o_cmp, o_sel, o_win = per_head(o_cmp), per_head(o_sel), per_head(o_win)
    for r in range(R):
        pair = []
        for g in range(G):
            h = R * g + r
            pair.append(gate[:, 3 * h:3 * h + 1] * o_cmp[h] + gate[:, 3 * h + 1:3 * h + 2] * o_sel[h]
                        + gate[:, 3 * h + 2:3 * h + 3] * o_win[h])
        o_ref[0, :, r * PAIR:(r + 1) * PAIR] = jnp.where(first, pair[0], pair[1])


def _nsa3(q, kc, vc, ks, vs, kw, vw, gates, sel_t, tq, tk):
    B, H, S, _ = q.shape
    nc, ns = S // CMP_STRIDE, S // SEL_BLOCK
    aux_k = _alibi_key_columns(np.arange(S), ns, onehot=True)
    aux_c = _alibi_key_columns(np.arange(nc) * CMP_STRIDE + (CMP_LEN - 1), ns, onehot=False)
    full = lambda n: pl.BlockSpec((1, n, PAIR), lambda b, i: (b, 0, 0))
    const = lambda a: pl.BlockSpec(a.shape, lambda b, i: (0, 0))
    body = functools.partial(_nsa3_body, tq=tq, tk=tk, seq=S)
    return pl.pallas_call(
        body,
        grid=(B, S // tq),
        in_specs=[pl.BlockSpec((1, H, tq, PAIR), lambda b, i: (b, 0, i, 0)),
                  full(nc), full(nc), full(S), full(S), full(S), full(S),
                  pl.BlockSpec((1, tq, 128), lambda b, i: (b, i, 0)),
                  const(sel_t), const(aux_k), const(aux_c)],
        out_specs=pl.BlockSpec((1, tq, NSA_GROUP * PAIR), lambda b, i: (b, i, 0)),
        out_shape=jax.ShapeDtypeStruct((B, S, NSA_GROUP * PAIR), f32),
        scratch_shapes=[pltpu.VMEM((H * tq, PAIR), f32)] * 2,
        compiler_params=pltpu.CompilerParams(
            dimension_semantics=("arbitrary", "arbitrary"), vmem_limit_bytes=VMEM_LIMIT),
        name="nsa",
    )(q, kc, vc, ks, vs, kw, vw, gates, sel_t, aux_k, aux_c)


def _nsa4_body(q_ref, kc_ref, vc_ref, ks_ref, vs_ref, kw_ref, vw_ref, gate_ref, selT_ref, auxk_ref, auxc_ref,
               o_ref, m_scr, acc_scr, sa_scr, sb_scr, *, tq, tk, seq):
    R, G, H = NSA_GROUP, NSA_KV_HEADS, NSA_HEADS
    t0 = pl.program_id(1) * tq
    nc = seq // CMP_STRIDE
    ns = seq // SEL_BLOCK
    gm = R * tq
    tok = (t0 + lax.broadcasted_iota(jnp.int32, (tq, 1), 0)).astype(f32)
    lane = lax.broadcasted_iota(jnp.int32, (1, PAIR), 1)
    first = lane < HEAD_DIM

    tl = t0 + lax.broadcasted_iota(jnp.int32, (8, tq), 1)
    rid = lax.broadcasted_iota(jnp.int32, (8, tq), 0)
    alibi_rows = jnp.where(rid == 0, ((tl // SEL_BLOCK) * SEL_BLOCK).astype(f32),
                           jnp.where(rid == 1, (tl % SEL_BLOCK).astype(f32),
                                     jnp.where(rid == 2, float(SEL_BLOCK), jnp.where(rid == 3, 1.0, 0.0))))
    aux0 = jnp.concatenate([jnp.zeros((ns, tq), f32), alibi_rows, jnp.zeros((PAIR - ns - 8, tq), f32)], axis=0).T

    def augment(q8, aux_by_group):
        aux8 = jnp.concatenate([aux_by_group[h // R] * jnp.where(lane < ns, 1.0, 2.0 ** (-(h + 1)))
                                for h in range(H)], axis=0)
        return jnp.concatenate([q8, aux8.astype(bf16)], axis=1)

    def per_head(a):
        return a.reshape(H, tq, a.shape[-1])

    def with_ones(v):
        one = jnp.ones((), v.dtype)
        return jnp.where(first, v, one), jnp.where(first, one, v)

    def weighted_values(p, v):
        v0, v1 = with_ones(v)
        return jnp.concatenate([jnp.dot(p[:gm], v0, preferred_element_type=f32),
                                jnp.dot(p[gm:], v1, preferred_element_type=f32)], axis=0)

    def split_sum(acc):
        top, bot = acc[:gm], acc[gm:]
        return jnp.concatenate([top / top[:, HEAD_DIM:HEAD_DIM + 1], bot / bot[:, 0:1]], axis=0)

    q8 = q_ref[0].reshape(H * tq, PAIR)
    qa = augment(q8, [aux0, aux0])

    kc_aug = jnp.concatenate([kc_ref[0], auxc_ref[...]], axis=1)
    sc = per_head(_dot_nt(qa, kc_aug))
    w0 = pl.multiple_of(jnp.maximum(t0 - WINDOW, 0), tq)
    wrows = pl.ds(w0, WINDOW + tq)
    kw_aug = jnp.concatenate([kw_ref[0, wrows, :], auxk_ref[wrows, :]], axis=1)
    sw = per_head(_dot_nt(qa, kw_aug))

    cend = (lax.broadcasted_iota(jnp.int32, (1, nc), 1) * CMP_STRIDE + (CMP_LEN - 1)).astype(f32)
    vis_c = tok >= cend
    any_c = (tok >= float(CMP_LEN - 1)).astype(f32)
    s = jnp.where(vis_c[None], sc, NEG_INF)
    e = jnp.exp(s - jnp.max(s, axis=-1, keepdims=True))
    p_cmp = e * (any_c[None] / jnp.sum(e, axis=-1, keepdims=True))
    o_cmp = _dot(p_cmp.reshape(H * tq, nc), vc_ref[0])

    kpos = w0 + lax.broadcasted_iota(jnp.int32, (1, WINDOW + tq), 1)
    dist_w = tok - kpos.astype(f32)
    vis_w = jnp.abs(dist_w - (WINDOW - 1) / 2.0) < WINDOW / 2.0
    s = jnp.where(vis_w[None], sw, NEG_INF)
    e = jnp.exp(s - jnp.max(s, axis=-1, keepdims=True)).astype(bf16).reshape(H * tq, WINDOW + tq)
    acc_win = weighted_values(e, vw_ref[0, wrows, :])

    jrow = lax.broadcasted_iota(jnp.int32, (ns, tq), 0)
    cur = (t0 + lax.broadcasted_iota(jnp.int32, (ns, tq), 1)) // SEL_BLOCK
    forced = (jrow == 0) | (jrow == cur) | (jrow == cur - 1)
    aux_sel = []
    for g in range(G):
        psum = p_cmp[R * g]
        for r in range(1, R):
            psum = psum + p_cmp[R * g + r]
        imp_t = lax.dot_general(selT_ref[...], psum.astype(bf16), (((1,), (1,)), ((), ())),
                                preferred_element_type=f32)
        p_lo = (psum - psum.astype(bf16).astype(f32)).astype(bf16)
        imp_t = imp_t + lax.dot_general(selT_ref[...], p_lo, (((1,), (1,)), ((), ())),
                                        preferred_element_type=f32)
        score = jnp.where(forced, FORCE_SCORE, jnp.where(jrow <= cur, imp_t, -1.0))
        cnt = jnp.zeros((ns, tq), f32)
        for i in range(ns):
            si = score[i:i + 1, :]
            cnt = cnt + jnp.where(jrow > i, jnp.where(si >= score, 1.0, 0.0), jnp.where(si > score, 1.0, 0.0))
        bias_t = jnp.where(cnt < float(SEL_TOPK), 0.0, NEG_INF)
        aux_sel.append(aux0 + jnp.concatenate([bias_t, jnp.zeros((PAIR - ns, tq), f32)], axis=0).T)
    qs = augment(q8, aux_sel)

    m_scr[...] = jnp.full(m_scr.shape, NEG_INF, f32)
    acc_scr[...] = jnp.zeros(acc_scr.shape, f32)

    def key_rows(kt):
        return pl.ds(pl.multiple_of(kt * tk, tk), tk)

    def scores(kt, dst):
        rows_ = key_rows(kt)
        dst[...] = _dot_nt(qs, jnp.concatenate([ks_ref[0, rows_, :], auxk_ref[rows_, :]], axis=1))

    def attend(src, kt, diag):
        s8 = src[...]
        if diag:
            kpos = (kt * tk + lax.broadcasted_iota(jnp.int32, (1, tk), 1)).astype(f32)
            s8 = jnp.where((tok >= kpos)[None], per_head(s8), NEG_INF).reshape(H * tq, tk)
        m_prev = m_scr[...][:, 0:1]
        m_new = jnp.maximum(m_prev, jnp.max(s8, axis=-1, keepdims=True))
        p = jnp.exp(s8 - m_new).astype(bf16)
        acc_scr[...] = jnp.exp(m_prev - m_new) * acc_scr[...] + weighted_values(p, vs_ref[0, key_rows(kt), :])
        m_scr[...] = jnp.broadcast_to(m_new, m_scr.shape)

    n_full = t0 // tk
    scores(0, sa_scr)

    def two_tiles(j, carry):
        scores(2 * j + 1, sb_scr)
        attend(sa_scr, 2 * j, False)
        scores(2 * j + 2, sa_scr)
        attend(sb_scr, 2 * j + 1, False)
        return carry

    lax.fori_loop(0, n_full // 2, two_tiles, 0)

    @pl.when(n_full % 2 == 1)
    def _():
        scores(n_full, sb_scr)
        attend(sa_scr, n_full - 1, False)
        attend(sb_scr, n_full, True)

    @pl.when(n_full % 2 == 0)
    def _():
        attend(sa_scr, n_full, True)

    o_sel = split_sum(acc_scr[...])
    o_win = split_sum(acc_win)

    gate = gate_ref[0]
    o_cmp, o_sel, o_win = per_head(o_cmp), per_head(o_sel), per_head(o_win)
    for r in range(R):
        pair = []
        for g in range(G):
            h = R * g + r
            pair.append(gate[:, 3 * h:3 * h + 1] * o_cmp[h] + gate[:, 3 * h + 1:3 * h + 2] * o_sel[h]
                        + gate[:, 3 * h + 2:3 * h + 3] * o_win[h])
        o_ref[0, :, r * PAIR:(r + 1) * PAIR] = jnp.where(first, pair[0], pair[1])


def _nsa4(q, kc, vc, ks, vs, kw, vw, gates, sel_t, tq, tk):
    B, H, S, _ = q.shape
    nc, ns = S // CMP_STRIDE, S // SEL_BLOCK
    aux_k = _alibi_key_columns(np.arange(S), ns, onehot=True)
    aux_c = _alibi_key_columns(np.arange(nc) * CMP_STRIDE + (CMP_LEN - 1), ns, onehot=False)
    full = lambda n: pl.BlockSpec((1, n, PAIR), lambda b, i: (b, 0, 0))
    const = lambda a: pl.BlockSpec(a.shape, lambda b, i: (0, 0))
    body = functools.partial(_nsa4_body, tq=tq, tk=tk, seq=S)
    return pl.pallas_call(
        body,
        grid=(B, S // tq),
        in_specs=[pl.BlockSpec((1, H, tq, PAIR), lambda b, i: (b, 0, i, 0)),
                  full(nc), full(nc), full(S), full(S), full(S), full(S),
                  pl.BlockSpec((1, tq, 128), lambda b, i: (b, i, 0)),
                  const(sel_t), const(aux_k), const(aux_c)],
        out_specs=pl.BlockSpec((1, tq, NSA_GROUP * PAIR), lambda b, i: (b, i, 0)),
        out_shape=jax.ShapeDtypeStruct((B, S, NSA_GROUP * PAIR), f32),
        scratch_shapes=[pltpu.VMEM((H * tq, PAIR), f32)] * 2 + [pltpu.VMEM((H * tq, tk), f32)] * 2,
        compiler_params=pltpu.CompilerParams(
            dimension_semantics=("arbitrary", "arbitrary"), vmem_limit_bytes=VMEM_LIMIT),
        name="nsa",
    )(q, kc, vc, ks, vs, kw, vw, gates, sel_t, aux_k, aux_c)


def _rwkv_prep_body(z_ref, zp_ref, mu_ref, w0_ref, wl_ref, a0_ref, al_ref, gl_ref, kk_ref, ka_ref, rk_ref,
                    bd_ref, r_out, lw_out, k_out, v_out, kk_out, b_out, g_out, bv_out):
    i = pl.program_id(1)
    z = z_ref[0]
    tm = z.shape[0]
    prev_row = jnp.where(i > 0, zp_ref[0, 7:8, :], 0.0)
    row = lax.broadcasted_iota(jnp.int32, (tm, 1), 0)
    z_prev = jnp.where(row == 0, prev_row, pltpu.roll(z, 1, axis=0))
    zs = z + (z_prev - z) * mu_ref[...]
    r = zs[:, 0:512]
    k = zs[:, 512:1024]
    v = zs[:, 1024:1536]
    lora = zs[:, 1536:1664]
    gd = zs[:, 1664:1792]
    wlog = w0_ref[...] + _dot(jnp.tanh(lora), wl_ref[...])
    sp = jnp.maximum(-wlog, 0.0) + jnp.log(1.0 + jnp.exp(-jnp.abs(wlog)))
    lw = -jnp.exp(-sp - 0.5)
    a = _sigmoid(a0_ref[...] + _dot(lora, al_ref[...]))
    g = _dot(_sigmoid(gd), gl_ref[...])
    kk = k * kk_ref[...]
    ss = _split_dot(kk * kk, bd_ref[...])
    kk = kk * lax.rsqrt(jnp.maximum(ss, 1e-24))
    k2 = k * (1.0 + (a - 1.0) * ka_ref[...])
    bonus = _split_dot(r * k2 * rk_ref[...], bd_ref[...])
    r_out[0] = r
    lw_out[0] = lw
    k_out[0] = k2
    v_out[0] = v
    kk_out[0] = kk
    b_out[0] = kk * a
    g_out[0] = g
    bv_out[0] = bonus * v


def _rwkv_prep(z, params, tm):
    B, S, W = z.shape
    tok = lambda w: pl.BlockSpec((1, tm, w), lambda b, i: (b, i, 0))
    const = lambda a: pl.BlockSpec(a.shape, lambda b, i: (0,) * a.ndim)
    prev = pl.BlockSpec((1, 8, W), lambda b, i: (b, jnp.maximum(i * (tm // 8) - 1, 0), 0))
    return pl.pallas_call(
        _rwkv_prep_body,
        grid=(B, S // tm),
        in_specs=[tok(W), prev] + [const(a) for a in params],
        out_specs=[tok(RWKV_WIDTH)] * 8,
        out_shape=[jax.ShapeDtypeStruct((B, S, RWKV_WIDTH), f32)] * 8,
        compiler_params=pltpu.CompilerParams(
            dimension_semantics=("arbitrary", "arbitrary"), vmem_limit_bytes=VMEM_LIMIT),
        name="rwkv_prep",
    )(z, z, *params)


def _rwkv_scan_body(r_ref, lw_ref, k_ref, v_ref, kk_ref, b_ref, g_ref, bv_ref, lnw_ref, lnb_ref, y_ref,
                    s_scr, *, tt):
    C = CHUNK

    @pl.when(pl.program_id(1) == 0)
    def _():
        s_scr[...] = jnp.zeros(s_scr.shape, f32)

    ri = lax.broadcasted_iota(jnp.int32, (C, C), 0)
    ci = lax.broadcasted_iota(jnp.int32, (C, C), 1)
    tri_incl = ri >= ci
    tri_strict = ri > ci
    cum_mat = jnp.where(tri_incl, 1.0, 0.0)
    eye = jnp.where(ri == ci, 1.0, 0.0)
    lane = lax.broadcasted_iota(jnp.int32, (1, PAIR), 1)
    pr = lax.broadcasted_iota(jnp.int32, (PAIR, PAIR), 0) // HEAD_DIM
    pc = lax.broadcasted_iota(jnp.int32, (PAIR, PAIR), 1) // HEAD_DIM
    blockdiag = pr == pc
    head_mean = jnp.where(blockdiag, 1.0, 0.0).astype(bf16)
    first = lane < HEAD_DIM

    def chunk(c, carry):
        rows = pl.ds(pl.multiple_of(c * C, C), C)
        for p in range(RWKV_WIDTH // PAIR):
            cols = slice(p * PAIR, (p + 1) * PAIR)
            r_ = r_ref[0, rows, cols]
            lw_ = lw_ref[0, rows, cols]
            k_ = k_ref[0, rows, cols]
            v_ = v_ref[0, rows, cols]
            kk_ = kk_ref[0, rows, cols]
            b_ = b_ref[0, rows, cols]
            cum = jnp.dot(cum_mat, lw_, preferred_element_type=f32, precision=lax.Precision.HIGHEST)
            cum_end = cum[C - 1:C, :]
            e_neg = jnp.exp(-cum)
            e_end = jnp.exp(cum_end - cum)
            a_t = -kk_ * jnp.exp(cum - lw_)
            r_t = r_ * jnp.exp(cum)
            b_t = b_ * e_neg
            k_t = k_ * e_neg
            b_h = b_ * e_end
            k_h = k_ * e_end
            w_end = jnp.exp(cum_end)
            lhs = jnp.concatenate([a_t, r_t], axis=0)
            ua, w2, rq, yin = [], [], [], []
            for h in range(2):
                mine = first if h == 0 else jnp.logical_not(first)
                lhs_h = jnp.where(mine, lhs, 0.0)
                ab = _dot_nt(lhs_h, b_t)
                ak = _dot_nt(lhs_h, k_t)
                a_ab = jnp.where(tri_strict, ab[:C], 0.0)
                a_ak = jnp.where(tri_strict, ak[:C], 0.0)
                a_rb = jnp.where(tri_incl, ab[C:], 0.0)
                a_rk = jnp.where(tri_incl, ak[C:], 0.0)
                tinv = eye + a_ab
                pw = a_ab
                for _ in range(5):
                    pw = _dot(pw, pw)
                    tinv = tinv + _dot(tinv, pw)
                x1 = _dot(jnp.concatenate([a_ak, a_rk], axis=0), v_)
                tx = _dot(tinv, jnp.concatenate([a_t, x1[:C]], axis=1))
                ex = _dot(a_rb, tx)
                ua.append(tx[:, :PAIR])
                w2.append(tx[:, PAIR:])
                rq.append(r_t + ex[:, :PAIR])
                yin.append(x1[C:] + ex[:, PAIR:])
            ua = jnp.where(first, ua[0], ua[1])
            w2 = jnp.where(first, w2[0], w2[1])
            rq = jnp.where(first, rq[0], rq[1])
            yin = jnp.where(first, yin[0], yin[1])
            s0 = s_scr[p]
            y = _dot_nt(rq, s0) + yin
            gq = _dot_tn(jnp.concatenate([ua, w2], axis=1), b_h)
            gmat = jnp.where(blockdiag, gq[:PAIR], 0.0)
            qmat = jnp.where(blockdiag, gq[PAIR:] + _dot_tn(v_, k_h), 0.0)
            s_scr[p] = s0 * w_end + _dot(s0, gmat) + qmat
            mu = _split_dot(y, head_mean) * (1.0 / HEAD_DIM)
            d = y - mu
            var = _split_dot(d * d, head_mean) * (1.0 / HEAD_DIM)
            yn = d * lax.rsqrt(var + GN_EPS) * lnw_ref[:, cols] + lnb_ref[:, cols]
            y_ref[0, rows, cols] = (yn + bv_ref[0, rows, cols]) * g_ref[0, rows, cols]
        return carry

    lax.fori_loop(0, tt // C, chunk, 0)


def _rwkv_scan(r, lw, k, v, kk, b, g, bv, lnw, lnb, tt):
    B, S, W = r.shape
    tok = pl.BlockSpec((1, tt, W), lambda bb, i: (bb, i, 0))
    const = pl.BlockSpec((1, W), lambda bb, i: (0, 0))
    return pl.pallas_call(
        functools.partial(_rwkv_scan_body, tt=tt),
        grid=(B, S // tt),
        in_specs=[tok] * 8 + [const, const],
        out_specs=tok,
        out_shape=jax.ShapeDtypeStruct((B, S, W), f32),
        scratch_shapes=[pltpu.VMEM((W // PAIR, PAIR, PAIR), f32)],
        compiler_params=pltpu.CompilerParams(
            dimension_semantics=("arbitrary", "arbitrary"), vmem_limit_bytes=VMEM_LIMIT),
        name="rwkv_scan",
    )(r, lw, k, v, kk, b, g, bv, lnw, lnb)


SCAN_GROUP = 2


def _rwkv_body(z_ref, zp_ref, mu_ref, w0_ref, wl_ref, a0_ref, al_ref, gl_ref, kkw_ref, ka_ref, rk_ref, bd_ref,
               lnw_ref, lnb_ref, y_ref, s_scr, r_s, lw_s, k_s, v_s, kk_s, b_s, g_s, bv_s, *, tt):
    C = CHUNK
    n_pairs = RWKV_WIDTH // PAIR
    i = pl.program_id(1)

    @pl.when(i == 0)
    def _():
        s_scr[...] = jnp.zeros(s_scr.shape, f32)

    z = z_ref[0]
    prev_row = jnp.where(i > 0, zp_ref[0, 7:8, :], 0.0)
    row = lax.broadcasted_iota(jnp.int32, (tt, 1), 0)
    z_prev = jnp.where(row == 0, prev_row, pltpu.roll(z, 1, axis=0))
    zs = z + (z_prev - z) * mu_ref[...]
    r = zs[:, 0:512]
    k = zs[:, 512:1024]
    v = zs[:, 1024:1536]
    lora = zs[:, 1536:1664]
    gd = zs[:, 1664:1792]
    wlog = w0_ref[...] + _dot(jnp.tanh(lora), wl_ref[...])
    sp = jnp.maximum(-wlog, 0.0) + jnp.log(1.0 + jnp.exp(-jnp.abs(wlog)))
    a = _sigmoid(a0_ref[...] + _dot(lora, al_ref[...]))
    kk = k * kkw_ref[...]
    kk = kk * lax.rsqrt(jnp.maximum(_split_dot(kk * kk, bd_ref[...]), 1e-24))
    k2 = k * (1.0 + (a - 1.0) * ka_ref[...])
    r_s[...] = r
    lw_s[...] = -jnp.exp(-sp - 0.5)
    k_s[...] = k2
    v_s[...] = v
    kk_s[...] = kk
    b_s[...] = kk * a
    g_s[...] = _dot(_sigmoid(gd), gl_ref[...])
    bv_s[...] = _split_dot(r * k2 * rk_ref[...], bd_ref[...]) * v

    ri = lax.broadcasted_iota(jnp.int32, (C, C), 0)
    ci = lax.broadcasted_iota(jnp.int32, (C, C), 1)
    tri_incl = ri >= ci
    tri_strict = ri > ci
    cum_mat = jnp.where(tri_incl, 1.0, 0.0)
    lane = lax.broadcasted_iota(jnp.int32, (1, PAIR), 1)
    first = lane < HEAD_DIM
    mine = (first, jnp.logical_not(first))
    blockdiag = (lax.broadcasted_iota(jnp.int32, (PAIR, PAIR), 0) // HEAD_DIM
                 == lax.broadcasted_iota(jnp.int32, (PAIR, PAIR), 1) // HEAD_DIM)
    head_mean = jnp.where(blockdiag, 1.0, 0.0).astype(bf16)

    def swap_heads(x):
        return pltpu.roll(x, HEAD_DIM, axis=1)

    def group(gi, carry):
        units = []
        for cc in range(SCAN_GROUP):
            rows = pl.ds(pl.multiple_of((gi * SCAN_GROUP + cc) * C, C), C)
            r_, lw_, k_, v_, kk_, b_ = (s[rows, :] for s in (r_s, lw_s, k_s, v_s, kk_s, b_s))
            cum = jnp.dot(cum_mat, lw_, preferred_element_type=f32, precision=lax.Precision.HIGHEST)
            cum_end = cum[C - 1:C, :]
            e_neg = jnp.exp(-cum)
            e_end = jnp.exp(cum_end - cum)
            a_t = -kk_ * jnp.exp(cum - lw_)
            r_t = r_ * jnp.exp(cum)
            b_t = b_ * e_neg
            k_t = k_ * e_neg
            b_h = b_ * e_end
            k_h = k_ * e_end
            w_end = jnp.exp(cum_end)
            for p in range(n_pairs):
                pc = slice(p * PAIR, (p + 1) * PAIR)
                units.append(dict(cc=cc, p=p, rows=rows, pc=pc, a_t=a_t[:, pc], r_t=r_t[:, pc], b_t=b_t[:, pc],
                                  k_t=k_t[:, pc], b_h=b_h[:, pc], k_h=k_h[:, pc], v=v_[:, pc], w_end=w_end[:, pc]))
        heads = [(u, h) for u in units for h in range(2)]

        for u in units:
            lhs = jnp.concatenate([u["a_t"], u["r_t"]], axis=0)
            u["ab"], u["ak"] = [], []
            for h in range(2):
                lhs_h = jnp.where(mine[h], lhs, 0.0)
                u["ab"].append(_dot_nt(lhs_h, u["b_t"]))
                u["ak"].append(_dot_nt(lhs_h, u["k_t"]))
            u["kv"] = _dot_tn(u["v"], u["k_h"])
            u["v_sw"] = swap_heads(u["v"])
        st = {}
        for u, h in heads:
            key = (u["cc"], u["p"], h)
            ab, ak = u["ab"][h], u["ak"][h]
            st[key] = dict(a=jnp.where(tri_strict, ab[:C], 0.0), a_rb=jnp.where(tri_incl, ab[C:], 0.0),
                           akrk=jnp.concatenate([jnp.where(tri_strict, ak[:C], 0.0),
                                                 jnp.where(tri_incl, ak[C:], 0.0)], axis=0))
        for u, h in heads:
            s = st[(u["cc"], u["p"], h)]
            x1 = _dot(s["akrk"], u["v_sw"])
            s["x"] = jnp.where(mine[h], u["a_t"], x1[:C])
            s["arkv"] = x1[C:]
            s["pows"] = [s["a"], _dot(s["a"], s["a"])]
        for j in range(6):
            for u, h in heads:
                s = st[(u["cc"], u["p"], h)]
                s["x"] = s["x"] + _dot(s["pows"][j], s["x"])
            if j + 2 <= 5:
                for u, h in heads:
                    s = st[(u["cc"], u["p"], h)]
                    s["pows"].append(_dot(s["pows"][j + 1], s["pows"][j + 1]))
        for u, h in heads:
            s = st[(u["cc"], u["p"], h)]
            s["ex"] = _dot(s["a_rb"], s["x"])
        for u in units:
            s0, s1 = st[(u["cc"], u["p"], 0)], st[(u["cc"], u["p"], 1)]
            ua = jnp.where(first, s0["x"], s1["x"])
            w2 = swap_heads(jnp.where(first, s1["x"], s0["x"]))
            u["rq"] = u["r_t"] + jnp.where(first, s0["ex"], s1["ex"])
            u["yin"] = swap_heads(jnp.where(first, s1["arkv"] + s1["ex"], s0["arkv"] + s0["ex"]))
            gq = _dot_tn(jnp.concatenate([ua, w2], axis=1), u["b_h"])
            u["gmat"] = jnp.where(blockdiag, gq[:PAIR], 0.0)
            u["qmat"] = jnp.where(blockdiag, gq[PAIR:] + u["kv"], 0.0)
        state = [s_scr[p] for p in range(n_pairs)]
        for u in units:
            s0 = state[u["p"]]
            u["y"] = _dot_nt(u["rq"], s0) + u["yin"]
            state[u["p"]] = s0 * u["w_end"] + _dot(s0, u["gmat"]) + u["qmat"]
        for p in range(n_pairs):
            s_scr[p] = state[p]
        for u in units:
            y = u["y"]
            mu = _split_dot(y, head_mean) * (1.0 / HEAD_DIM)
            u["d"] = y - mu
        for u in units:
            d = u["d"]
            var = _split_dot(d * d, head_mean) * (1.0 / HEAD_DIM)
            yn = d * lax.rsqrt(var + GN_EPS) * lnw_ref[:, u["pc"]] + lnb_ref[:, u["pc"]]
            y_ref[0, u["rows"], u["pc"]] = (yn + bv_s[u["rows"], u["pc"]]) * g_s[u["rows"], u["pc"]]
        return carry

    lax.fori_loop(0, tt // (C * SCAN_GROUP), group, 0)


def _rwkv(z, params, tt):
    B, S, W = z.shape
    const = lambda a: pl.BlockSpec(a.shape, lambda b, i: (0,) * a.ndim)
    prev = pl.BlockSpec((1, 8, W), lambda b, i: (b, jnp.maximum(i * (tt // 8) - 1, 0), 0))
    return pl.pallas_call(
        functools.partial(_rwkv_body, tt=tt),
        grid=(B, S // tt),
        in_specs=[pl.BlockSpec((1, tt, W), lambda b, i: (b, i, 0)), prev] + [const(a) for a in params],
        out_specs=pl.BlockSpec((1, tt, RWKV_WIDTH), lambda b, i: (b, i, 0)),
        out_shape=jax.ShapeDtypeStruct((B, S, RWKV_WIDTH), f32),
        scratch_shapes=[pltpu.VMEM((RWKV_WIDTH // PAIR, PAIR, PAIR), f32)]
        + [pltpu.VMEM((tt, RWKV_WIDTH), f32)] * 8,
        compiler_params=pltpu.CompilerParams(
            dimension_semantics=("arbitrary", "arbitrary"), vmem_limit_bytes=VMEM_LIMIT),
        name="rwkv",
    )(z, z, *params)


def _post_body(x_ref, yn_ref, yr_ref, p_ref, wo_ref, gpost_ref, gpre_ref, gmlp_ref, wup_ref, wdn_ref,
               wpg_ref, wple_ref, o_ref):
    y = jnp.concatenate([yn_ref[0], yr_ref[0]], axis=1).astype(bf16)
    mix = jnp.dot(y, wo_ref[...], preferred_element_type=f32)
    x1 = x_ref[0] + _rms(mix, gpost_ref[...])
    h = _rms(x1, gpre_ref[...]).astype(bf16)
    acc = None
    for c in range(D_FF // D_MODEL):
        cs = slice(c * D_MODEL, (c + 1) * D_MODEL)
        u = jnp.dot(h, wup_ref[:, cs], preferred_element_type=f32)
        u = jnp.square(jnp.maximum(u, 0.0)).astype(bf16)
        part = jnp.dot(u, wdn_ref[cs, :], preferred_element_type=f32)
        acc = part if acc is None else acc + part
    x2 = x1 + _rms(acc, gmlp_ref[...])
    gate = _sigmoid(jnp.dot(x2.astype(bf16), wpg_ref[...], preferred_element_type=f32))
    o_ref[0] = x2 + gate * jnp.dot(p_ref[0].astype(bf16), wple_ref[...], preferred_element_type=f32)


def _post(x, yn, yr, p, weights, tm):
    B, S, D = x.shape
    tok = lambda w: pl.BlockSpec((1, tm, w), lambda b, i: (b, i, 0))
    const = lambda a: pl.BlockSpec(a.shape, lambda b, i: (0,) * a.ndim, pipeline_mode=pl.Buffered(1))
    return pl.pallas_call(
        _post_body,
        grid=(B, S // tm),
        in_specs=[tok(D), tok(512), tok(512), tok(PLE_DIM)] + [const(a) for a in weights],
        out_specs=tok(D),
        out_shape=jax.ShapeDtypeStruct((B, S, D), f32),
        compiler_params=pltpu.CompilerParams(
            dimension_semantics=("arbitrary", "arbitrary"), vmem_limit_bytes=VMEM_LIMIT),
        name="post",
    )(x, yn, yr, p, *weights)


def _pack_inproj(w_in, gate_bias):
    wq = w_in[:, 0:512].reshape(D_MODEL, NSA_KV_HEADS, NSA_GROUP, 1, HEAD_DIM)
    onehot = jnp.eye(NSA_KV_HEADS, dtype=w_in.dtype).reshape(1, NSA_KV_HEADS, 1, NSA_KV_HEADS, 1)
    wq = (wq * onehot).reshape(D_MODEL, NSA_HEADS * PAIR)
    wkv = w_in[:, 512:1280]
    wg = jnp.pad(w_in[:, 1280:1304], ((0, 0), (0, 128 - 24)))
    wr = w_in[:, 1304:]
    wcat = jnp.concatenate([wq, wkv, wg, wr], axis=1).astype(bf16)
    return wcat, jnp.pad(gate_bias, (0, 128 - 24)).reshape(1, 128)


def _pack_compress(pe, w1, b1, w2):
    eye2 = jnp.eye(NSA_KV_HEADS, dtype=f32)
    w1r = w1.reshape(CMP_LEN, HEAD_DIM, CMP_HIDDEN)
    halves = []
    for part in (w1r[:CMP_STRIDE], w1r[CMP_STRIDE:]):
        halves.append(jnp.einsum("jdc,gh->jgdhc", part, eye2).reshape(CMP_STRIDE * PAIR, 2 * CMP_HIDDEN))
    w = jnp.stack(halves).astype(bf16)
    per = jnp.broadcast_to(pe.reshape(2, CMP_STRIDE, 1, HEAD_DIM), (2, CMP_STRIDE, NSA_KV_HEADS, HEAD_DIM))
    per = per.reshape(2, CMP_STRIDE * PAIR)
    b1p = jnp.tile(b1, NSA_KV_HEADS).reshape(1, 2 * CMP_HIDDEN)
    w2p = jnp.einsum("cd,gh->gchd", w2, eye2).reshape(2 * CMP_HIDDEN, PAIR).astype(bf16)
    return per, w, b1p, w2p


def _sel_map_t(seq):
    nc, ns = seq // CMP_STRIDE, seq // SEL_BLOCK
    c0 = np.arange(nc) * CMP_STRIDE
    s0 = np.arange(ns) * SEL_BLOCK
    ov = (np.minimum(c0[:, None] + CMP_LEN - 1, s0[None, :] + SEL_BLOCK - 1)
          - np.maximum(c0[:, None], s0[None, :]) + 1)
    m = np.clip(ov, 0, None).astype(np.float32) / CMP_STRIDE
    m[nc - 1] = 0.0
    return jnp.asarray(m.T, dtype=bf16)


def _row(a):
    return a.reshape(1, -1)


def _mixers(x, g_mix_pre, w_in, nsa_gate_bias, cmp_k, cmp_v, shift_mu, w0, w_lora_up, a0, a_lora_up, g_lora_up,
            k_k, k_a, r_k, lnx_w, lnx_b):
    B, S, _ = x.shape
    bd = jnp.asarray(np.kron(np.eye(RWKV_WIDTH // HEAD_DIM), np.ones((HEAD_DIM, HEAD_DIM))), dtype=bf16)
    wcat, gbias = _pack_inproj(w_in, nsa_gate_bias)
    q, kc, vc, ks, vs, kw, vw, gates, z = _inproj(x, _row(g_mix_pre), wcat, gbias, tm=512)
    chunks = (B, S // CMP_STRIDE, CMP_STRIDE * PAIR)
    kcmp, vcmp = _compress(kc.reshape(chunks), vc.reshape(chunks), _pack_compress(*cmp_k), _pack_compress(*cmp_v))
    y_nsa = _nsa4(q, kcmp, vcmp, ks, vs, kw, vw, gates, _sel_map_t(S), tq=256, tk=512)

    wl = jnp.concatenate([w_lora_up, jnp.zeros((ICLR_LORA, RWKV_WIDTH), f32)], axis=0).astype(bf16)
    al = jnp.concatenate([jnp.zeros((DECAY_LORA, RWKV_WIDTH), f32), a_lora_up], axis=0).astype(bf16)
    rwkv_params = (_row(shift_mu), _row(w0), wl, _row(a0), al, g_lora_up.astype(bf16),
                   _row(k_k), _row(k_a), _row(r_k), bd, _row(lnx_w), _row(lnx_b))
    y_rwkv = _rwkv(z, rwkv_params, tt=256)
    return z, y_nsa, y_rwkv


def kernel(x, p, g_mix_pre, g_mix_post, g_mlp_pre, g_mlp_post, w_in, nsa_gate_bias, cmp_pe_k, cmp_k_w1, cmp_k_b1, cmp_k_w2, cmp_pe_v, cmp_v_w1, cmp_v_b1, cmp_v_w2, shift_mu, w0, w_lora_up, a0, a_lora_up, g_lora_up, k_k, k_a, r_k, lnx_w, lnx_b, w_out, w_up, w_down, w_ple, w_ple_gate):
    D = x.shape[-1]
    for i in range(p.shape[0]):
        _, y_nsa, y_rwkv = _mixers(
            x, g_mix_pre[i], w_in[i], nsa_gate_bias[i],
            (cmp_pe_k[i], cmp_k_w1[i], cmp_k_b1[i], cmp_k_w2[i]), (cmp_pe_v[i], cmp_v_w1[i], cmp_v_b1[i], cmp_v_w2[i]),
            shift_mu[i], w0[i], w_lora_up[i], a0[i], a_lora_up[i], g_lora_up[i], k_k[i], k_a[i], r_k[i],
            lnx_w[i], lnx_b[i])
        wo_nsa = w_out[i][:512].reshape(NSA_KV_HEADS, NSA_GROUP, HEAD_DIM, D).transpose(1, 0, 2, 3).reshape(512, D)
        wo = jnp.concatenate([wo_nsa, w_out[i][512:]], axis=0).astype(bf16)
        weights = (wo, _row(g_mix_post[i]), _row(g_mlp_pre[i]), _row(g_mlp_post[i]), w_up[i].astype(bf16),
                   w_down[i].astype(bf16), w_ple_gate[i].astype(bf16), w_ple[i].astype(bf16))
        x = _post(x, y_nsa, y_rwkv, p[i], weights, tm=512)
    return x
```

```python
import functools

import jax
import jax.numpy as jnp
import numpy as np
from jax import lax
from jax.experimental import pallas as pl
from jax.experimental.pallas import tpu as pltpu

f32 = jnp.float32
bf16 = jnp.bfloat16

D_MODEL = 1024
HEAD_DIM = 64
NSA_HEADS = 8
NSA_KV_HEADS = 2
NSA_GROUP = NSA_HEADS // NSA_KV_HEADS
CMP_LEN = 32
CMP_STRIDE = 16
CMP_HIDDEN = 2 * HEAD_DIM
SEL_BLOCK = 64
SEL_TOPK = 16
WINDOW = 512
RWKV_WIDTH = 512
RWKV_COLS = 1792
DECAY_LORA = 64
ICLR_LORA = 64
GATE_LORA = 128
D_FF = 4 * D_MODEL
PLE_DIM = 256
NORM_EPS = 1e-6
GN_EPS = 64e-5
NEG_INF = -1e30
FORCE_SCORE = 1e4

PAIR = 2 * HEAD_DIM
CHUNK = 64
VMEM_LIMIT = 56 * 1024 * 1024

_Q0, _KV0, _GATE0, _RW0, _WCOLS = 0, 1024, 1792, 1920, 3712


def _dot(a, b):
    return jnp.dot(a.astype(bf16), b.astype(bf16), preferred_element_type=f32)


def _dot_nt(a, b):
    return lax.dot_general(a.astype(bf16), b.astype(bf16), (((1,), (1,)), ((), ())), preferred_element_type=f32)


def _dot_tn(a, b):
    return lax.dot_general(a.astype(bf16), b.astype(bf16), (((0,), (0,)), ((), ())), preferred_element_type=f32)


def _split_dot(x, w):
    hi = x.astype(bf16)
    lo = (x - hi.astype(f32)).astype(bf16)
    return jnp.dot(hi, w, preferred_element_type=f32) + jnp.dot(lo, w, preferred_element_type=f32)


def _split3_dot(w, x):
    hi = x.astype(bf16)
    r1 = x - hi.astype(f32)
    mid = r1.astype(bf16)
    lo = (r1 - mid.astype(f32)).astype(bf16)
    return (jnp.dot(w, hi, preferred_element_type=f32) + jnp.dot(w, mid, preferred_element_type=f32)
            + jnp.dot(w, lo, preferred_element_type=f32))


def _rms(x, g):
    ms = jnp.mean(x * x, axis=-1, keepdims=True)
    return x * lax.rsqrt(ms + NORM_EPS) * g


def _sigmoid(x):
    return 1.0 / (1.0 + jnp.exp(-x))


def _inproj_body(x_ref, g_ref, w_ref, gb_ref, q_ref, kc_ref, vc_ref, ks_ref, vs_ref, kw_ref, vw_ref,
                 gate_ref, z_ref):
    h = _rms(x_ref[0], g_ref[...]).astype(bf16)
    q = jnp.dot(h, w_ref[:, _Q0:_KV0], preferred_element_type=f32) * (HEAD_DIM ** -0.5)
    for hd in range(NSA_HEADS):
        q_ref[0, hd] = q[:, hd * PAIR:(hd + 1) * PAIR].astype(bf16)
    kv = jnp.dot(h, w_ref[:, _KV0:_GATE0], preferred_element_type=f32)
    kc_ref[0] = kv[:, 0:128]
    vc_ref[0] = kv[:, 128:256]
    ks_ref[0] = kv[:, 256:384].astype(bf16)
    vs_ref[0] = kv[:, 384:512].astype(bf16)
    kw_ref[0] = kv[:, 512:640].astype(bf16)
    vw_ref[0] = kv[:, 640:768].astype(bf16)
    gl = jnp.dot(h, w_ref[:, _GATE0:_RW0], preferred_element_type=f32)
    gate_ref[0] = _sigmoid(gl + gb_ref[...])
    z_ref[0] = jnp.dot(h, w_ref[:, _RW0:_WCOLS], preferred_element_type=f32)


def _inproj(x, g, wcat, gbias, tm):
    B, S, D = x.shape
    tok = lambda w: pl.BlockSpec((1, tm, w), lambda b, i: (b, i, 0))
    const = lambda shp: pl.BlockSpec(shp, lambda b, i: (0,) * len(shp))
    return pl.pallas_call(
        _inproj_body,
        grid=(B, S // tm),
        in_specs=[tok(D), const((1, D)), const((D, _WCOLS)), const((1, 128))],
        out_specs=[pl.BlockSpec((1, NSA_HEADS, tm, PAIR), lambda b, i: (b, 0, i, 0))]
        + [tok(128)] * 7 + [tok(RWKV_COLS)],
        out_shape=[jax.ShapeDtypeStruct((B, NSA_HEADS, S, PAIR), bf16),
                   jax.ShapeDtypeStruct((B, S, 128), f32), jax.ShapeDtypeStruct((B, S, 128), f32)]
        + [jax.ShapeDtypeStruct((B, S, 128), bf16)] * 4
        + [jax.ShapeDtypeStruct((B, S, 128), f32), jax.ShapeDtypeStruct((B, S, RWKV_COLS), f32)],
        compiler_params=pltpu.CompilerParams(
            dimension_semantics=("arbitrary", "arbitrary"), vmem_limit_bytes=VMEM_LIMIT),
        name="inproj",
    )(x, g, wcat, gbias)


def _gelu_tanh(x):
    return x * (0.5 * (1.0 + jnp.tanh(np.sqrt(2.0 / np.pi) * (x + 0.044715 * (x * x * x)))))


def _compress_one(x, pe_ref, w_ref, b1_ref, w2_ref):
    n = x.shape[0]
    lo = _dot(x + pe_ref[0:1, :], w_ref[0])
    hi = _dot(x + pe_ref[1:2, :], w_ref[1])
    pre = lo + pltpu.roll(hi, n - 1, axis=0) + b1_ref[...]
    return _dot(_gelu_tanh(pre), w2_ref[...])


def _compress_body(xk_ref, xv_ref, pek_ref, wk_ref, bk_ref, w2k_ref, pev_ref, wv_ref, bv_ref, w2v_ref,
                   kc_ref, vc_ref):
    kc_ref[0] = _compress_one(xk_ref[0], pek_ref, wk_ref, bk_ref, w2k_ref).astype(bf16)
    vc_ref[0] = _compress_one(xv_ref[0], pev_ref, wv_ref, bv_ref, w2v_ref).astype(bf16)


def _compress(xk, xv, kparams, vparams):
    B, NC, W = xk.shape
    const = lambda a: pl.BlockSpec(a.shape, lambda b: (0,) * a.ndim)
    seq = pl.BlockSpec((1, NC, W), lambda b: (b, 0, 0))
    out = pl.BlockSpec((1, NC, PAIR), lambda b: (b, 0, 0))
    return pl.pallas_call(
        _compress_body,
        grid=(B,),
        in_specs=[seq, seq] + [const(a) for a in kparams] + [const(a) for a in vparams],
        out_specs=[out, out],
        out_shape=[jax.ShapeDtypeStruct((B, NC, PAIR), bf16)] * 2,
        compiler_params=pltpu.CompilerParams(dimension_semantics=("arbitrary",), vmem_limit_bytes=VMEM_LIMIT),
        name="compress",
    )(xk, xv, *kparams, *vparams)


def _alibi_key_columns(pos, ns, onehot):
    a = np.zeros((pos.shape[0], PAIR), np.float32)
    if onehot:
        a[np.arange(pos.shape[0]), pos // SEL_BLOCK] = 1.0
    a[:, ns] = -1.0
    a[:, ns + 1] = -1.0
    a[:, ns + 2] = pos // SEL_BLOCK
    a[:, ns + 3] = pos % SEL_BLOCK
    return jnp.asarray(a, dtype=bf16)


def _nsa_body(q_ref, kc_ref, vc_ref, ks_ref, vs_ref, kw_ref, vw_ref, gate_ref, selT_ref, auxk_ref, auxc_ref,
              o_ref, m_scr, acc_scr, sa_scr, sb_scr, rank_scr, *, tq, tk, seq):
    R, G, H = NSA_GROUP, NSA_KV_HEADS, NSA_HEADS
    t0 = pl.program_id(1) * tq
    nc = seq // CMP_STRIDE
    ns = seq // SEL_BLOCK
    gm = R * tq
    tok = (t0 + lax.broadcasted_iota(jnp.int32, (tq, 1), 0)).astype(f32)
    lane = lax.broadcasted_iota(jnp.int32, (1, PAIR), 1)
    first = lane < HEAD_DIM

    tl = t0 + lax.broadcasted_iota(jnp.int32, (8, tq), 1)
    rid = lax.broadcasted_iota(jnp.int32, (8, tq), 0)
    alibi_rows = jnp.where(rid == 0, ((tl // SEL_BLOCK) * SEL_BLOCK).astype(f32),
                           jnp.where(rid == 1, (tl % SEL_BLOCK).astype(f32),
                                     jnp.where(rid == 2, float(SEL_BLOCK), jnp.where(rid == 3, 1.0, 0.0))))
    aux0 = jnp.concatenate([jnp.zeros((ns, tq), f32), alibi_rows, jnp.zeros((PAIR - ns - 8, tq), f32)], axis=0).T

    def augment(q8, aux_by_group):
        aux8 = jnp.concatenate([aux_by_group[h // R] * jnp.where(lane < ns, 1.0, 2.0 ** (-(h + 1)))
                                for h in range(H)], axis=0)
        return jnp.concatenate([q8, aux8.astype(bf16)], axis=1)

    def per_head(a):
        return a.reshape(H, tq, a.shape[-1])

    def with_ones(v):
        one = jnp.ones((), v.dtype)
        return jnp.where(first, v, one), jnp.where(first, one, v)

    def weighted_values(p, v):
        v0, v1 = with_ones(v)
        return jnp.concatenate([jnp.dot(p[:gm], v0, preferred_element_type=f32),
                                jnp.dot(p[gm:], v1, preferred_element_type=f32)], axis=0)

    def split_sum(acc):
        top, bot = acc[:gm], acc[gm:]
        return jnp.concatenate([top / top[:, HEAD_DIM:HEAD_DIM + 1], bot / bot[:, 0:1]], axis=0)

    q8 = q_ref[0].reshape(H * tq, PAIR)
    qa = augment(q8, [aux0, aux0])

    kc_aug = jnp.concatenate([kc_ref[0], auxc_ref[...]], axis=1)
    sc = per_head(_dot_nt(qa, kc_aug))
    w0 = pl.multiple_of(jnp.maximum(t0 - WINDOW, 0), tq)
    wrows = pl.ds(w0, WINDOW + tq)
    kw_aug = jnp.concatenate([kw_ref[0, wrows, :], auxk_ref[wrows, :]], axis=1)
    sw = per_head(_dot_nt(qa, kw_aug))

    cend = (lax.broadcasted_iota(jnp.int32, (1, nc), 1) * CMP_STRIDE + (CMP_LEN - 1)).astype(f32)
    vis_c = tok >= cend
    any_c = (tok >= float(CMP_LEN - 1)).astype(f32)
    s = jnp.where(vis_c[None], sc, NEG_INF)
    e = jnp.exp(s - jnp.max(s, axis=-1, keepdims=True))
    p_cmp = e * (any_c[None] / jnp.sum(e, axis=-1, keepdims=True))
    o_cmp = _dot(p_cmp.reshape(H * tq, nc), vc_ref[0])

    jrow = lax.broadcasted_iota(jnp.int32, (ns, tq), 0)
    cur = (t0 + lax.broadcasted_iota(jnp.int32, (ns, tq), 1)) // SEL_BLOCK
    forced = (jrow == 0) | (jrow == cur) | (jrow == cur - 1)
    scores_t = []
    for g in range(G):
        psum = p_cmp[R * g]
        for r in range(1, R):
            psum = psum + p_cmp[R * g + r]
        imp_t = lax.dot_general(selT_ref[...], psum.astype(bf16), (((1,), (1,)), ((), ())),
                                preferred_element_type=f32)
        p_lo = (psum - psum.astype(bf16).astype(f32)).astype(bf16)
        imp_t = imp_t + lax.dot_general(selT_ref[...], p_lo, (((1,), (1,)), ((), ())),
                                        preferred_element_type=f32)
        scores_t.append(jnp.where(forced, FORCE_SCORE, jnp.where(jrow <= cur, imp_t, -1.0)))
        rank_scr[g] = scores_t[g]

    def rank_rows(i8, cnts):
        base = pl.multiple_of(i8 * 8, 8)
        out = []
        for g in range(G):
            rows8 = rank_scr[g, pl.ds(base, 8), :]
            cnt = cnts[g]
            for k in range(8):
                si = rows8[k:k + 1, :]
                cnt = cnt + jnp.where(jrow > base + k, jnp.where(si >= scores_t[g], 1.0, 0.0),
                                      jnp.where(si > scores_t[g], 1.0, 0.0))
            out.append(cnt)
        return tuple(out)

    n_rows8 = jnp.minimum((t0 + tq - 1) // SEL_BLOCK // 8 + 1, ns // 8)
    cnts = lax.fori_loop(0, n_rows8, rank_rows, tuple(jnp.zeros((ns, tq), f32) for _ in range(G)))
    aux_sel = []
    for g in range(G):
        bias_t = jnp.where(cnts[g] < float(SEL_TOPK), 0.0, NEG_INF)
        aux_sel.append(aux0 + jnp.concatenate([bias_t, jnp.zeros((PAIR - ns, tq), f32)], axis=0).T)
    qs = augment(q8, aux_sel)

    m_scr[...] = jnp.full(m_scr.shape, NEG_INF, f32)
    acc_scr[...] = jnp.zeros(acc_scr.shape, f32)

    def key_rows(kt):
        return pl.ds(pl.multiple_of(kt * tk, tk), tk)

    def scores(kt, dst):
        rows_ = key_rows(kt)
        dst[...] = _dot_nt(qs, jnp.concatenate([ks_ref[0, rows_, :], auxk_ref[rows_, :]], axis=1))

    def attend(src, kt, diag):
        s8 = src[...]
        if diag:
            kpos = (kt * tk + lax.broadcasted_iota(jnp.int32, (1, tk), 1)).astype(f32)
            s8 = jnp.where((tok >= kpos)[None], per_head(s8), NEG_INF).reshape(H * tq, tk)
        m_prev = m_scr[...][:, 0:1]
        m_new = jnp.maximum(m_prev, jnp.max(s8, axis=-1, keepdims=True))
        p = jnp.exp(s8 - m_new).astype(bf16)
        acc_scr[...] = jnp.exp(m_prev - m_new) * acc_scr[...] + weighted_values(p, vs_ref[0, key_rows(kt), :])
        m_scr[...] = jnp.broadcast_to(m_new, m_scr.shape)

    n_full = t0 // tk
    scores(n_full, sa_scr)

    kpos = w0 + lax.broadcasted_iota(jnp.int32, (1, WINDOW + tq), 1)
    dist_w = tok - kpos.astype(f32)
    vis_w = jnp.abs(dist_w - (WINDOW - 1) / 2.0) < WINDOW / 2.0
    s = jnp.where(vis_w[None], sw, NEG_INF)
    e = jnp.exp(s - jnp.max(s, axis=-1, keepdims=True)).astype(bf16).reshape(H * tq, WINDOW + tq)
    acc_win = weighted_values(e, vw_ref[0, wrows, :])

    scores(0, sb_scr)
    attend(sa_scr, n_full, True)

    def two_tiles(j, carry):
        scores(2 * j + 1, sa_scr)
        attend(sb_scr, 2 * j, False)
        scores(2 * j + 2, sb_scr)
        attend(sa_scr, 2 * j + 1, False)
        return carry

    lax.fori_loop(0, n_full // 2, two_tiles, 0)

    @pl.when(n_full % 2 == 1)
    def _():
        attend(sb_scr, n_full - 1, False)

    o_sel = split_sum(acc_scr[...])
    o_win = split_sum(acc_win)

    gate = gate_ref[0]
    o_cmp, o_sel, o_win = per_head(o_cmp), per_head(o_sel), per_head(o_win)
    for r in range(R):
        pair = []
        for g in range(G):
            h = R * g + r
            pair.append(gate[:, 3 * h:3 * h + 1] * o_cmp[h] + gate[:, 3 * h + 1:3 * h + 2] * o_sel[h]
                        + gate[:, 3 * h + 2:3 * h + 3] * o_win[h])
        o_ref[0, :, r * PAIR:(r + 1) * PAIR] = jnp.where(first, pair[0], pair[1])


def _nsa(q, kc, vc, ks, vs, kw, vw, gates, sel_t, tq, tk):
    B, H, S, _ = q.shape
    nc, ns = S // CMP_STRIDE, S // SEL_BLOCK
    aux_k = _alibi_key_columns(np.arange(S), ns, onehot=True)
    aux_c = _alibi_key_columns(np.arange(nc) * CMP_STRIDE + (CMP_LEN - 1), ns, onehot=False)
    full = lambda n: pl.BlockSpec((1, n, PAIR), lambda b, i: (b, 0, 0))
    const = lambda a: pl.BlockSpec(a.shape, lambda b, i: (0, 0))
    body = functools.partial(_nsa_body, tq=tq, tk=tk, seq=S)
    return pl.pallas_call(
        body,
        grid=(B, S // tq),
        in_specs=[pl.BlockSpec((1, H, tq, PAIR), lambda b, i: (b, 0, i, 0)),
                  full(nc), full(nc), full(S), full(S), full(S), full(S),
                  pl.BlockSpec((1, tq, 128), lambda b, i: (b, i, 0)),
                  const(sel_t), const(aux_k), const(aux_c)],
        out_specs=pl.BlockSpec((1, tq, NSA_GROUP * PAIR), lambda b, i: (b, i, 0)),
        out_shape=jax.ShapeDtypeStruct((B, S, NSA_GROUP * PAIR), f32),
        scratch_shapes=[pltpu.VMEM((H * tq, PAIR), f32)] * 2 + [pltpu.VMEM((H * tq, tk), f32)] * 2
        + [pltpu.VMEM((NSA_KV_HEADS, ns, tq), f32)],
        compiler_params=pltpu.CompilerParams(
            dimension_semantics=("arbitrary", "arbitrary"), vmem_limit_bytes=VMEM_LIMIT),
        name="nsa",
    )(q, kc, vc, ks, vs, kw, vw, gates, sel_t, aux_k, aux_c)


SCAN_GROUP = 4


def _rwkv_body(z_ref, zp_ref, mu_ref, w0_ref, wl_ref, a0_ref, al_ref, gl_ref, kkw_ref, ka_ref, rk_ref, bd_ref,
               lnw_ref, lnb_ref, y_ref, s_scr, r_s, lw_s, k_s, v_s, kk_s, b_s, g_s, bv_s, *, tt, nb):
    C = CHUNK
    n_pairs = RWKV_WIDTH // PAIR
    i = pl.program_id(1)

    @pl.when(i == 0)
    def _():
        s_scr[...] = jnp.zeros(s_scr.shape, f32)

    z = z_ref[...].reshape(nb * tt, RWKV_COLS)
    row = lax.broadcasted_iota(jnp.int32, (nb * tt, 1), 0)
    z_prev = pltpu.roll(z, 1, axis=0)
    for bb in range(nb):
        z_prev = jnp.where(row == bb * tt, jnp.where(i > 0, zp_ref[bb, 7:8, :], 0.0), z_prev)
    zs = z + (z_prev - z) * mu_ref[...]
    r = zs[:, 0:512]
    k = zs[:, 512:1024]
    v = zs[:, 1024:1536]
    lora = zs[:, 1536:1664]
    gd = zs[:, 1664:1792]
    wlog = w0_ref[...] + _dot(jnp.tanh(lora), wl_ref[...])
    sp = jnp.maximum(-wlog, 0.0) + jnp.log(1.0 + jnp.exp(-jnp.abs(wlog)))
    a = _sigmoid(a0_ref[...] + _dot(lora, al_ref[...]))
    kk = k * kkw_ref[...]
    kk = kk * lax.rsqrt(jnp.maximum(_split_dot(kk * kk, bd_ref[...]), 1e-24))
    k2 = k * (1.0 + (a - 1.0) * ka_ref[...])
    r_s[...] = r
    lw_s[...] = -jnp.exp(-sp - 0.5)
    k_s[...] = k2
    v_s[...] = v
    kk_s[...] = kk
    b_s[...] = kk * a
    g_s[...] = _dot(_sigmoid(gd), gl_ref[...])
    bv_s[...] = _split_dot(r * k2 * rk_ref[...], bd_ref[...]) * v

    ri = lax.broadcasted_iota(jnp.int32, (C, C), 0)
    ci = lax.broadcasted_iota(jnp.int32, (C, C), 1)
    cum_mat = jnp.where(ri >= ci, 1.0, 0.0).astype(bf16)
    ri2 = lax.broadcasted_iota(jnp.int32, (C, 2 * C), 0)
    ci2 = lax.broadcasted_iota(jnp.int32, (C, 2 * C), 1) % C
    tri2_incl = ri2 >= ci2
    tri2_strict = ri2 > ci2
    lane = lax.broadcasted_iota(jnp.int32, (1, PAIR), 1)
    first = lane < HEAD_DIM
    blockdiag = (lax.broadcasted_iota(jnp.int32, (PAIR, PAIR), 0) // HEAD_DIM
                 == lax.broadcasted_iota(jnp.int32, (PAIR, PAIR), 1) // HEAD_DIM)
    head_mean = jnp.where(blockdiag, 1.0, 0.0).astype(bf16)

    def swap_heads(x):
        return pltpu.roll(x, HEAD_DIM, axis=1)

    def by_head_rows(x):
        return jnp.concatenate([jnp.where(first, x, 0.0), jnp.where(first, 0.0, x)], axis=0)

    def block_diag(x2):
        zero = jnp.zeros((x2.shape[0], PAIR), x2.dtype)
        return jnp.concatenate([jnp.concatenate([x2[:, :PAIR], zero], axis=1),
                                jnp.concatenate([zero, x2[:, PAIR:]], axis=1)], axis=0)

    def group(gi):
        units = []
        for cc, bb in [(cc, bb) for cc in range(SCAN_GROUP) for bb in range(nb)]:
            local = pl.ds((gi * SCAN_GROUP + cc) * C, C)
            rows = pl.ds(bb * tt + (gi * SCAN_GROUP + cc) * C, C)
            r_, lw_, k_, v_, kk_, b_ = (s[rows, :] for s in (r_s, lw_s, k_s, v_s, kk_s, b_s))
            cum = _split3_dot(cum_mat, lw_)
            cum_end = cum[C - 1:C, :]
            e_neg = jnp.exp(-cum)
            e_end = jnp.exp(cum_end - cum)
            a_t = -kk_ * jnp.exp(cum - lw_)
            r_t = r_ * jnp.exp(cum)
            b_t = b_ * e_neg
            k_t = k_ * e_neg
            b_h = b_ * e_end
            k_h = k_ * e_end
            w_end = jnp.exp(cum_end)
            for p in range(n_pairs):
                pc = slice(p * PAIR, (p + 1) * PAIR)
                units.append(dict(p=bb * n_pairs + p, bb=bb, local=local, rows=rows, pc=pc, a_t=a_t[:, pc],
                                  r_t=r_t[:, pc], b_t=b_t[:, pc], k_t=k_t[:, pc], b_h=b_h[:, pc], k_h=k_h[:, pc],
                                  v=v_[:, pc], w_end=w_end[:, pc]))

        for u in units:
            lhs = jnp.concatenate([u["a_t"], u["r_t"]], axis=0)
            u["ab"] = _dot_nt(lhs, by_head_rows(u["b_t"]))
            u["ak"] = _dot_nt(lhs, by_head_rows(u["k_t"]))
        for u in units:
            ab, ak = u["ab"], u["ak"]
            u["a"] = jnp.where(tri2_strict, ab[:C], 0.0)
            u["a_rb"] = jnp.where(tri2_incl, ab[C:], 0.0)
            u["akrk"] = jnp.concatenate([jnp.where(tri2_strict, ak[:C], 0.0), jnp.where(tri2_incl, ak[C:], 0.0)],
                                        axis=0)
        for u in units:
            v_sw = swap_heads(u["v"]).astype(bf16)
            x1 = _dot(u["akrk"], block_diag(jnp.concatenate([v_sw, v_sw], axis=1)))
            u["x"] = jnp.concatenate([jnp.where(first, u["a_t"], x1[:C, :PAIR]),
                                      jnp.where(first, x1[:C, PAIR:], u["a_t"])], axis=1)
            u["arkv"] = x1[C:]
            u["pows"] = [u["a"], _dot(u["a"], by_head_rows(u["a"]))]
        for j in range(6):
            for u in units:
                u["x"] = u["x"] + _dot(u["pows"][j], block_diag(u["x"].astype(bf16)))
            if j + 2 <= 5:
                for u in units:
                    u["pows"].append(_dot(u["pows"][j + 1], by_head_rows(u["pows"][j + 1])))
        for u in units:
            u["ex"] = _dot(u["a_rb"], block_diag(u["x"].astype(bf16)))
        for u in units:
            x0, x1 = u["x"][:, :PAIR], u["x"][:, PAIR:]
            ex0, ex1 = u["ex"][:, :PAIR], u["ex"][:, PAIR:]
            ua = jnp.where(first, x0, x1)
            w2 = swap_heads(jnp.where(first, x1, x0))
            u["rq"] = u["r_t"] + jnp.where(first, ex0, ex1)
            u["yin"] = swap_heads(jnp.where(first, u["arkv"][:, PAIR:] + ex1, u["arkv"][:, :PAIR] + ex0))
            u["gmat"] = jnp.where(blockdiag, _dot_tn(ua, u["b_h"]), 0.0)
            u["qmat"] = jnp.where(blockdiag, _dot_tn(jnp.concatenate([w2, u["v"]], axis=0),
                                                     jnp.concatenate([u["b_h"], u["k_h"]], axis=0)), 0.0)
        state = [s_scr[p] for p in range(nb * n_pairs)]
        for u in units:
            s0 = state[u["p"]]
            u["y"] = _dot_nt(u["rq"], s0) + u["yin"]
            state[u["p"]] = s0 * u["w_end"] + _dot(s0, u["gmat"]) + u["qmat"]
        for p in range(nb * n_pairs):
            s_scr[p] = state[p]
        for u in units:
            y = u["y"]
            mu = jnp.dot(y.astype(bf16), head_mean, preferred_element_type=f32) * (1.0 / HEAD_DIM)
            u["d"] = y - mu
        for u in units:
            d = u["d"]
            var = jnp.dot((d * d).astype(bf16), head_mean, preferred_element_type=f32) * (1.0 / HEAD_DIM)
            yn = d * lax.rsqrt(var + GN_EPS) * lnw_ref[:, u["pc"]] + lnb_ref[:, u["pc"]]
            y_ref[u["bb"], u["local"], u["pc"]] = (yn + bv_s[u["rows"], u["pc"]]) * g_s[u["rows"], u["pc"]]

    for gi in range(tt // (C * SCAN_GROUP)):
        group(gi)


def _rwkv(z, params, tt, nb):
    B, S, W = z.shape
    const = lambda a: pl.BlockSpec(a.shape, lambda b, i: (0,) * a.ndim)
    prev = pl.BlockSpec((nb, 8, W), lambda b, i: (b, jnp.maximum(i * (tt // 8) - 1, 0), 0))
    return pl.pallas_call(
        functools.partial(_rwkv_body, tt=tt, nb=nb),
        grid=(B // nb, S // tt),
        in_specs=[pl.BlockSpec((nb, tt, W), lambda b, i: (b, i, 0)), prev] + [const(a) for a in params],
        out_specs=pl.BlockSpec((nb, tt, RWKV_WIDTH), lambda b, i: (b, i, 0)),
        out_shape=jax.ShapeDtypeStruct((B, S, RWKV_WIDTH), f32),
        scratch_shapes=[pltpu.VMEM((nb * RWKV_WIDTH // PAIR, PAIR, PAIR), f32)]
        + [pltpu.VMEM((nb * tt, RWKV_WIDTH), f32)] * 8,
        compiler_params=pltpu.CompilerParams(
            dimension_semantics=("arbitrary", "arbitrary"), vmem_limit_bytes=VMEM_LIMIT),
        name="rwkv",
    )(z, z, *params)


def _post_body(x_ref, yn_ref, yr_ref, p_ref, wo_ref, gpost_ref, gpre_ref, gmlp_ref, wup_ref, wdn_ref,
               wpg_ref, wple_ref, o_ref):
    y = jnp.concatenate([yn_ref[0], yr_ref[0]], axis=1).astype(bf16)
    mix = jnp.dot(y, wo_ref[...], preferred_element_type=f32)
    x1 = x_ref[0] + _rms(mix, gpost_ref[...])
    h = _rms(x1, gpre_ref[...]).astype(bf16)
    acc = None
    for c in range(D_FF // D_MODEL):
        cs = slice(c * D_MODEL, (c + 1) * D_MODEL)
        u = jnp.dot(h, wup_ref[:, cs], preferred_element_type=f32)
        u = jnp.square(jnp.maximum(u, 0.0)).astype(bf16)
        part = jnp.dot(u, wdn_ref[cs, :], preferred_element_type=f32)
        acc = part if acc is None else acc + part
    x2 = x1 + _rms(acc, gmlp_ref[...])
    gate = _sigmoid(jnp.dot(x2.astype(bf16), wpg_ref[...], preferred_element_type=f32))
    o_ref[0] = x2 + gate * jnp.dot(p_ref[0].astype(bf16), wple_ref[...], preferred_element_type=f32)


def _post(x, yn, yr, p, weights, tm):
    B, S, D = x.shape
    tok = lambda w: pl.BlockSpec((1, tm, w), lambda b, i: (b, i, 0))
    const = lambda a: pl.BlockSpec(a.shape, lambda b, i: (0,) * a.ndim, pipeline_mode=pl.Buffered(1))
    return pl.pallas_call(
        _post_body,
        grid=(B, S // tm),
        in_specs=[tok(D), tok(512), tok(512), tok(PLE_DIM)] + [const(a) for a in weights],
        out_specs=tok(D),
        out_shape=jax.ShapeDtypeStruct((B, S, D), f32),
        compiler_params=pltpu.CompilerParams(
            dimension_semantics=("arbitrary", "arbitrary"), vmem_limit_bytes=VMEM_LIMIT),
        name="post",
    )(x, yn, yr, p, *weights)


def _pack_inproj(w_in, gate_bias):
    wq = w_in[:, 0:512].reshape(D_MODEL, NSA_KV_HEADS, NSA_GROUP, 1, HEAD_DIM)
    onehot = jnp.eye(NSA_KV_HEADS, dtype=w_in.dtype).reshape(1, NSA_KV_HEADS, 1, NSA_KV_HEADS, 1)
    wq = (wq * onehot).reshape(D_MODEL, NSA_HEADS * PAIR)
    wkv = w_in[:, 512:1280]
    wg = jnp.pad(w_in[:, 1280:1304], ((0, 0), (0, 128 - 24)))
    wr = w_in[:, 1304:]
    wcat = jnp.concatenate([wq, wkv, wg, wr], axis=1).astype(bf16)
    return wcat, jnp.pad(gate_bias, (0, 128 - 24)).reshape(1, 128)


def _pack_compress(pe, w1, b1, w2):
    eye2 = jnp.eye(NSA_KV_HEADS, dtype=f32)
    w1r = w1.reshape(CMP_LEN, HEAD_DIM, CMP_HIDDEN)
    halves = []
    for part in (w1r[:CMP_STRIDE], w1r[CMP_STRIDE:]):
        halves.append(jnp.einsum("jdc,gh->jgdhc", part, eye2).reshape(CMP_STRIDE * PAIR, 2 * CMP_HIDDEN))
    w = jnp.stack(halves).astype(bf16)
    per = jnp.broadcast_to(pe.reshape(2, CMP_STRIDE, 1, HEAD_DIM), (2, CMP_STRIDE, NSA_KV_HEADS, HEAD_DIM))
    per = per.reshape(2, CMP_STRIDE * PAIR)
    b1p = jnp.tile(b1, NSA_KV_HEADS).reshape(1, 2 * CMP_HIDDEN)
    w2p = jnp.einsum("cd,gh->gchd", w2, eye2).reshape(2 * CMP_HIDDEN, PAIR).astype(bf16)
    return per, w, b1p, w2p


def _sel_map_t(seq):
    nc, ns = seq // CMP_STRIDE, seq // SEL_BLOCK
    c0 = np.arange(nc) * CMP_STRIDE
    s0 = np.arange(ns) * SEL_BLOCK
    ov = (np.minimum(c0[:, None] + CMP_LEN - 1, s0[None, :] + SEL_BLOCK - 1)
          - np.maximum(c0[:, None], s0[None, :]) + 1)
    m = np.clip(ov, 0, None).astype(np.float32) / CMP_STRIDE
    m[nc - 1] = 0.0
    return jnp.asarray(m.T, dtype=bf16)


def _row(a):
    return a.reshape(1, -1)


def _mixers(x, g_mix_pre, w_in, nsa_gate_bias, cmp_k, cmp_v, shift_mu, w0, w_lora_up, a0, a_lora_up, g_lora_up,
            k_k, k_a, r_k, lnx_w, lnx_b):
    B, S, _ = x.shape
    bd = jnp.asarray(np.kron(np.eye(RWKV_WIDTH // HEAD_DIM), np.ones((HEAD_DIM, HEAD_DIM))), dtype=bf16)
    wcat, gbias = _pack_inproj(w_in, nsa_gate_bias)
    q, kc, vc, ks, vs, kw, vw, gates, z = _inproj(x, _row(g_mix_pre), wcat, gbias, tm=512)
    chunks = (B, S // CMP_STRIDE, CMP_STRIDE * PAIR)
    kcmp, vcmp = _compress(kc.reshape(chunks), vc.reshape(chunks), _pack_compress(*cmp_k), _pack_compress(*cmp_v))
    y_nsa = _nsa(q, kcmp, vcmp, ks, vs, kw, vw, gates, _sel_map_t(S), tq=256, tk=512)

    wl = jnp.concatenate([w_lora_up, jnp.zeros((ICLR_LORA, RWKV_WIDTH), f32)], axis=0).astype(bf16)
    al = jnp.concatenate([jnp.zeros((DECAY_LORA, RWKV_WIDTH), f32), a_lora_up], axis=0).astype(bf16)
    rwkv_params = (_row(shift_mu), _row(w0), wl, _row(a0), al, g_lora_up.astype(bf16),
                   _row(k_k), _row(k_a), _row(r_k), bd, _row(lnx_w), _row(lnx_b))
    y_rwkv = _rwkv(z, rwkv_params, tt=256, nb=1)
    return z, y_nsa, y_rwkv


def kernel(x, p, g_mix_pre, g_mix_post, g_mlp_pre, g_mlp_post, w_in, nsa_gate_bias, cmp_pe_k, cmp_k_w1, cmp_k_b1, cmp_k_w2, cmp_pe_v, cmp_v_w1, cmp_v_b1, cmp_v_w2, shift_mu, w0, w_lora_up, a0, a_lora_up, g_lora_up, k_k, k_a, r_k, lnx_w, lnx_b, w_out, w_up, w_down, w_ple, w_ple_gate):
    D = x.shape[-1]
    for i in range(p.shape[0]):
        _, y_nsa, y_rwkv = _mixers(
            x, g_mix_pre[i], w_in[i], nsa_gate_bias[i],
            (cmp_pe_k[i], cmp_k_w1[i], cmp_k_b1[i], cmp_k_w2[i]), (cmp_pe_v[i], cmp_v_w1[i], cmp_v_b1[i], cmp_v_w2[i]),
            shift_mu[i], w0[i], w_lora_up[i], a0[i], a_lora_up[i], g_lora_up[i], k_k[i], k_a[i], r_k[i],
            lnx_w[i], lnx_b[i])
        wo_nsa = w_out[i][:512].reshape(NSA_KV_HEADS, NSA_GROUP, HEAD_DIM, D).transpose(1, 0, 2, 3).reshape(512, D)
        wo = jnp.concatenate([wo_nsa, w_out[i][512:]], axis=0).astype(bf16)
        weights = (wo, _row(g_mix_post[i]), _row(g_mlp_pre[i]), _row(g_mlp_post[i]), w_up[i].astype(bf16),
                   w_down[i].astype(bf16), w_ple_gate[i].astype(bf16), w_ple[i].astype(bf16))
        x = _post(x, y_nsa, y_rwkv, p[i], weights, tm=512)
    return x
```

```python
import functools

import jax
import jax.numpy as jnp
import numpy as np
from jax import lax
from jax.experimental import pallas as pl
from jax.experimental.pallas import tpu as pltpu

f32 = jnp.float32
bf16 = jnp.bfloat16

D_MODEL = 1024
HEAD_DIM = 64
NSA_HEADS = 8
NSA_KV_HEADS = 2
NSA_GROUP = NSA_HEADS // NSA_KV_HEADS
CMP_LEN = 32
CMP_STRIDE = 16
CMP_HIDDEN = 2 * HEAD_DIM
SEL_BLOCK = 64
SEL_TOPK = 16
WINDOW = 512
RWKV_WIDTH = 512
RWKV_COLS = 1792
DECAY_LORA = 64
ICLR_LORA = 64
GATE_LORA = 128
D_FF = 4 * D_MODEL
PLE_DIM = 256
NORM_EPS = 1e-6
GN_EPS = 64e-5
NEG_INF = -1e30
FORCE_SCORE = 1e4

PAIR = 2 * HEAD_DIM
CHUNK = 64
VMEM_LIMIT = 56 * 1024 * 1024

_Q0, _KV0, _GATE0, _RW0, _WCOLS = 0, 512, 1280, 1408, 3200


def _dot(a, b):
    return jnp.dot(a.astype(bf16), b.astype(bf16), preferred_element_type=f32)


def _dot_nt(a, b):
    return lax.dot_general(a.astype(bf16), b.astype(bf16), (((1,), (1,)), ((), ())), preferred_element_type=f32)


def _dot_tn(a, b):
    return lax.dot_general(a.astype(bf16), b.astype(bf16), (((0,), (0,)), ((), ())), preferred_element_type=f32)


def _split_dot(x, w):
    hi = x.astype(bf16)
    lo = (x - hi.astype(f32)).astype(bf16)
    return jnp.dot(hi, w, preferred_element_type=f32) + jnp.dot(lo, w, preferred_element_type=f32)


def _split3_dot(w, x):
    hi = x.astype(bf16)
    r1 = x - hi.astype(f32)
    mid = r1.astype(bf16)
    lo = (r1 - mid.astype(f32)).astype(bf16)
    return (jnp.dot(w, hi, preferred_element_type=f32) + jnp.dot(w, mid, preferred_element_type=f32)
            + jnp.dot(w, lo, preferred_element_type=f32))


def _rms(x, g):
    ms = jnp.mean(x * x, axis=-1, keepdims=True)
    return x * lax.rsqrt(ms + NORM_EPS) * g


def _sigmoid(x):
    return 1.0 / (1.0 + jnp.exp(-x))


def _inproj_body(x_ref, g_ref, w_ref, gb_ref, q_ref, kc_ref, vc_ref, ks_ref, vs_ref, kw_ref, vw_ref,
                 gate_ref, z_ref):
    h = _rms(x_ref[0], g_ref[...]).astype(bf16)
    q = jnp.dot(h, w_ref[:, _Q0:_KV0], preferred_element_type=f32) * (HEAD_DIM ** -0.5)
    first = lax.broadcasted_iota(jnp.int32, (1, PAIR), 1) < HEAD_DIM
    for hd in range(NSA_HEADS):
        two = q[:, (hd // 2) * PAIR:(hd // 2 + 1) * PAIR]
        if (hd % 2) != (hd // NSA_GROUP):
            two = pltpu.roll(two, HEAD_DIM, axis=1)
        keep = first if hd < NSA_GROUP else jnp.logical_not(first)
        q_ref[0, hd] = jnp.where(keep, two, 0.0).astype(bf16)
    kv = jnp.dot(h, w_ref[:, _KV0:_GATE0], preferred_element_type=f32)
    kc_ref[0] = kv[:, 0:128]
    vc_ref[0] = kv[:, 128:256]
    ks_ref[0] = kv[:, 256:384].astype(bf16)
    vs_ref[0] = kv[:, 384:512].astype(bf16)
    kw_ref[0] = kv[:, 512:640].astype(bf16)
    vw_ref[0] = kv[:, 640:768].astype(bf16)
    gl = jnp.dot(h, w_ref[:, _GATE0:_RW0], preferred_element_type=f32)
    gate_ref[0] = _sigmoid(gl + gb_ref[...])
    z_ref[0] = jnp.dot(h, w_ref[:, _RW0:_WCOLS], preferred_element_type=f32)


def _inproj(x, g, wcat, gbias, tm):
    B, S, D = x.shape
    tok = lambda w: pl.BlockSpec((1, tm, w), lambda b, i: (b, i, 0))
    const = lambda shp: pl.BlockSpec(shp, lambda b, i: (0,) * len(shp))
    return pl.pallas_call(
        _inproj_body,
        grid=(B, S // tm),
        in_specs=[tok(D), const((1, D)), const((D, _WCOLS)), const((1, 128))],
        out_specs=[pl.BlockSpec((1, NSA_HEADS, tm, PAIR), lambda b, i: (b, 0, i, 0))]
        + [tok(128)] * 7 + [tok(RWKV_COLS)],
        out_shape=[jax.ShapeDtypeStruct((B, NSA_HEADS, S, PAIR), bf16),
                   jax.ShapeDtypeStruct((B, S, 128), f32), jax.ShapeDtypeStruct((B, S, 128), f32)]
        + [jax.ShapeDtypeStruct((B, S, 128), bf16)] * 4
        + [jax.ShapeDtypeStruct((B, S, 128), f32), jax.ShapeDtypeStruct((B, S, RWKV_COLS), f32)],
        compiler_params=pltpu.CompilerParams(
            dimension_semantics=("arbitrary", "arbitrary"), vmem_limit_bytes=VMEM_LIMIT),
        name="inproj",
    )(x, g, wcat, gbias)


def _gelu_tanh(x):
    return x * (0.5 * (1.0 + jnp.tanh(np.sqrt(2.0 / np.pi) * (x + 0.044715 * (x * x * x)))))


def _compress_one(x_ref, pe_ref, w_ref, b1_ref, w2_ref):
    n = x_ref.shape[1] // CMP_STRIDE
    x = jnp.concatenate([x_ref[0, pl.ds(j, n, stride=CMP_STRIDE), :] for j in range(CMP_STRIDE)], axis=1)
    lo = _dot(x + pe_ref[0:1, :], w_ref[0])
    hi = _dot(x + pe_ref[1:2, :], w_ref[1])
    pre = lo + pltpu.roll(hi, n - 1, axis=0) + b1_ref[...]
    return _dot(_gelu_tanh(pre), w2_ref[...])


def _compress_body(xk_ref, xv_ref, pek_ref, wk_ref, bk_ref, w2k_ref, pev_ref, wv_ref, bv_ref, w2v_ref,
                   kc_ref, vc_ref):
    kc_ref[0] = _compress_one(xk_ref, pek_ref, wk_ref, bk_ref, w2k_ref).astype(bf16)
    vc_ref[0] = _compress_one(xv_ref, pev_ref, wv_ref, bv_ref, w2v_ref).astype(bf16)


def _compress(xk, xv, kparams, vparams):
    B, S, W = xk.shape
    NC = S // CMP_STRIDE
    const = lambda a: pl.BlockSpec(a.shape, lambda b: (0,) * a.ndim)
    seq = pl.BlockSpec((1, S, W), lambda b: (b, 0, 0))
    out = pl.BlockSpec((1, NC, PAIR), lambda b: (b, 0, 0))
    return pl.pallas_call(
        _compress_body,
        grid=(B,),
        in_specs=[seq, seq] + [const(a) for a in kparams] + [const(a) for a in vparams],
        out_specs=[out, out],
        out_shape=[jax.ShapeDtypeStruct((B, NC, PAIR), bf16)] * 2,
        compiler_params=pltpu.CompilerParams(dimension_semantics=("arbitrary",), vmem_limit_bytes=VMEM_LIMIT),
        name="compress",
    )(xk, xv, *kparams, *vparams)


def _alibi_key_columns(pos, ns, onehot):
    a = np.zeros((pos.shape[0], PAIR), np.float32)
    if onehot:
        a[np.arange(pos.shape[0]), pos // SEL_BLOCK] = 1.0
    a[:, ns] = -1.0
    a[:, ns + 1] = -1.0
    a[:, ns + 2] = pos // SEL_BLOCK
    a[:, ns + 3] = pos % SEL_BLOCK
    return jnp.asarray(a, dtype=bf16)


def _nsa_body(q_ref, kc_ref, vc_ref, ks_ref, vs_ref, kw_ref, vw_ref, gate_ref, selT_ref, auxk_ref, auxc_ref,
              o_ref, m_scr, acc_scr, sa_scr, sb_scr, rank_scr, *, tq, tk, seq):
    R, G, H = NSA_GROUP, NSA_KV_HEADS, NSA_HEADS
    t0 = pl.program_id(1) * tq
    nc = seq // CMP_STRIDE
    ns = seq // SEL_BLOCK
    gm = R * tq
    tok = (t0 + lax.broadcasted_iota(jnp.int32, (tq, 1), 0)).astype(f32)
    lane = lax.broadcasted_iota(jnp.int32, (1, PAIR), 1)
    first = lane < HEAD_DIM

    tl = t0 + lax.broadcasted_iota(jnp.int32, (8, tq), 1)
    rid = lax.broadcasted_iota(jnp.int32, (8, tq), 0)
    alibi_rows = jnp.where(rid == 0, ((tl // SEL_BLOCK) * SEL_BLOCK).astype(f32),
                           jnp.where(rid == 1, (tl % SEL_BLOCK).astype(f32),
                                     jnp.where(rid == 2, float(SEL_BLOCK), jnp.where(rid == 3, 1.0, 0.0))))
    aux0 = jnp.concatenate([jnp.zeros((ns, tq), f32), alibi_rows, jnp.zeros((PAIR - ns - 8, tq), f32)], axis=0).T

    def augment(q8, aux_by_group):
        aux8 = jnp.concatenate([aux_by_group[h // R] * jnp.where(lane < ns, 1.0, 2.0 ** (-(h + 1)))
                                for h in range(H)], axis=0)
        return jnp.concatenate([q8, aux8.astype(bf16)], axis=1)

    def per_head(a):
        return a.reshape(H, tq, a.shape[-1])

    def with_ones(v):
        one = jnp.ones((), v.dtype)
        return jnp.where(first, v, one), jnp.where(first, one, v)

    def weighted_values(p, v):
        v0, v1 = with_ones(v)
        return jnp.concatenate([jnp.dot(p[:gm], v0, preferred_element_type=f32),
                                jnp.dot(p[gm:], v1, preferred_element_type=f32)], axis=0)

    def split_sum(acc):
        top, bot = acc[:gm], acc[gm:]
        return jnp.concatenate([top / top[:, HEAD_DIM:HEAD_DIM + 1], bot / bot[:, 0:1]], axis=0)

    q8 = q_ref[0].reshape(H * tq, PAIR)
    qa = augment(q8, [aux0, aux0])

    kc_aug = jnp.concatenate([kc_ref[0], auxc_ref[...]], axis=1)
    sc = per_head(_dot_nt(qa, kc_aug))
    w0 = pl.multiple_of(jnp.maximum(t0 - WINDOW, 0), tq)
    wrows = pl.ds(w0, WINDOW + tq)
    kw_aug = jnp.concatenate([kw_ref[0, wrows, :], auxk_ref[wrows, :]], axis=1)
    sw = per_head(_dot_nt(qa, kw_aug))

    cend = (lax.broadcasted_iota(jnp.int32, (1, nc), 1) * CMP_STRIDE + (CMP_LEN - 1)).astype(f32)
    vis_c = tok >= cend
    any_c = (tok >= float(CMP_LEN - 1)).astype(f32)
    s = jnp.where(vis_c[None], sc, NEG_INF)
    e = jnp.exp(s - jnp.max(s, axis=-1, keepdims=True))
    p_cmp = e * (any_c[None] / jnp.sum(e, axis=-1, keepdims=True))
    o_cmp = _dot(p_cmp.reshape(H * tq, nc), vc_ref[0])

    jrow = lax.broadcasted_iota(jnp.int32, (ns, tq), 0)
    cur = (t0 + lax.broadcasted_iota(jnp.int32, (ns, tq), 1)) // SEL_BLOCK
    forced = (jrow == 0) | (jrow == cur) | (jrow == cur - 1)
    scores_t = []
    for g in range(G):
        psum = p_cmp[R * g]
        for r in range(1, R):
            psum = psum + p_cmp[R * g + r]
        imp_t = lax.dot_general(selT_ref[...], psum.astype(bf16), (((1,), (1,)), ((), ())),
                                preferred_element_type=f32)
        p_lo = (psum - psum.astype(bf16).astype(f32)).astype(bf16)
        imp_t = imp_t + lax.dot_general(selT_ref[...], p_lo, (((1,), (1,)), ((), ())),
                                        preferred_element_type=f32)
        scores_t.append(jnp.where(forced, FORCE_SCORE, jnp.where(jrow <= cur, imp_t, -1.0)))
        rank_scr[g] = scores_t[g]

    def rank_rows(i8, cnts):
        base = pl.multiple_of(i8 * 8, 8)
        out = []
        for g in range(G):
            rows8 = rank_scr[g, pl.ds(base, 8), :]
            cnt = cnts[g]
            for k in range(8):
                si = rows8[k:k + 1, :]
                cnt = cnt + jnp.where(jrow > base + k, jnp.where(si >= scores_t[g], 1.0, 0.0),
                                      jnp.where(si > scores_t[g], 1.0, 0.0))
            out.append(cnt)
        return tuple(out)

    n_rows8 = jnp.minimum((t0 + tq - 1) // SEL_BLOCK // 8 + 1, ns // 8)
    cnts = lax.fori_loop(0, n_rows8, rank_rows, tuple(jnp.zeros((ns, tq), f32) for _ in range(G)))
    aux_sel = []
    for g in range(G):
        bias_t = jnp.where(cnts[g] < float(SEL_TOPK), 0.0, NEG_INF)
        aux_sel.append(aux0 + jnp.concatenate([bias_t, jnp.zeros((PAIR - ns, tq), f32)], axis=0).T)
    qs = augment(q8, aux_sel)

    m_scr[...] = jnp.full(m_scr.shape, NEG_INF, f32)
    acc_scr[...] = jnp.zeros(acc_scr.shape, f32)

    def key_rows(kt):
        return pl.ds(pl.multiple_of(kt * tk, tk), tk)

    def scores(kt, dst):
        rows_ = key_rows(kt)
        dst[...] = _dot_nt(qs, jnp.concatenate([ks_ref[0, rows_, :], auxk_ref[rows_, :]], axis=1))

    def attend(src, kt, diag):
        s8 = src[...]
        if diag:
            kpos = (kt * tk + lax.broadcasted_iota(jnp.int32, (1, tk), 1)).astype(f32)
            s8 = jnp.where((tok >= kpos)[None], per_head(s8), NEG_INF).reshape(H * tq, tk)
        m_prev = m_scr[...][:, 0:1]
        m_new = jnp.maximum(m_prev, jnp.max(s8, axis=-1, keepdims=True))
        p = jnp.exp(s8 - m_new).astype(bf16)
        acc_scr[...] = jnp.exp(m_prev - m_new) * acc_scr[...] + weighted_values(p, vs_ref[0, key_rows(kt), :])
        m_scr[...] = jnp.broadcast_to(m_new, m_scr.shape)

    n_full = t0 // tk
    scores(n_full, sa_scr)

    kpos = w0 + lax.broadcasted_iota(jnp.int32, (1, WINDOW + tq), 1)
    dist_w = tok - kpos.astype(f32)
    vis_w = jnp.abs(dist_w - (WINDOW - 1) / 2.0) < WINDOW / 2.0
    s = jnp.where(vis_w[None], sw, NEG_INF)
    e = jnp.exp(s - jnp.max(s, axis=-1, keepdims=True)).astype(bf16).reshape(H * tq, WINDOW + tq)
    acc_win = weighted_values(e, vw_ref[0, wrows, :])

    scores(0, sb_scr)
    attend(sa_scr, n_full, True)

    def two_tiles(j, carry):
        scores(2 * j + 1, sa_scr)
        attend(sb_scr, 2 * j, False)
        scores(2 * j + 2, sb_scr)
        attend(sa_scr, 2 * j + 1, False)
        return carry

    lax.fori_loop(0, n_full // 2, two_tiles, 0)

    @pl.when(n_full % 2 == 1)
    def _():
        attend(sb_scr, n_full - 1, False)

    o_sel = split_sum(acc_scr[...])
    o_win = split_sum(acc_win)

    gate = gate_ref[0]
    o_cmp, o_sel, o_win = per_head(o_cmp), per_head(o_sel), per_head(o_win)
    for r in range(R):
        pair = []
        for g in range(G):
            h = R * g + r
            pair.append(gate[:, 3 * h:3 * h + 1] * o_cmp[h] + gate[:, 3 * h + 1:3 * h + 2] * o_sel[h]
                        + gate[:, 3 * h + 2:3 * h + 3] * o_win[h])
        o_ref[0, :, r * PAIR:(r + 1) * PAIR] = jnp.where(first, pair[0], pair[1])


def _nsa(q, kc, vc, ks, vs, kw, vw, gates, sel_t, tq, tk):
    B, H, S, _ = q.shape
    nc, ns = S // CMP_STRIDE, S // SEL_BLOCK
    aux_k = _alibi_key_columns(np.arange(S), ns, onehot=True)
    aux_c = _alibi_key_columns(np.arange(nc) * CMP_STRIDE + (CMP_LEN - 1), ns, onehot=False)
    full = lambda n: pl.BlockSpec((1, n, PAIR), lambda b, i: (b, 0, 0))
    const = lambda a: pl.BlockSpec(a.shape, lambda b, i: (0, 0))
    body = functools.partial(_nsa_body, tq=tq, tk=tk, seq=S)
    return pl.pallas_call(
        body,
        grid=(B, S // tq),
        in_specs=[pl.BlockSpec((1, H, tq, PAIR), lambda b, i: (b, 0, i, 0)),
                  full(nc), full(nc), full(S), full(S), full(S), full(S),
                  pl.BlockSpec((1, tq, 128), lambda b, i: (b, i, 0)),
                  const(sel_t), const(aux_k), const(aux_c)],
        out_specs=pl.BlockSpec((1, tq, NSA_GROUP * PAIR), lambda b, i: (b, i, 0)),
        out_shape=jax.ShapeDtypeStruct((B, S, NSA_GROUP * PAIR), f32),
        scratch_shapes=[pltpu.VMEM((H * tq, PAIR), f32)] * 2 + [pltpu.VMEM((H * tq, tk), f32)] * 2
        + [pltpu.VMEM((NSA_KV_HEADS, ns, tq), f32)],
        compiler_params=pltpu.CompilerParams(
            dimension_semantics=("arbitrary", "arbitrary"), vmem_limit_bytes=VMEM_LIMIT),
        name="nsa",
    )(q, kc, vc, ks, vs, kw, vw, gates, sel_t, aux_k, aux_c)


SCAN_GROUP = 4


def _rwkv_body(z_ref, zp_ref, mu_ref, w0_ref, wl_ref, a0_ref, al_ref, gl_ref, kkw_ref, ka_ref, rk_ref, bd_ref,
               lnw_ref, lnb_ref, y_ref, s_scr, r_s, lw_s, k_s, v_s, kk_s, b_s, g_s, bv_s, *, tt, nb):
    C = CHUNK
    n_pairs = RWKV_WIDTH // PAIR
    i = pl.program_id(1)

    @pl.when(i == 0)
    def _():
        s_scr[...] = jnp.zeros(s_scr.shape, f32)

    z = z_ref[...].reshape(nb * tt, RWKV_COLS)
    row = lax.broadcasted_iota(jnp.int32, (nb * tt, 1), 0)
    z_prev = pltpu.roll(z, 1, axis=0)
    for bb in range(nb):
        z_prev = jnp.where(row == bb * tt, jnp.where(i > 0, zp_ref[bb, 7:8, :], 0.0), z_prev)
    zs = z + (z_prev - z) * mu_ref[...]
    r = zs[:, 0:512]
    k = zs[:, 512:1024]
    v = zs[:, 1024:1536]
    lora = zs[:, 1536:1664]
    gd = zs[:, 1664:1792]
    wlog = w0_ref[...] + _dot(jnp.tanh(lora), wl_ref[...])
    sp = jnp.maximum(-wlog, 0.0) + jnp.log(1.0 + jnp.exp(-jnp.abs(wlog)))
    a = _sigmoid(a0_ref[...] + _dot(lora, al_ref[...]))
    kk = k * kkw_ref[...]
    kk = kk * lax.rsqrt(jnp.maximum(_split_dot(kk * kk, bd_ref[...]), 1e-24))
    k2 = k * (1.0 + (a - 1.0) * ka_ref[...])
    r_s[...] = r
    lw_s[...] = -jnp.exp(-sp - 0.5)
    k_s[...] = k2
    v_s[...] = v
    kk_s[...] = kk
    b_s[...] = kk * a
    g_s[...] = _dot(_sigmoid(gd), gl_ref[...])
    bv_s[...] = _split_dot(r * k2 * rk_ref[...], bd_ref[...]) * v

    ri = lax.broadcasted_iota(jnp.int32, (C, C), 0)
    ci = lax.broadcasted_iota(jnp.int32, (C, C), 1)
    cum_mat = jnp.where(ri >= ci, 1.0, 0.0).astype(bf16)
    ri2 = lax.broadcasted_iota(jnp.int32, (C, 2 * C), 0)
    ci2 = lax.broadcasted_iota(jnp.int32, (C, 2 * C), 1) % C
    tri2_incl = ri2 >= ci2
    tri2_strict = ri2 > ci2
    lane = lax.broadcasted_iota(jnp.int32, (1, PAIR), 1)
    first = lane < HEAD_DIM
    blockdiag = (lax.broadcasted_iota(jnp.int32, (PAIR, PAIR), 0) // HEAD_DIM
                 == lax.broadcasted_iota(jnp.int32, (PAIR, PAIR), 1) // HEAD_DIM)
    head_mean = jnp.where(blockdiag, 1.0, 0.0).astype(bf16)

    def swap_heads(x):
        return pltpu.roll(x, HEAD_DIM, axis=1)

    def by_head_rows(x):
        return jnp.concatenate([jnp.where(first, x, 0.0), jnp.where(first, 0.0, x)], axis=0)

    def block_diag(x2):
        zero = jnp.zeros((x2.shape[0], PAIR), x2.dtype)
        return jnp.concatenate([jnp.concatenate([x2[:, :PAIR], zero], axis=1),
                                jnp.concatenate([zero, x2[:, PAIR:]], axis=1)], axis=0)

    def group(gi):
        units = []
        for cc, bb in [(cc, bb) for cc in range(SCAN_GROUP) for bb in range(nb)]:
            local = pl.ds((gi * SCAN_GROUP + cc) * C, C)
            rows = pl.ds(bb * tt + (gi * SCAN_GROUP + cc) * C, C)
            r_, lw_, k_, v_, kk_, b_ = (s[rows, :] for s in (r_s, lw_s, k_s, v_s, kk_s, b_s))
            cum = _split3_dot(cum_mat, lw_)
            cum_end = cum[C - 1:C, :]
            e_neg = jnp.exp(-cum)
            e_end = jnp.exp(cum_end - cum)
            a_t = -kk_ * jnp.exp(cum - lw_)
            r_t = r_ * jnp.exp(cum)
            b_t = b_ * e_neg
            k_t = k_ * e_neg
            b_h = b_ * e_end
            k_h = k_ * e_end
            w_end = jnp.exp(cum_end)
            for p in range(n_pairs):
                pc = slice(p * PAIR, (p + 1) * PAIR)
                units.append(dict(p=bb * n_pairs + p, bb=bb, local=local, rows=rows, pc=pc, a_t=a_t[:, pc],
                                  r_t=r_t[:, pc], b_t=b_t[:, pc], k_t=k_t[:, pc], b_h=b_h[:, pc], k_h=k_h[:, pc],
                                  v=v_[:, pc], w_end=w_end[:, pc]))

        for u in units:
            lhs = jnp.concatenate([u["a_t"], u["r_t"]], axis=0)
            u["ab"] = _dot_nt(lhs, by_head_rows(u["b_t"]))
            u["ak"] = _dot_nt(lhs, by_head_rows(u["k_t"]))
        for u in units:
            ab, ak = u["ab"], u["ak"]
            u["a"] = jnp.where(tri2_strict, ab[:C], 0.0)
            u["a_rb"] = jnp.where(tri2_incl, ab[C:], 0.0)
            u["akrk"] = jnp.concatenate([jnp.where(tri2_strict, ak[:C], 0.0), jnp.where(tri2_incl, ak[C:], 0.0)],
                                        axis=0)
        for u in units:
            v_sw = swap_heads(u["v"]).astype(bf16)
            x1 = _dot(u["akrk"], block_diag(jnp.concatenate([v_sw, v_sw], axis=1)))
            u["x"] = jnp.concatenate([jnp.where(first, u["a_t"], x1[:C, :PAIR]),
                                      jnp.where(first, x1[:C, PAIR:], u["a_t"])], axis=1)
            u["arkv"] = x1[C:]
            u["pows"] = [u["a"], _dot(u["a"], by_head_rows(u["a"]))]
        for j in range(6):
            for u in units:
                u["x"] = u["x"] + _dot(u["pows"][j], block_diag(u["x"].astype(bf16)))
            if j + 2 <= 5:
                for u in units:
                    u["pows"].append(_dot(u["pows"][j + 1], by_head_rows(u["pows"][j + 1])))
        for u in units:
            u["ex"] = _dot(u["a_rb"], block_diag(u["x"].astype(bf16)))
        for u in units:
            x0, x1 = u["x"][:, :PAIR], u["x"][:, PAIR:]
            ex0, ex1 = u["ex"][:, :PAIR], u["ex"][:, PAIR:]
            ua = jnp.where(first, x0, x1)
            w2 = swap_heads(jnp.where(first, x1, x0))
            u["rq"] = u["r_t"] + jnp.where(first, ex0, ex1)
            u["yin"] = swap_heads(jnp.where(first, u["arkv"][:, PAIR:] + ex1, u["arkv"][:, :PAIR] + ex0))
            u["gmat"] = jnp.where(blockdiag, _dot_tn(ua, u["b_h"]), 0.0)
            u["qmat"] = jnp.where(blockdiag, _dot_tn(jnp.concatenate([w2, u["v"]], axis=0),
                                                     jnp.concatenate([u["b_h"], u["k_h"]], axis=0)), 0.0)
        state = [s_scr[p] for p in range(nb * n_pairs)]
        for u in units:
            s0 = state[u["p"]]
            u["y"] = _dot_nt(u["rq"], s0) + u["yin"]
            state[u["p"]] = s0 * u["w_end"] + _dot(s0, u["gmat"]) + u["qmat"]
        for p in range(nb * n_pairs):
            s_scr[p] = state[p]
        for u in units:
            y = u["y"]
            mu = jnp.dot(y.astype(bf16), head_mean, preferred_element_type=f32) * (1.0 / HEAD_DIM)
            u["d"] = y - mu
        for u in units:
            d = u["d"]
            var = jnp.dot((d * d).astype(bf16), head_mean, preferred_element_type=f32) * (1.0 / HEAD_DIM)
            yn = d * lax.rsqrt(var + GN_EPS) * lnw_ref[:, u["pc"]] + lnb_ref[:, u["pc"]]
            y_ref[u["bb"], u["local"], u["pc"]] = (yn + bv_s[u["rows"], u["pc"]]) * g_s[u["rows"], u["pc"]]

    for gi in range(tt // (C * SCAN_GROUP)):
        group(gi)


def _rwkv(z, params, tt, nb):
    B, S, W = z.shape
    const = lambda a: pl.BlockSpec(a.shape, lambda b, i: (0,) * a.ndim)
    prev = pl.BlockSpec((nb, 8, W), lambda b, i: (b, jnp.maximum(i * (tt // 8) - 1, 0), 0))
    return pl.pallas_call(
        functools.partial(_rwkv_body, tt=tt, nb=nb),
        grid=(B // nb, S // tt),
        in_specs=[pl.BlockSpec((nb, tt, W), lambda b, i: (b, i, 0)), prev] + [const(a) for a in params],
        out_specs=pl.BlockSpec((nb, tt, RWKV_WIDTH), lambda b, i: (b, i, 0)),
        out_shape=jax.ShapeDtypeStruct((B, S, RWKV_WIDTH), f32),
        scratch_shapes=[pltpu.VMEM((nb * RWKV_WIDTH // PAIR, PAIR, PAIR), f32)]
        + [pltpu.VMEM((nb * tt, RWKV_WIDTH), f32)] * 8,
        compiler_params=pltpu.CompilerParams(
            dimension_semantics=("arbitrary", "arbitrary"), vmem_limit_bytes=VMEM_LIMIT),
        name="rwkv",
    )(z, z, *params)


def _post_body(x_ref, yn_ref, yr_ref, p_ref, wo_ref, gpost_ref, gpre_ref, gmlp_ref, wup_ref, wdn_ref,
               wpg_ref, wple_ref, o_ref):
    y = jnp.concatenate([yn_ref[0], yr_ref[0]], axis=1).astype(bf16)
    mix = jnp.dot(y, wo_ref[...], preferred_element_type=f32)
    x1 = x_ref[0] + _rms(mix, gpost_ref[...])
    h = _rms(x1, gpre_ref[...]).astype(bf16)
    acc = None
    for c in range(D_FF // D_MODEL):
        cs = slice(c * D_MODEL, (c + 1) * D_MODEL)
        u = jnp.dot(h, wup_ref[:, cs], preferred_element_type=f32)
        u = jnp.square(jnp.maximum(u, 0.0)).astype(bf16)
        part = jnp.dot(u, wdn_ref[cs, :], preferred_element_type=f32)
        acc = part if acc is None else acc + part
    x2 = x1 + _rms(acc, gmlp_ref[...])
    gate = _sigmoid(jnp.dot(x2.astype(bf16), wpg_ref[...], preferred_element_type=f32))
    o_ref[0] = x2 + gate * jnp.dot(p_ref[0].astype(bf16), wple_ref[...], preferred_element_type=f32)


def _post(x, yn, yr, p, weights, tm):
    B, S, D = x.shape
    tok = lambda w: pl.BlockSpec((1, tm, w), lambda b, i: (b, i, 0))
    const = lambda a: pl.BlockSpec(a.shape, lambda b, i: (0,) * a.ndim, pipeline_mode=pl.Buffered(1))
    return pl.pallas_call(
        _post_body,
        grid=(B, S // tm),
        in_specs=[tok(D), tok(512), tok(512), tok(PLE_DIM)] + [const(a) for a in weights],
        out_specs=tok(D),
        out_shape=jax.ShapeDtypeStruct((B, S, D), f32),
        compiler_params=pltpu.CompilerParams(
            dimension_semantics=("arbitrary", "arbitrary"), vmem_limit_bytes=VMEM_LIMIT),
        name="post",
    )(x, yn, yr, p, *weights)


def _pack_inproj(w_in, gate_bias):
    wg = jnp.pad(w_in[:, 1280:1304], ((0, 0), (0, 128 - 24)))
    wcat = jnp.concatenate([w_in[:, :1280], wg, w_in[:, 1304:]], axis=1).astype(bf16)
    return wcat, jnp.pad(gate_bias, (0, 128 - 24)).reshape(1, 128)


def _pack_compress(pe, w1, b1, w2):
    eye2 = jnp.eye(NSA_KV_HEADS, dtype=f32)
    w1r = w1.reshape(CMP_LEN, HEAD_DIM, CMP_HIDDEN)
    halves = []
    for part in (w1r[:CMP_STRIDE], w1r[CMP_STRIDE:]):
        halves.append(jnp.einsum("jdc,gh->jgdhc", part, eye2).reshape(CMP_STRIDE * PAIR, 2 * CMP_HIDDEN))
    w = jnp.stack(halves).astype(bf16)
    per = jnp.broadcast_to(pe.reshape(2, CMP_STRIDE, 1, HEAD_DIM), (2, CMP_STRIDE, NSA_KV_HEADS, HEAD_DIM))
    per = per.reshape(2, CMP_STRIDE * PAIR)
    b1p = jnp.tile(b1, NSA_KV_HEADS).reshape(1, 2 * CMP_HIDDEN)
    w2p = jnp.einsum("cd,gh->gchd", w2, eye2).reshape(2 * CMP_HIDDEN, PAIR).astype(bf16)
    return per, w, b1p, w2p


def _sel_map_t(seq):
    nc, ns = seq // CMP_STRIDE, seq // SEL_BLOCK
    c0 = np.arange(nc) * CMP_STRIDE
    s0 = np.arange(ns) * SEL_BLOCK
    ov = (np.minimum(c0[:, None] + CMP_LEN - 1, s0[None, :] + SEL_BLOCK - 1)
          - np.maximum(c0[:, None], s0[None, :]) + 1)
    m = np.clip(ov, 0, None).astype(np.float32) / CMP_STRIDE
    m[nc - 1] = 0.0
    return jnp.asarray(m.T, dtype=bf16)


def _row(a):
    return a.reshape(1, -1)


def _mixers(x, g_mix_pre, w_in, nsa_gate_bias, cmp_k, cmp_v, shift_mu, w0, w_lora_up, a0, a_lora_up, g_lora_up,
            k_k, k_a, r_k, lnx_w, lnx_b):
    B, S, _ = x.shape
    bd = jnp.asarray(np.kron(np.eye(RWKV_WIDTH // HEAD_DIM), np.ones((HEAD_DIM, HEAD_DIM))), dtype=bf16)
    wcat, gbias = _pack_inproj(w_in, nsa_gate_bias)
    q, kc, vc, ks, vs, kw, vw, gates, z = _inproj(x, _row(g_mix_pre), wcat, gbias, tm=512)
    kcmp, vcmp = _compress(kc, vc, _pack_compress(*cmp_k), _pack_compress(*cmp_v))
    y_nsa = _nsa(q, kcmp, vcmp, ks, vs, kw, vw, gates, _sel_map_t(S), tq=256, tk=512)

    wl = jnp.concatenate([w_lora_up, jnp.zeros((ICLR_LORA, RWKV_WIDTH), f32)], axis=0).astype(bf16)
    al = jnp.concatenate([jnp.zeros((DECAY_LORA, RWKV_WIDTH), f32), a_lora_up], axis=0).astype(bf16)
    rwkv_params = (_row(shift_mu), _row(w0), wl, _row(a0), al, g_lora_up.astype(bf16),
                   _row(k_k), _row(k_a), _row(r_k), bd, _row(lnx_w), _row(lnx_b))
    y_rwkv = _rwkv(z, rwkv_params, tt=256, nb=1)
    return z, y_nsa, y_rwkv


def kernel(x, p, g_mix_pre, g_mix_post, g_mlp_pre, g_mlp_post, w_in, nsa_gate_bias, cmp_pe_k, cmp_k_w1, cmp_k_b1, cmp_k_w2, cmp_pe_v, cmp_v_w1, cmp_v_b1, cmp_v_w2, shift_mu, w0, w_lora_up, a0, a_lora_up, g_lora_up, k_k, k_a, r_k, lnx_w, lnx_b, w_out, w_up, w_down, w_ple, w_ple_gate):
    D = x.shape[-1]
    for i in range(p.shape[0]):
        _, y_nsa, y_rwkv = _mixers(
            x, g_mix_pre[i], w_in[i], nsa_gate_bias[i],
            (cmp_pe_k[i], cmp_k_w1[i], cmp_k_b1[i], cmp_k_w2[i]), (cmp_pe_v[i], cmp_v_w1[i], cmp_v_b1[i], cmp_v_w2[i]),
            shift_mu[i], w0[i], w_lora_up[i], a0[i], a_lora_up[i], g_lora_up[i], k_k[i], k_a[i], r_k[i],
            lnx_w[i], lnx_b[i])
        wo_nsa = w_out[i][:512].reshape(NSA_KV_HEADS, NSA_GROUP, HEAD_DIM, D).transpose(1, 0, 2, 3).reshape(512, D)
        wo = jnp.concatenate([wo_nsa, w_out[i][512:]], axis=0).astype(bf16)
        weights = (wo, _row(g_mix_post[i]), _row(g_mlp_pre[i]), _row(g_mlp_post[i]), w_up[i].astype(bf16),
                   w_down[i].astype(bf16), w_ple_gate[i].astype(bf16), w_ple[i].astype(bf16))
        x = _post(x, y_nsa, y_rwkv, p[i], weights, tm=512)
    return x
```

```python
import functools

import jax
import jax.numpy as jnp
import numpy as np
from jax import lax
from jax.experimental import pallas as pl
from jax.experimental.pallas import tpu as pltpu

f32 = jnp.float32
bf16 = jnp.bfloat16

D_MODEL = 1024
HEAD_DIM = 64
NSA_HEADS = 8
NSA_KV_HEADS = 2
NSA_GROUP = NSA_HEADS // NSA_KV_HEADS
CMP_LEN = 32
CMP_STRIDE = 16
CMP_HIDDEN = 2 * HEAD_DIM
SEL_BLOCK = 64
SEL_TOPK = 16
WINDOW = 512
RWKV_WIDTH = 512
RWKV_COLS = 1792
DECAY_LORA = 64
ICLR_LORA = 64
D_FF = 4 * D_MODEL
PLE_DIM = 256
NORM_EPS = 1e-6
GN_EPS = 64e-5
NEG_INF = -1e30
FORCE_SCORE = 1e4

PAIR = 2 * HEAD_DIM
CHUNK = 64
SCAN_GROUP = 4

INPROJ_TM = 512
NSA_TQ = 256
NSA_TK = 512
RWKV_TT = CHUNK * SCAN_GROUP
POST_TM = 512
VMEM_LIMIT = 56 * 1024 * 1024

_Q0, _KV0, _GATE0, _RW0, _WCOLS = 0, 512, 1280, 1408, 3200


def _dot(a, b):
    return jnp.dot(a.astype(bf16), b.astype(bf16), preferred_element_type=f32)


def _dot_nt(a, b):
    return lax.dot_general(a.astype(bf16), b.astype(bf16), (((1,), (1,)), ((), ())), preferred_element_type=f32)


def _dot_tn(a, b):
    return lax.dot_general(a.astype(bf16), b.astype(bf16), (((0,), (0,)), ((), ())), preferred_element_type=f32)


def _split3_dot(w, x):
    hi = x.astype(bf16)
    r1 = x - hi.astype(f32)
    mid = r1.astype(bf16)
    lo = (r1 - mid.astype(f32)).astype(bf16)
    return (jnp.dot(w, hi, preferred_element_type=f32) + jnp.dot(w, mid, preferred_element_type=f32)
            + jnp.dot(w, lo, preferred_element_type=f32))


def _rms(x, g):
    ms = jnp.mean(x * x, axis=-1, keepdims=True)
    return x * lax.rsqrt(ms + NORM_EPS) * g


def _sigmoid(x):
    return 1.0 / (1.0 + jnp.exp(-x))


def _inproj_body(x_ref, g_ref, w_ref, gb_ref, mu_ref, w0_ref, wl_ref, a0_ref, al_ref, gl_ref, kkw_ref, ka_ref,
                 rk_ref, bd_ref, q_ref, kc_ref, vc_ref, ks_ref, vs_ref, kw_ref, vw_ref, gate_ref,
                 r_ref, lw_ref, k_ref, v_ref, kk_ref, b_ref, g_out_ref, bv_ref, zlast_scr):
    h = _rms(x_ref[0], g_ref[...]).astype(bf16)
    z = jnp.dot(h, w_ref[:, _RW0:_WCOLS], preferred_element_type=f32)
    tm = z.shape[0]
    prev_row = jnp.where(pl.program_id(1) > 0, zlast_scr[7:8, :], 0.0)
    zlast_scr[...] = z[tm - 8:tm, :]
    q = jnp.dot(h, w_ref[:, _Q0:_KV0], preferred_element_type=f32) * (HEAD_DIM ** -0.5)
    first = lax.broadcasted_iota(jnp.int32, (1, PAIR), 1) < HEAD_DIM
    for hd in range(NSA_HEADS):
        two = q[:, (hd // 2) * PAIR:(hd // 2 + 1) * PAIR]
        if (hd % 2) != (hd // NSA_GROUP):
            two = pltpu.roll(two, HEAD_DIM, axis=1)
        keep = first if hd < NSA_GROUP else jnp.logical_not(first)
        q_ref[0, hd] = jnp.where(keep, two, 0.0).astype(bf16)
    kv = jnp.dot(h, w_ref[:, _KV0:_GATE0], preferred_element_type=f32)
    kc_ref[0] = kv[:, 0:128]
    vc_ref[0] = kv[:, 128:256]
    ks_ref[0] = kv[:, 256:384].astype(bf16)
    vs_ref[0] = kv[:, 384:512].astype(bf16)
    kw_ref[0] = kv[:, 512:640].astype(bf16)
    vw_ref[0] = kv[:, 640:768].astype(bf16)
    gl = jnp.dot(h, w_ref[:, _GATE0:_RW0], preferred_element_type=f32)
    gate_ref[0] = _sigmoid(gl + gb_ref[...])

    row = lax.broadcasted_iota(jnp.int32, (tm, 1), 0)
    z_prev = jnp.where(row == 0, prev_row, pltpu.roll(z, 1, axis=0))
    zs = z + (z_prev - z) * mu_ref[...]
    r = zs[:, 0:512]
    k = zs[:, 512:1024]
    v = zs[:, 1024:1536]
    lora = zs[:, 1536:1664]
    gd = zs[:, 1664:1792]
    wlog = w0_ref[...] + _dot(jnp.tanh(lora), wl_ref[...])
    sp = jnp.maximum(-wlog, 0.0) + jnp.log(1.0 + jnp.exp(-jnp.abs(wlog)))
    a = _sigmoid(a0_ref[...] + _dot(lora, al_ref[...]))
    kk = k * kkw_ref[...]
    kk = kk * lax.rsqrt(jnp.maximum(_dot(kk * kk, bd_ref[...]), 1e-24))
    k2 = k * (1.0 + (a - 1.0) * ka_ref[...])
    r_ref[0] = r
    lw_ref[0] = -jnp.exp(-sp - 0.5)
    k_ref[0] = k2
    v_ref[0] = v
    kk_ref[0] = kk
    b_ref[0] = kk * a
    g_out_ref[0] = _dot(_sigmoid(gd), gl_ref[...])
    bv_ref[0] = _dot(r * k2 * rk_ref[...], bd_ref[...]) * v


def _inproj(x, g, wcat, gbias, rwkv_params, tm):
    B, S, D = x.shape
    tok = lambda w: pl.BlockSpec((1, tm, w), lambda b, i: (b, i, 0))
    const = lambda a: pl.BlockSpec(a.shape, lambda b, i: (0,) * a.ndim)
    consts = (g, wcat, gbias) + tuple(rwkv_params)
    return pl.pallas_call(
        _inproj_body,
        grid=(B, S // tm),
        in_specs=[tok(D)] + [const(a) for a in consts],
        out_specs=[pl.BlockSpec((1, NSA_HEADS, tm, PAIR), lambda b, i: (b, 0, i, 0))]
        + [tok(128)] * 7 + [tok(RWKV_WIDTH)] * 8,
        out_shape=[jax.ShapeDtypeStruct((B, NSA_HEADS, S, PAIR), bf16),
                   jax.ShapeDtypeStruct((B, S, 128), f32), jax.ShapeDtypeStruct((B, S, 128), f32)]
        + [jax.ShapeDtypeStruct((B, S, 128), bf16)] * 4
        + [jax.ShapeDtypeStruct((B, S, 128), f32)] + [jax.ShapeDtypeStruct((B, S, RWKV_WIDTH), f32)] * 8,
        scratch_shapes=[pltpu.VMEM((8, RWKV_COLS), f32)],
        compiler_params=pltpu.CompilerParams(
            dimension_semantics=("arbitrary", "arbitrary"), vmem_limit_bytes=VMEM_LIMIT),
        name="inproj",
    )(x, *consts)


def _gelu_tanh(x):
    return x * (0.5 * (1.0 + jnp.tanh(np.sqrt(2.0 / np.pi) * (x + 0.044715 * (x * x * x)))))


def _compress_one(x_ref, pe_ref, w_ref, b1_ref, w2_ref):
    n = x_ref.shape[1] // CMP_STRIDE
    x = jnp.concatenate([x_ref[0, pl.ds(j, n, stride=CMP_STRIDE), :] for j in range(CMP_STRIDE)], axis=1)
    lo = _dot(x + pe_ref[0:1, :], w_ref[0])
    hi = _dot(x + pe_ref[1:2, :], w_ref[1])
    pre = lo + pltpu.roll(hi, n - 1, axis=0) + b1_ref[...]
    return _dot(_gelu_tanh(pre), w2_ref[...])


def _compress_body(xk_ref, xv_ref, pek_ref, wk_ref, bk_ref, w2k_ref, pev_ref, wv_ref, bv_ref, w2v_ref,
                   kc_ref, vc_ref):
    kc_ref[0] = _compress_one(xk_ref, pek_ref, wk_ref, bk_ref, w2k_ref).astype(bf16)
    vc_ref[0] = _compress_one(xv_ref, pev_ref, wv_ref, bv_ref, w2v_ref).astype(bf16)


def _compress(xk, xv, kparams, vparams):
    B, S, W = xk.shape
    NC = S // CMP_STRIDE
    const = lambda a: pl.BlockSpec(a.shape, lambda b: (0,) * a.ndim)
    seq = pl.BlockSpec((1, S, W), lambda b: (b, 0, 0))
    out = pl.BlockSpec((1, NC, PAIR), lambda b: (b, 0, 0))
    return pl.pallas_call(
        _compress_body,
        grid=(B,),
        in_specs=[seq, seq] + [const(a) for a in kparams] + [const(a) for a in vparams],
        out_specs=[out, out],
        out_shape=[jax.ShapeDtypeStruct((B, NC, PAIR), bf16)] * 2,
        compiler_params=pltpu.CompilerParams(dimension_semantics=("arbitrary",), vmem_limit_bytes=VMEM_LIMIT),
        name="compress",
    )(xk, xv, *kparams, *vparams)


def _alibi_key_columns(pos, ns, onehot):
    a = np.zeros((pos.shape[0], PAIR), np.float32)
    if onehot:
        a[np.arange(pos.shape[0]), pos // SEL_BLOCK] = 1.0
    a[:, ns] = -1.0
    a[:, ns + 1] = -1.0
    a[:, ns + 2] = pos // SEL_BLOCK
    a[:, ns + 3] = pos % SEL_BLOCK
    return jnp.asarray(a, dtype=bf16)


def _nsa_body(q_ref, kc_ref, vc_ref, ks_ref, vs_ref, kw_ref, vw_ref, gate_ref, selT_ref, auxk_ref, auxc_ref,
              o_ref, m_scr, acc_scr, sa_scr, sb_scr, rank_scr, *, tq, tk, seq):
    R, G, H = NSA_GROUP, NSA_KV_HEADS, NSA_HEADS
    t0 = pl.program_id(1) * tq
    nc = seq // CMP_STRIDE
    ns = seq // SEL_BLOCK
    gm = R * tq
    tok = (t0 + lax.broadcasted_iota(jnp.int32, (tq, 1), 0)).astype(f32)
    lane = lax.broadcasted_iota(jnp.int32, (1, PAIR), 1)
    first = lane < HEAD_DIM

    tl = t0 + lax.broadcasted_iota(jnp.int32, (8, tq), 1)
    rid = lax.broadcasted_iota(jnp.int32, (8, tq), 0)
    alibi_rows = jnp.where(rid == 0, ((tl // SEL_BLOCK) * SEL_BLOCK).astype(f32),
                           jnp.where(rid == 1, (tl % SEL_BLOCK).astype(f32),
                                     jnp.where(rid == 2, float(SEL_BLOCK), jnp.where(rid == 3, 1.0, 0.0))))
    aux0 = jnp.concatenate([jnp.zeros((ns, tq), f32), alibi_rows, jnp.zeros((PAIR - ns - 8, tq), f32)], axis=0).T

    def augment(q8, aux_by_group):
        aux8 = jnp.concatenate([aux_by_group[h // R] * jnp.where(lane < ns, 1.0, 2.0 ** (-(h + 1)))
                                for h in range(H)], axis=0)
        return jnp.concatenate([q8, aux8.astype(bf16)], axis=1)

    def per_head(a):
        return a.reshape(H, tq, a.shape[-1])

    def with_ones(v):
        one = jnp.ones((), v.dtype)
        return jnp.where(first, v, one), jnp.where(first, one, v)

    def weighted_values(p, v):
        v0, v1 = with_ones(v)
        return jnp.concatenate([jnp.dot(p[:gm], v0, preferred_element_type=f32),
                                jnp.dot(p[gm:], v1, preferred_element_type=f32)], axis=0)

    def split_sum(acc):
        top, bot = acc[:gm], acc[gm:]
        return jnp.concatenate([top / top[:, HEAD_DIM:HEAD_DIM + 1], bot / bot[:, 0:1]], axis=0)

    q8 = q_ref[0].reshape(H * tq, PAIR)
    qa = augment(q8, [aux0, aux0])

    kc_aug = jnp.concatenate([kc_ref[0], auxc_ref[...]], axis=1)
    sc = per_head(_dot_nt(qa, kc_aug))
    w0 = pl.multiple_of(jnp.maximum(t0 - WINDOW, 0), tq)
    wrows = pl.ds(w0, WINDOW + tq)
    kw_aug = jnp.concatenate([kw_ref[0, wrows, :], auxk_ref[wrows, :]], axis=1)
    sw = per_head(_dot_nt(qa, kw_aug))

    cend = (lax.broadcasted_iota(jnp.int32, (1, nc), 1) * CMP_STRIDE + (CMP_LEN - 1)).astype(f32)
    vis_c = tok >= cend
    any_c = (tok >= float(CMP_LEN - 1)).astype(f32)
    s = jnp.where(vis_c[None], sc, NEG_INF)
    e = jnp.exp(s - jnp.max(s, axis=-1, keepdims=True))
    p_cmp = e * (any_c[None] / jnp.sum(e, axis=-1, keepdims=True))
    o_cmp = _dot(p_cmp.reshape(H * tq, nc), vc_ref[0])

    jrow = lax.broadcasted_iota(jnp.int32, (ns, tq), 0)
    cur = (t0 + lax.broadcasted_iota(jnp.int32, (ns, tq), 1)) // SEL_BLOCK
    forced = (jrow == 0) | (jrow == cur) | (jrow == cur - 1)
    scores_t = []
    for g in range(G):
        psum = p_cmp[R * g]
        for r in range(1, R):
            psum = psum + p_cmp[R * g + r]
        imp_t = lax.dot_general(selT_ref[...], psum.astype(bf16), (((1,), (1,)), ((), ())),
                                preferred_element_type=f32)
        p_lo = (psum - psum.astype(bf16).astype(f32)).astype(bf16)
        imp_t = imp_t + lax.dot_general(selT_ref[...], p_lo, (((1,), (1,)), ((), ())),
                                        preferred_element_type=f32)
        scores_t.append(jnp.where(forced, FORCE_SCORE, jnp.where(jrow <= cur, imp_t, -1.0)))
        rank_scr[g] = scores_t[g]

    def rank_rows(i8, cnts):
        base = pl.multiple_of(i8 * 8, 8)
        out = []
        for g in range(G):
            rows8 = rank_scr[g, pl.ds(base, 8), :]
            cnt = cnts[g]
            for k in range(8):
                si = rows8[k:k + 1, :]
                cnt = cnt + jnp.where(jrow > base + k, jnp.where(si >= scores_t[g], 1.0, 0.0),
                                      jnp.where(si > scores_t[g], 1.0, 0.0))
            out.append(cnt)
        return tuple(out)

    n_rows8 = jnp.minimum((t0 + tq - 1) // SEL_BLOCK // 8 + 1, ns // 8)
    cnts = lax.fori_loop(0, n_rows8, rank_rows, tuple(jnp.zeros((ns, tq), f32) for _ in range(G)))
    aux_sel = []
    for g in range(G):
        bias_t = jnp.where(cnts[g] < float(SEL_TOPK), 0.0, NEG_INF)
        aux_sel.append(aux0 + jnp.concatenate([bias_t, jnp.zeros((PAIR - ns, tq), f32)], axis=0).T)
    qs = augment(q8, aux_sel)

    m_scr[...] = jnp.full(m_scr.shape, NEG_INF, f32)
    acc_scr[...] = jnp.zeros(acc_scr.shape, f32)

    def key_rows(kt):
        return pl.ds(pl.multiple_of(kt * tk, tk), tk)

    def scores(kt, dst):
        rows_ = key_rows(kt)
        dst[...] = _dot_nt(qs, jnp.concatenate([ks_ref[0, rows_, :], auxk_ref[rows_, :]], axis=1))

    def attend(src, kt, diag):
        s8 = src[...]
        if diag:
            kpos = (kt * tk + lax.broadcasted_iota(jnp.int32, (1, tk), 1)).astype(f32)
            s8 = jnp.where((tok >= kpos)[None], per_head(s8), NEG_INF).reshape(H * tq, tk)
        m_prev = m_scr[...][:, 0:1]
        m_new = jnp.maximum(m_prev, jnp.max(s8, axis=-1, keepdims=True))
        p = jnp.exp(s8 - m_new).astype(bf16)
        acc_scr[...] = jnp.exp(m_prev - m_new) * acc_scr[...] + weighted_values(p, vs_ref[0, key_rows(kt), :])
        m_scr[...] = jnp.broadcast_to(m_new, m_scr.shape)

    n_full = t0 // tk
    scores(n_full, sa_scr)

    kpos = w0 + lax.broadcasted_iota(jnp.int32, (1, WINDOW + tq), 1)
    dist_w = tok - kpos.astype(f32)
    vis_w = jnp.abs(dist_w - (WINDOW - 1) / 2.0) < WINDOW / 2.0
    s = jnp.where(vis_w[None], sw, NEG_INF)
    e = jnp.exp(s - jnp.max(s, axis=-1, keepdims=True)).astype(bf16).reshape(H * tq, WINDOW + tq)
    acc_win = weighted_values(e, vw_ref[0, wrows, :])

    scores(0, sb_scr)
    attend(sa_scr, n_full, True)

    def two_tiles(j, carry):
        scores(2 * j + 1, sa_scr)
        attend(sb_scr, 2 * j, False)
        scores(2 * j + 2, sb_scr)
        attend(sa_scr, 2 * j + 1, False)
        return carry

    lax.fori_loop(0, n_full // 2, two_tiles, 0)

    @pl.when(n_full % 2 == 1)
    def _():
        attend(sb_scr, n_full - 1, False)

    o_sel = split_sum(acc_scr[...])
    o_win = split_sum(acc_win)

    gate = gate_ref[0]
    o_cmp, o_sel, o_win = per_head(o_cmp), per_head(o_sel), per_head(o_win)
    for r in range(R):
        pair = []
        for g in range(G):
            h = R * g + r
            pair.append(gate[:, 3 * h:3 * h + 1] * o_cmp[h] + gate[:, 3 * h + 1:3 * h + 2] * o_sel[h]
                        + gate[:, 3 * h + 2:3 * h + 3] * o_win[h])
        o_ref[0, :, r * PAIR:(r + 1) * PAIR] = jnp.where(first, pair[0], pair[1])


def _nsa(q, kc, vc, ks, vs, kw, vw, gates, sel_t, tq, tk):
    B, H, S, _ = q.shape
    nc, ns = S // CMP_STRIDE, S // SEL_BLOCK
    aux_k = _alibi_key_columns(np.arange(S), ns, onehot=True)
    aux_c = _alibi_key_columns(np.arange(nc) * CMP_STRIDE + (CMP_LEN - 1), ns, onehot=False)
    full = lambda n: pl.BlockSpec((1, n, PAIR), lambda b, i: (b, 0, 0))
    const = lambda a: pl.BlockSpec(a.shape, lambda b, i: (0, 0))
    body = functools.partial(_nsa_body, tq=tq, tk=tk, seq=S)
    return pl.pallas_call(
        body,
        grid=(B, S // tq),
        in_specs=[pl.BlockSpec((1, H, tq, PAIR), lambda b, i: (b, 0, i, 0)),
                  full(nc), full(nc), full(S), full(S), full(S), full(S),
                  pl.BlockSpec((1, tq, 128), lambda b, i: (b, i, 0)),
                  const(sel_t), const(aux_k), const(aux_c)],
        out_specs=pl.BlockSpec((1, tq, NSA_GROUP * PAIR), lambda b, i: (b, i, 0)),
        out_shape=jax.ShapeDtypeStruct((B, S, NSA_GROUP * PAIR), f32),
        scratch_shapes=[pltpu.VMEM((H * tq, PAIR), f32)] * 2 + [pltpu.VMEM((H * tq, tk), f32)] * 2
        + [pltpu.VMEM((NSA_KV_HEADS, ns, tq), f32)],
        compiler_params=pltpu.CompilerParams(
            dimension_semantics=("arbitrary", "arbitrary"), vmem_limit_bytes=VMEM_LIMIT),
        name="nsa",
    )(q, kc, vc, ks, vs, kw, vw, gates, sel_t, aux_k, aux_c)


def _rwkv_body(r_s, lw_s, k_s, v_s, kk_s, b_s, g_s, bv_s, lnw_ref, lnb_ref, y_ref, s_scr, *, tt):
    C = CHUNK
    n_pairs = RWKV_WIDTH // PAIR

    @pl.when(pl.program_id(1) == 0)
    def _():
        s_scr[...] = jnp.zeros(s_scr.shape, f32)

    ri = lax.broadcasted_iota(jnp.int32, (C, C), 0)
    ci = lax.broadcasted_iota(jnp.int32, (C, C), 1)
    cum_mat = jnp.where(ri >= ci, 1.0, 0.0).astype(bf16)
    ri2 = lax.broadcasted_iota(jnp.int32, (C, 2 * C), 0)
    ci2 = lax.broadcasted_iota(jnp.int32, (C, 2 * C), 1) % C
    tri2_incl = ri2 >= ci2
    tri2_strict = ri2 > ci2
    lane = lax.broadcasted_iota(jnp.int32, (1, PAIR), 1)
    first = lane < HEAD_DIM
    blockdiag = (lax.broadcasted_iota(jnp.int32, (PAIR, PAIR), 0) // HEAD_DIM
                 == lax.broadcasted_iota(jnp.int32, (PAIR, PAIR), 1) // HEAD_DIM)
    head_mean = jnp.where(blockdiag, 1.0, 0.0).astype(bf16)

    def swap_heads(x):
        return pltpu.roll(x, HEAD_DIM, axis=1)

    def by_head_rows(x):
        return jnp.concatenate([jnp.where(first, x, 0.0), jnp.where(first, 0.0, x)], axis=0)

    def block_diag(x2):
        zero = jnp.zeros((x2.shape[0], PAIR), x2.dtype)
        return jnp.concatenate([jnp.concatenate([x2[:, :PAIR], zero], axis=1),
                                jnp.concatenate([zero, x2[:, PAIR:]], axis=1)], axis=0)

    def group(gi):
        units = []
        for cc in range(SCAN_GROUP):
            rows = pl.ds((gi * SCAN_GROUP + cc) * C, C)
            r_, lw_, k_, v_, kk_, b_ = (s[0, rows, :] for s in (r_s, lw_s, k_s, v_s, kk_s, b_s))
            cum = _split3_dot(cum_mat, lw_)
            cum_end = cum[C - 1:C, :]
            e_neg = jnp.exp(-cum)
            e_end = jnp.exp(cum_end - cum)
            a_t = -kk_ * jnp.exp(cum - lw_)
            r_t = r_ * jnp.exp(cum)
            b_t = b_ * e_neg
            k_t = k_ * e_neg
            b_h = b_ * e_end
            k_h = k_ * e_end
            w_end = jnp.exp(cum_end)
            for p in range(n_pairs):
                pc = slice(p * PAIR, (p + 1) * PAIR)
                units.append(dict(p=p, rows=rows, pc=pc, a_t=a_t[:, pc],
                                  r_t=r_t[:, pc], b_t=b_t[:, pc], k_t=k_t[:, pc], b_h=b_h[:, pc], k_h=k_h[:, pc],
                                  v=v_[:, pc], w_end=w_end[:, pc]))

        for u in units:
            lhs = jnp.concatenate([u["a_t"], u["r_t"]], axis=0)
            u["ab"] = _dot_nt(lhs, by_head_rows(u["b_t"]))
            u["ak"] = _dot_nt(lhs, by_head_rows(u["k_t"]))
        for u in units:
            ab, ak = u["ab"], u["ak"]
            u["a"] = jnp.where(tri2_strict, ab[:C], 0.0)
            u["a_rb"] = jnp.where(tri2_incl, ab[C:], 0.0)
            u["akrk"] = jnp.concatenate([jnp.where(tri2_strict, ak[:C], 0.0), jnp.where(tri2_incl, ak[C:], 0.0)],
                                        axis=0)
        for u in units:
            v_sw = swap_heads(u["v"]).astype(bf16)
            x1 = _dot(u["akrk"], block_diag(jnp.concatenate([v_sw, v_sw], axis=1)))
            u["x"] = jnp.concatenate([jnp.where(first, u["a_t"], x1[:C, :PAIR]),
                                      jnp.where(first, x1[:C, PAIR:], u["a_t"])], axis=1)
            u["arkv"] = x1[C:]
            u["pows"] = [u["a"], _dot(u["a"], by_head_rows(u["a"]))]
        for j in range(6):
            for u in units:
                u["x"] = u["x"] + _dot(u["pows"][j], block_diag(u["x"].astype(bf16)))
            if j + 2 <= 5:
                for u in units:
                    u["pows"].append(_dot(u["pows"][j + 1], by_head_rows(u["pows"][j + 1])))
        for u in units:
            u["ex"] = _dot(u["a_rb"], block_diag(u["x"].astype(bf16)))
        for u in units:
            x0, x1 = u["x"][:, :PAIR], u["x"][:, PAIR:]
            ex0, ex1 = u["ex"][:, :PAIR], u["ex"][:, PAIR:]
            ua = jnp.where(first, x0, x1)
            w2 = swap_heads(jnp.where(first, x1, x0))
            u["rq"] = u["r_t"] + jnp.where(first, ex0, ex1)
            u["yin"] = swap_heads(jnp.where(first, u["arkv"][:, PAIR:] + ex1, u["arkv"][:, :PAIR] + ex0))
            u["gmat"] = jnp.where(blockdiag, _dot_tn(ua, u["b_h"]), 0.0)
            u["qmat"] = jnp.where(blockdiag, _dot_tn(jnp.concatenate([w2, u["v"]], axis=0),
                                                     jnp.concatenate([u["b_h"], u["k_h"]], axis=0)), 0.0)
        state = [s_scr[p] for p in range(n_pairs)]
        for u in units:
            s0 = state[u["p"]]
            u["y"] = _dot_nt(u["rq"], s0) + u["yin"]
            state[u["p"]] = s0 * u["w_end"] + _dot(s0, u["gmat"]) + u["qmat"]
        for p in range(n_pairs):
            s_scr[p] = state[p]
        for u in units:
            y = u["y"]
            mu = jnp.dot(y.astype(bf16), head_mean, preferred_element_type=f32) * (1.0 / HEAD_DIM)
            u["d"] = y - mu
        for u in units:
            d = u["d"]
            var = jnp.dot((d * d).astype(bf16), head_mean, preferred_element_type=f32) * (1.0 / HEAD_DIM)
            yn = d * lax.rsqrt(var + GN_EPS) * lnw_ref[:, u["pc"]] + lnb_ref[:, u["pc"]]
            y_ref[0, u["rows"], u["pc"]] = (yn + bv_s[0, u["rows"], u["pc"]]) * g_s[0, u["rows"], u["pc"]]

    for gi in range(tt // (C * SCAN_GROUP)):
        group(gi)


def _rwkv(r, lw, k, v, kk, b, g, bv, lnw, lnb, tt):
    B, S, W = r.shape
    tok = pl.BlockSpec((1, tt, W), lambda bb, i: (bb, i, 0))
    const = pl.BlockSpec((1, W), lambda bb, i: (0, 0))
    return pl.pallas_call(
        functools.partial(_rwkv_body, tt=tt),
        grid=(B, S // tt),
        in_specs=[tok] * 8 + [const, const],
        out_specs=tok,
        out_shape=jax.ShapeDtypeStruct((B, S, W), f32),
        scratch_shapes=[pltpu.VMEM((W // PAIR, PAIR, PAIR), f32)],
        compiler_params=pltpu.CompilerParams(
            dimension_semantics=("arbitrary", "arbitrary"), vmem_limit_bytes=VMEM_LIMIT),
        name="rwkv",
    )(r, lw, k, v, kk, b, g, bv, lnw, lnb)


def _post_body(x_ref, yn_ref, yr_ref, p_ref, wo_ref, gpost_ref, gpre_ref, gmlp_ref, wup_ref, wdn_ref,
               wpg_ref, wple_ref, o_ref):
    y = jnp.concatenate([yn_ref[0], yr_ref[0]], axis=1).astype(bf16)
    mix = jnp.dot(y, wo_ref[...], preferred_element_type=f32)
    x1 = x_ref[0] + _rms(mix, gpost_ref[...])
    h = _rms(x1, gpre_ref[...]).astype(bf16)
    acc = None
    for c in range(D_FF // D_MODEL):
        cs = slice(c * D_MODEL, (c + 1) * D_MODEL)
        u = jnp.dot(h, wup_ref[:, cs], preferred_element_type=f32)
        u = jnp.square(jnp.maximum(u, 0.0)).astype(bf16)
        part = jnp.dot(u, wdn_ref[cs, :], preferred_element_type=f32)
        acc = part if acc is None else acc + part
    x2 = x1 + _rms(acc, gmlp_ref[...])
    gate = _sigmoid(jnp.dot(x2.astype(bf16), wpg_ref[...], preferred_element_type=f32))
    o_ref[0] = x2 + gate * jnp.dot(p_ref[0].astype(bf16), wple_ref[...], preferred_element_type=f32)


def _post(x, yn, yr, p, weights, tm):
    B, S, D = x.shape
    tok = lambda w: pl.BlockSpec((1, tm, w), lambda b, i: (b, i, 0))
    const = lambda a: pl.BlockSpec(a.shape, lambda b, i: (0,) * a.ndim, pipeline_mode=pl.Buffered(1))
    return pl.pallas_call(
        _post_body,
        grid=(B, S // tm),
        in_specs=[tok(D), tok(512), tok(512), tok(PLE_DIM)] + [const(a) for a in weights],
        out_specs=tok(D),
        out_shape=jax.ShapeDtypeStruct((B, S, D), f32),
        compiler_params=pltpu.CompilerParams(
            dimension_semantics=("arbitrary", "arbitrary"), vmem_limit_bytes=VMEM_LIMIT),
        name="post",
    )(x, yn, yr, p, *weights)


def _pack_inproj(w_in, gate_bias):
    wg = jnp.pad(w_in[:, 1280:1304], ((0, 0), (0, 128 - 24)))
    wcat = jnp.concatenate([w_in[:, :1280], wg, w_in[:, 1304:]], axis=1).astype(bf16)
    return wcat, jnp.pad(gate_bias, (0, 128 - 24)).reshape(1, 128)


def _pack_compress(pe, w1, b1, w2):
    eye2 = jnp.eye(NSA_KV_HEADS, dtype=f32)
    w1r = w1.reshape(CMP_LEN, HEAD_DIM, CMP_HIDDEN)
    halves = []
    for part in (w1r[:CMP_STRIDE], w1r[CMP_STRIDE:]):
        halves.append(jnp.einsum("jdc,gh->jgdhc", part, eye2).reshape(CMP_STRIDE * PAIR, 2 * CMP_HIDDEN))
    w = jnp.stack(halves).astype(bf16)
    per = jnp.broadcast_to(pe.reshape(2, CMP_STRIDE, 1, HEAD_DIM), (2, CMP_STRIDE, NSA_KV_HEADS, HEAD_DIM))
    per = per.reshape(2, CMP_STRIDE * PAIR)
    b1p = jnp.tile(b1, NSA_KV_HEADS).reshape(1, 2 * CMP_HIDDEN)
    w2p = jnp.einsum("cd,gh->gchd", w2, eye2).reshape(2 * CMP_HIDDEN, PAIR).astype(bf16)
    return per, w, b1p, w2p


def _sel_map_t(seq):
    nc, ns = seq // CMP_STRIDE, seq // SEL_BLOCK
    c0 = np.arange(nc) * CMP_STRIDE
    s0 = np.arange(ns) * SEL_BLOCK
    ov = (np.minimum(c0[:, None] + CMP_LEN - 1, s0[None, :] + SEL_BLOCK - 1)
          - np.maximum(c0[:, None], s0[None, :]) + 1)
    m = np.clip(ov, 0, None).astype(np.float32) / CMP_STRIDE
    m[nc - 1] = 0.0
    return jnp.asarray(m.T, dtype=bf16)


def _row(a):
    return a.reshape(1, -1)


def _mixers(x, g_mix_pre, w_in, nsa_gate_bias, cmp_k, cmp_v, shift_mu, w0, w_lora_up, a0, a_lora_up, g_lora_up,
            k_k, k_a, r_k, lnx_w, lnx_b):
    S = x.shape[1]
    bd = jnp.asarray(np.kron(np.eye(RWKV_WIDTH // HEAD_DIM), np.ones((HEAD_DIM, HEAD_DIM))), dtype=bf16)
    wcat, gbias = _pack_inproj(w_in, nsa_gate_bias)
    wl = jnp.concatenate([w_lora_up, jnp.zeros((ICLR_LORA, RWKV_WIDTH), f32)], axis=0).astype(bf16)
    al = jnp.concatenate([jnp.zeros((DECAY_LORA, RWKV_WIDTH), f32), a_lora_up], axis=0).astype(bf16)
    rwkv_params = (_row(shift_mu), _row(w0), wl, _row(a0), al, g_lora_up.astype(bf16),
                   _row(k_k), _row(k_a), _row(r_k), bd)
    q, kc, vc, ks, vs, kw, vw, gates, r, lw, k2, v, kk, b, g, bv = _inproj(
        x, _row(g_mix_pre), wcat, gbias, rwkv_params, tm=INPROJ_TM)
    kcmp, vcmp = _compress(kc, vc, _pack_compress(*cmp_k), _pack_compress(*cmp_v))
    y_nsa = _nsa(q, kcmp, vcmp, ks, vs, kw, vw, gates, _sel_map_t(S), tq=NSA_TQ, tk=NSA_TK)
    y_rwkv = _rwkv(r, lw, k2, v, kk, b, g, bv, _row(lnx_w), _row(lnx_b), tt=RWKV_TT)
    return r, y_nsa, y_rwkv


def kernel(x, p, g_mix_pre, g_mix_post, g_mlp_pre, g_mlp_post, w_in, nsa_gate_bias, cmp_pe_k, cmp_k_w1, cmp_k_b1, cmp_k_w2, cmp_pe_v, cmp_v_w1, cmp_v_b1, cmp_v_w2, shift_mu, w0, w_lora_up, a0, a_lora_up, g_lora_up, k_k, k_a, r_k, lnx_w, lnx_b, w_out, w_up, w_down, w_ple, w_ple_gate):
    D = x.shape[-1]
    for i in range(p.shape[0]):
        _, y_nsa, y_rwkv = _mixers(
            x, g_mix_pre[i], w_in[i], nsa_gate_bias[i],
            (cmp_pe_k[i], cmp_k_w1[i], cmp_k_b1[i], cmp_k_w2[i]), (cmp_pe_v[i], cmp_v_w1[i], cmp_v_b1[i], cmp_v_w2[i]),
            shift_mu[i], w0[i], w_lora_up[i], a0[i], a_lora_up[i], g_lora_up[i], k_k[i], k_a[i], r_k[i],
            lnx_w[i], lnx_b[i])
        wo_nsa = w_out[i][:512].reshape(NSA_KV_HEADS, NSA_GROUP, HEAD_DIM, D).transpose(1, 0, 2, 3).reshape(512, D)
        wo = jnp.concatenate([wo_nsa, w_out[i][512:]], axis=0).astype(bf16)
        weights = (wo, _row(g_mix_post[i]), _row(g_mlp_pre[i]), _row(g_mlp_post[i]), w_up[i].astype(bf16),
                   w_down[i].astype(bf16), w_ple_gate[i].astype(bf16), w_ple[i].astype(bf16))
        x = _post(x, y_nsa, y_rwkv, p[i], weights, tm=POST_TM)
    return x
```

```python
import functools

import jax
import jax.numpy as jnp
import numpy as np
from jax import lax
from jax.experimental import pallas as pl
from jax.experimental.pallas import tpu as pltpu

f32 = jnp.float32
bf16 = jnp.bfloat16

D_MODEL = 1024
HEAD_DIM = 64
NSA_HEADS = 8
NSA_KV_HEADS = 2
NSA_GROUP = NSA_HEADS // NSA_KV_HEADS
CMP_LEN = 32
CMP_STRIDE = 16
CMP_HIDDEN = 2 * HEAD_DIM
SEL_BLOCK = 64
SEL_TOPK = 16
WINDOW = 512
RWKV_WIDTH = 512
RWKV_COLS = 1792
DECAY_LORA = 64
ICLR_LORA = 64
D_FF = 4 * D_MODEL
PLE_DIM = 256
NORM_EPS = 1e-6
GN_EPS = 64e-5
NEG_INF = -1e30
FORCE_SCORE = 1e4

PAIR = 2 * HEAD_DIM
CHUNK = 64
SCAN_GROUP = 4

INPROJ_TM = 512
NSA_TQ = 256
NSA_TK = 512
RWKV_TT = CHUNK * SCAN_GROUP
POST_TM = 512
VMEM_LIMIT = 56 * 1024 * 1024

_Q0, _KV0, _GATE0, _RW0, _WCOLS = 0, 512, 1280, 1408, 3200


def _dot(a, b):
    return jnp.dot(a.astype(bf16), b.astype(bf16), preferred_element_type=f32)


def _dot_nt(a, b):
    return lax.dot_general(a.astype(bf16), b.astype(bf16), (((1,), (1,)), ((), ())), preferred_element_type=f32)


def _dot_tn(a, b):
    return lax.dot_general(a.astype(bf16), b.astype(bf16), (((0,), (0,)), ((), ())), preferred_element_type=f32)


def _split3_dot(w, x):
    hi = x.astype(bf16)
    r1 = x - hi.astype(f32)
    mid = r1.astype(bf16)
    lo = (r1 - mid.astype(f32)).astype(bf16)
    return (jnp.dot(w, hi, preferred_element_type=f32) + jnp.dot(w, mid, preferred_element_type=f32)
            + jnp.dot(w, lo, preferred_element_type=f32))


def _rms(x, g):
    ms = jnp.mean(x * x, axis=-1, keepdims=True)
    return x * lax.rsqrt(ms + NORM_EPS) * g


def _sigmoid(x):
    return 1.0 / (1.0 + jnp.exp(-x))


def _inproj_body(x_ref, g_ref, w_ref, gb_ref, mu_ref, w0_ref, wl_ref, a0_ref, al_ref, gl_ref, kkw_ref, ka_ref,
                 rk_ref, bd_ref, q_ref, kc_ref, vc_ref, ks_ref, vs_ref, kw_ref, vw_ref, gate_ref,
                 r_ref, lw_ref, k_ref, v_ref, kk_ref, b_ref, g_out_ref, bv_ref, zlast_scr):
    h = _rms(x_ref[0], g_ref[...]).astype(bf16)
    z = jnp.dot(h, w_ref[:, _RW0:_WCOLS], preferred_element_type=f32)
    tm = z.shape[0]
    prev_row = jnp.where(pl.program_id(1) > 0, zlast_scr[7:8, :], 0.0)
    zlast_scr[...] = z[tm - 8:tm, :]
    q = jnp.dot(h, w_ref[:, _Q0:_KV0], preferred_element_type=f32) * (HEAD_DIM ** -0.5)
    first = lax.broadcasted_iota(jnp.int32, (1, PAIR), 1) < HEAD_DIM
    for hd in range(NSA_HEADS):
        two = q[:, (hd // 2) * PAIR:(hd // 2 + 1) * PAIR]
        if (hd % 2) != (hd // NSA_GROUP):
            two = pltpu.roll(two, HEAD_DIM, axis=1)
        keep = first if hd < NSA_GROUP else jnp.logical_not(first)
        q_ref[0, hd] = jnp.where(keep, two, 0.0).astype(bf16)
    kv = jnp.dot(h, w_ref[:, _KV0:_GATE0], preferred_element_type=f32)
    kc_ref[0] = kv[:, 0:128]
    vc_ref[0] = kv[:, 128:256]
    ks_ref[0] = kv[:, 256:384].astype(bf16)
    vs_ref[0] = kv[:, 384:512].astype(bf16)
    kw_ref[0] = kv[:, 512:640].astype(bf16)
    vw_ref[0] = kv[:, 640:768].astype(bf16)
    gl = jnp.dot(h, w_ref[:, _GATE0:_RW0], preferred_element_type=f32)
    gate_ref[0] = _sigmoid(gl + gb_ref[...])

    row = lax.broadcasted_iota(jnp.int32, (tm, 1), 0)
    z_prev = jnp.where(row == 0, prev_row, pltpu.roll(z, 1, axis=0))
    zs = z + (z_prev - z) * mu_ref[...]
    r = zs[:, 0:512]
    k = zs[:, 512:1024]
    v = zs[:, 1024:1536]
    lora = zs[:, 1536:1664]
    gd = zs[:, 1664:1792]
    wlog = w0_ref[...] + _dot(jnp.tanh(lora), wl_ref[...])
    sp = jnp.maximum(-wlog, 0.0) + jnp.log(1.0 + jnp.exp(-jnp.abs(wlog)))
    a = _sigmoid(a0_ref[...] + _dot(lora, al_ref[...]))
    kk = k * kkw_ref[...]
    kk = kk * lax.rsqrt(jnp.maximum(_dot(kk * kk, bd_ref[...]), 1e-24))
    k2 = k * (1.0 + (a - 1.0) * ka_ref[...])
    r_ref[0] = r
    lw_ref[0] = -jnp.exp(-sp - 0.5)
    k_ref[0] = k2
    v_ref[0] = v.astype(bf16)
    kk_ref[0] = kk
    b_ref[0] = kk * a
    g_out_ref[0] = _dot(_sigmoid(gd), gl_ref[...]).astype(bf16)
    bv_ref[0] = (_dot(r * k2 * rk_ref[...], bd_ref[...]) * v).astype(bf16)


def _inproj(x, g, wcat, gbias, rwkv_params, tm):
    B, S, D = x.shape
    tok = lambda w: pl.BlockSpec((1, tm, w), lambda b, i: (b, i, 0))
    const = lambda a: pl.BlockSpec(a.shape, lambda b, i: (0,) * a.ndim)
    consts = (g, wcat, gbias) + tuple(rwkv_params)
    return pl.pallas_call(
        _inproj_body,
        grid=(B, S // tm),
        in_specs=[tok(D)] + [const(a) for a in consts],
        out_specs=[pl.BlockSpec((1, NSA_HEADS, tm, PAIR), lambda b, i: (b, 0, i, 0))]
        + [tok(128)] * 7 + [tok(RWKV_WIDTH)] * 8,
        out_shape=[jax.ShapeDtypeStruct((B, NSA_HEADS, S, PAIR), bf16),
                   jax.ShapeDtypeStruct((B, S, 128), f32), jax.ShapeDtypeStruct((B, S, 128), f32)]
        + [jax.ShapeDtypeStruct((B, S, 128), bf16)] * 4
        + [jax.ShapeDtypeStruct((B, S, 128), f32)]
        + [jax.ShapeDtypeStruct((B, S, RWKV_WIDTH), dt) for dt in (f32, f32, f32, bf16, f32, f32, bf16, bf16)],
        scratch_shapes=[pltpu.VMEM((8, RWKV_COLS), f32)],
        compiler_params=pltpu.CompilerParams(
            dimension_semantics=("arbitrary", "arbitrary"), vmem_limit_bytes=VMEM_LIMIT),
        name="inproj",
    )(x, *consts)


def _gelu_tanh(x):
    return x * (0.5 * (1.0 + jnp.tanh(np.sqrt(2.0 / np.pi) * (x + 0.044715 * (x * x * x)))))


def _compress_one(x_ref, pe_ref, w_ref, b1_ref, w2_ref):
    n = x_ref.shape[1] // CMP_STRIDE
    x = jnp.concatenate([x_ref[0, pl.ds(j, n, stride=CMP_STRIDE), :] for j in range(CMP_STRIDE)], axis=1)
    lo = _dot(x + pe_ref[0:1, :], w_ref[0])
    hi = _dot(x + pe_ref[1:2, :], w_ref[1])
    pre = lo + pltpu.roll(hi, n - 1, axis=0) + b1_ref[...]
    return _dot(_gelu_tanh(pre), w2_ref[...])


def _compress_body(xk_ref, xv_ref, pek_ref, wk_ref, bk_ref, w2k_ref, pev_ref, wv_ref, bv_ref, w2v_ref,
                   kc_ref, vc_ref):
    kc_ref[0] = _compress_one(xk_ref, pek_ref, wk_ref, bk_ref, w2k_ref).astype(bf16)
    vc_ref[0] = _compress_one(xv_ref, pev_ref, wv_ref, bv_ref, w2v_ref).astype(bf16)


def _compress(xk, xv, kparams, vparams):
    B, S, W = xk.shape
    NC = S // CMP_STRIDE
    const = lambda a: pl.BlockSpec(a.shape, lambda b: (0,) * a.ndim)
    seq = pl.BlockSpec((1, S, W), lambda b: (b, 0, 0))
    out = pl.BlockSpec((1, NC, PAIR), lambda b: (b, 0, 0))
    return pl.pallas_call(
        _compress_body,
        grid=(B,),
        in_specs=[seq, seq] + [const(a) for a in kparams] + [const(a) for a in vparams],
        out_specs=[out, out],
        out_shape=[jax.ShapeDtypeStruct((B, NC, PAIR), bf16)] * 2,
        compiler_params=pltpu.CompilerParams(dimension_semantics=("arbitrary",), vmem_limit_bytes=VMEM_LIMIT),
        name="compress",
    )(xk, xv, *kparams, *vparams)


def _alibi_key_columns(pos, ns, onehot):
    a = np.zeros((pos.shape[0], PAIR), np.float32)
    if onehot:
        a[np.arange(pos.shape[0]), pos // SEL_BLOCK] = 1.0
    a[:, ns] = -1.0
    a[:, ns + 1] = -1.0
    a[:, ns + 2] = pos // SEL_BLOCK
    a[:, ns + 3] = pos % SEL_BLOCK
    return jnp.asarray(a, dtype=bf16)


def _nsa_body(q_ref, kc_ref, vc_ref, ks_ref, vs_ref, kw_ref, vw_ref, gate_ref, selT_ref, auxk_ref, auxc_ref,
              o_ref, m_scr, acc_scr, sa_scr, sb_scr, rank_scr, *, tq, tk, seq):
    R, G, H = NSA_GROUP, NSA_KV_HEADS, NSA_HEADS
    t0 = pl.program_id(1) * tq
    nc = seq // CMP_STRIDE
    ns = seq // SEL_BLOCK
    gm = R * tq
    tok = (t0 + lax.broadcasted_iota(jnp.int32, (tq, 1), 0)).astype(f32)
    lane = lax.broadcasted_iota(jnp.int32, (1, PAIR), 1)
    first = lane < HEAD_DIM

    tl = t0 + lax.broadcasted_iota(jnp.int32, (8, tq), 1)
    rid = lax.broadcasted_iota(jnp.int32, (8, tq), 0)
    alibi_rows = jnp.where(rid == 0, ((tl // SEL_BLOCK) * SEL_BLOCK).astype(f32),
                           jnp.where(rid == 1, (tl % SEL_BLOCK).astype(f32),
                                     jnp.where(rid == 2, float(SEL_BLOCK), jnp.where(rid == 3, 1.0, 0.0))))
    aux0 = jnp.concatenate([jnp.zeros((ns, tq), f32), alibi_rows, jnp.zeros((PAIR - ns - 8, tq), f32)], axis=0).T

    def augment(q8, aux_by_group):
        aux8 = jnp.concatenate([aux_by_group[h // R] * jnp.where(lane < ns, 1.0, 2.0 ** (-(h + 1)))
                                for h in range(H)], axis=0)
        return jnp.concatenate([q8, aux8.astype(bf16)], axis=1)

    def per_head(a):
        return a.reshape(H, tq, a.shape[-1])

    def with_ones(v):
        one = jnp.ones((), v.dtype)
        return jnp.where(first, v, one), jnp.where(first, one, v)

    def weighted_values(p, v):
        v0, v1 = with_ones(v)
        return jnp.concatenate([jnp.dot(p[:gm], v0, preferred_element_type=f32),
                                jnp.dot(p[gm:], v1, preferred_element_type=f32)], axis=0)

    def split_sum(acc):
        top, bot = acc[:gm], acc[gm:]
        return jnp.concatenate([top / top[:, HEAD_DIM:HEAD_DIM + 1], bot / bot[:, 0:1]], axis=0)

    q8 = q_ref[0].reshape(H * tq, PAIR)
    qa = augment(q8, [aux0, aux0])

    kc_aug = jnp.concatenate([kc_ref[0], auxc_ref[...]], axis=1)
    sc = per_head(_dot_nt(qa, kc_aug))
    w0 = pl.multiple_of(jnp.maximum(t0 - WINDOW, 0), tq)
    wrows = pl.ds(w0, WINDOW + tq)
    kw_aug = jnp.concatenate([kw_ref[0, wrows, :], auxk_ref[wrows, :]], axis=1)
    sw = per_head(_dot_nt(qa, kw_aug))

    cend = (lax.broadcasted_iota(jnp.int32, (1, nc), 1) * CMP_STRIDE + (CMP_LEN - 1)).astype(f32)
    vis_c = tok >= cend
    any_c = (tok >= float(CMP_LEN - 1)).astype(f32)
    s = jnp.where(vis_c[None], sc, NEG_INF)
    e = jnp.exp(s - jnp.max(s, axis=-1, keepdims=True))
    p_cmp = e * (any_c[None] / jnp.sum(e, axis=-1, keepdims=True))
    o_cmp = _dot(p_cmp.reshape(H * tq, nc), vc_ref[0])

    jrow = lax.broadcasted_iota(jnp.int32, (ns, tq), 0)
    cur = (t0 + lax.broadcasted_iota(jnp.int32, (ns, tq), 1)) // SEL_BLOCK
    forced = (jrow == 0) | (jrow == cur) | (jrow == cur - 1)
    scores_t = []
    for g in range(G):
        psum = p_cmp[R * g]
        for r in range(1, R):
            psum = psum + p_cmp[R * g + r]
        imp_t = lax.dot_general(selT_ref[...], psum.astype(bf16), (((1,), (1,)), ((), ())),
                                preferred_element_type=f32)
        p_lo = (psum - psum.astype(bf16).astype(f32)).astype(bf16)
        imp_t = imp_t + lax.dot_general(selT_ref[...], p_lo, (((1,), (1,)), ((), ())),
                                        preferred_element_type=f32)
        scores_t.append(jnp.where(forced, FORCE_SCORE, jnp.where(jrow <= cur, imp_t, -1.0)))
        rank_scr[g] = scores_t[g]

    def rank_rows(i8, cnts):
        base = pl.multiple_of(i8 * 8, 8)
        out = []
        for g in range(G):
            rows8 = rank_scr[g, pl.ds(base, 8), :]
            cnt = cnts[g]
            for k in range(8):
                si = rows8[k:k + 1, :]
                cnt = cnt + jnp.where(jrow > base + k, jnp.where(si >= scores_t[g], 1.0, 0.0),
                                      jnp.where(si > scores_t[g], 1.0, 0.0))
            out.append(cnt)
        return tuple(out)

    n_rows8 = jnp.minimum((t0 + tq - 1) // SEL_BLOCK // 8 + 1, ns // 8)
    cnts = lax.fori_loop(0, n_rows8, rank_rows, tuple(jnp.zeros((ns, tq), f32) for _ in range(G)))
    aux_sel = []
    for g in range(G):
        bias_t = jnp.where(cnts[g] < float(SEL_TOPK), 0.0, NEG_INF)
        aux_sel.append(aux0 + jnp.concatenate([bias_t, jnp.zeros((PAIR - ns, tq), f32)], axis=0).T)
    qs = augment(q8, aux_sel)

    m_scr[...] = jnp.full(m_scr.shape, NEG_INF, f32)
    acc_scr[...] = jnp.zeros(acc_scr.shape, f32)

    def key_rows(kt):
        return pl.ds(pl.multiple_of(kt * tk, tk), tk)

    def scores(kt, dst):
        rows_ = key_rows(kt)
        dst[...] = _dot_nt(qs, jnp.concatenate([ks_ref[0, rows_, :], auxk_ref[rows_, :]], axis=1))

    def attend(src, kt, diag):
        s8 = src[...]
        if diag:
            kpos = (kt * tk + lax.broadcasted_iota(jnp.int32, (1, tk), 1)).astype(f32)
            s8 = jnp.where((tok >= kpos)[None], per_head(s8), NEG_INF).reshape(H * tq, tk)
        m_prev = m_scr[...][:, 0:1]
        m_new = jnp.maximum(m_prev, jnp.max(s8, axis=-1, keepdims=True))
        p = jnp.exp(s8 - m_new).astype(bf16)
        acc_scr[...] = jnp.exp(m_prev - m_new) * acc_scr[...] + weighted_values(p, vs_ref[0, key_rows(kt), :])
        m_scr[...] = jnp.broadcast_to(m_new, m_scr.shape)

    n_full = t0 // tk
    scores(n_full, sa_scr)

    kpos = w0 + lax.broadcasted_iota(jnp.int32, (1, WINDOW + tq), 1)
    dist_w = tok - kpos.astype(f32)
    vis_w = jnp.abs(dist_w - (WINDOW - 1) / 2.0) < WINDOW / 2.0
    s = jnp.where(vis_w[None], sw, NEG_INF)
    e = jnp.exp(s - jnp.max(s, axis=-1, keepdims=True)).astype(bf16).reshape(H * tq, WINDOW + tq)
    acc_win = weighted_values(e, vw_ref[0, wrows, :])

    scores(0, sb_scr)
    attend(sa_scr, n_full, True)

    def two_tiles(j, carry):
        scores(2 * j + 1, sa_scr)
        attend(sb_scr, 2 * j, False)
        scores(2 * j + 2, sb_scr)
        attend(sa_scr, 2 * j + 1, False)
        return carry

    lax.fori_loop(0, n_full // 2, two_tiles, 0)

    @pl.when(n_full % 2 == 1)
    def _():
        attend(sb_scr, n_full - 1, False)

    o_sel = split_sum(acc_scr[...])
    o_win = split_sum(acc_win)

    gate = gate_ref[0]
    o_cmp, o_sel, o_win = per_head(o_cmp), per_head(o_sel), per_head(o_win)
    for r in range(R):
        pair = []
        for g in range(G):
            h = R * g + r
            pair.append(gate[:, 3 * h:3 * h + 1] * o_cmp[h] + gate[:, 3 * h + 1:3 * h + 2] * o_sel[h]
                        + gate[:, 3 * h + 2:3 * h + 3] * o_win[h])
        o_ref[0, :, r * PAIR:(r + 1) * PAIR] = jnp.where(first, pair[0], pair[1])


def _nsa(q, kc, vc, ks, vs, kw, vw, gates, sel_t, tq, tk):
    B, H, S, _ = q.shape
    nc, ns = S // CMP_STRIDE, S // SEL_BLOCK
    aux_k = _alibi_key_columns(np.arange(S), ns, onehot=True)
    aux_c = _alibi_key_columns(np.arange(nc) * CMP_STRIDE + (CMP_LEN - 1), ns, onehot=False)
    full = lambda n: pl.BlockSpec((1, n, PAIR), lambda b, i: (b, 0, 0))
    const = lambda a: pl.BlockSpec(a.shape, lambda b, i: (0, 0))
    body = functools.partial(_nsa_body, tq=tq, tk=tk, seq=S)
    return pl.pallas_call(
        body,
        grid=(B, S // tq),
        in_specs=[pl.BlockSpec((1, H, tq, PAIR), lambda b, i: (b, 0, i, 0)),
                  full(nc), full(nc), full(S), full(S), full(S), full(S),
                  pl.BlockSpec((1, tq, 128), lambda b, i: (b, i, 0)),
                  const(sel_t), const(aux_k), const(aux_c)],
        out_specs=pl.BlockSpec((1, tq, NSA_GROUP * PAIR), lambda b, i: (b, i, 0)),
        out_shape=jax.ShapeDtypeStruct((B, S, NSA_GROUP * PAIR), f32),
        scratch_shapes=[pltpu.VMEM((H * tq, PAIR), f32)] * 2 + [pltpu.VMEM((H * tq, tk), f32)] * 2
        + [pltpu.VMEM((NSA_KV_HEADS, ns, tq), f32)],
        compiler_params=pltpu.CompilerParams(
            dimension_semantics=("arbitrary", "arbitrary"), vmem_limit_bytes=VMEM_LIMIT),
        name="nsa",
    )(q, kc, vc, ks, vs, kw, vw, gates, sel_t, aux_k, aux_c)


def _rwkv_body(r_s, lw_s, k_s, v_s, kk_s, b_s, g_s, bv_s, lnw_ref, lnb_ref, y_ref, s_scr, *, tt):
    C = CHUNK
    n_pairs = RWKV_WIDTH // PAIR

    @pl.when(pl.program_id(1) == 0)
    def _():
        s_scr[...] = jnp.zeros(s_scr.shape, f32)

    ri = lax.broadcasted_iota(jnp.int32, (C, C), 0)
    ci = lax.broadcasted_iota(jnp.int32, (C, C), 1)
    cum_mat = jnp.where(ri >= ci, 1.0, 0.0).astype(bf16)
    ri2 = lax.broadcasted_iota(jnp.int32, (C, 2 * C), 0)
    ci2 = lax.broadcasted_iota(jnp.int32, (C, 2 * C), 1) % C
    tri2_incl = ri2 >= ci2
    tri2_strict = ri2 > ci2
    lane = lax.broadcasted_iota(jnp.int32, (1, PAIR), 1)
    first = lane < HEAD_DIM
    blockdiag = (lax.broadcasted_iota(jnp.int32, (PAIR, PAIR), 0) // HEAD_DIM
                 == lax.broadcasted_iota(jnp.int32, (PAIR, PAIR), 1) // HEAD_DIM)
    head_mean = jnp.where(blockdiag, 1.0, 0.0).astype(bf16)

    def swap_heads(x):
        return pltpu.roll(x, HEAD_DIM, axis=1)

    def by_head_rows(x):
        return jnp.concatenate([jnp.where(first, x, 0.0), jnp.where(first, 0.0, x)], axis=0)

    def block_diag(x2):
        zero = jnp.zeros((x2.shape[0], PAIR), x2.dtype)
        return jnp.concatenate([jnp.concatenate([x2[:, :PAIR], zero], axis=1),
                                jnp.concatenate([zero, x2[:, PAIR:]], axis=1)], axis=0)

    def group(gi):
        units = []
        for cc in range(SCAN_GROUP):
            rows = pl.ds((gi * SCAN_GROUP + cc) * C, C)
            r_, lw_, k_, v_, kk_, b_ = (s[0, rows, :].astype(f32) for s in (r_s, lw_s, k_s, v_s, kk_s, b_s))
            cum = _split3_dot(cum_mat, lw_)
            cum_end = cum[C - 1:C, :]
            e_neg = jnp.exp(-cum)
            e_end = jnp.exp(cum_end - cum)
            a_t = -kk_ * jnp.exp(cum - lw_)
            r_t = r_ * jnp.exp(cum)
            b_t = b_ * e_neg
            k_t = k_ * e_neg
            b_h = b_ * e_end
            k_h = k_ * e_end
            w_end = jnp.exp(cum_end)
            for p in range(n_pairs):
                pc = slice(p * PAIR, (p + 1) * PAIR)
                units.append(dict(p=p, rows=rows, pc=pc, a_t=a_t[:, pc],
                                  r_t=r_t[:, pc], b_t=b_t[:, pc], k_t=k_t[:, pc], b_h=b_h[:, pc], k_h=k_h[:, pc],
                                  v=v_[:, pc], w_end=w_end[:, pc]))

        for u in units:
            lhs = jnp.concatenate([u["a_t"], u["r_t"]], axis=0)
            u["ab"] = _dot_nt(lhs, by_head_rows(u["b_t"]))
            u["ak"] = _dot_nt(lhs, by_head_rows(u["k_t"]))
        for u in units:
            ab, ak = u["ab"], u["ak"]
            u["a"] = jnp.where(tri2_strict, ab[:C], 0.0)
            u["a_rb"] = jnp.where(tri2_incl, ab[C:], 0.0)
            u["akrk"] = jnp.concatenate([jnp.where(tri2_strict, ak[:C], 0.0), jnp.where(tri2_incl, ak[C:], 0.0)],
                                        axis=0)
        for u in units:
            v_sw = swap_heads(u["v"]).astype(bf16)
            x1 = _dot(u["akrk"], block_diag(jnp.concatenate([v_sw, v_sw], axis=1)))
            u["x"] = jnp.concatenate([jnp.where(first, u["a_t"], x1[:C, :PAIR]),
                                      jnp.where(first, x1[:C, PAIR:], u["a_t"])], axis=1)
            u["arkv"] = x1[C:]
            u["pows"] = [u["a"], _dot(u["a"], by_head_rows(u["a"]))]
        for j in range(6):
            for u in units:
                u["x"] = u["x"] + _dot(u["pows"][j], block_diag(u["x"].astype(bf16)))
            if j + 2 <= 5:
                for u in units:
                    u["pows"].append(_dot(u["pows"][j + 1], by_head_rows(u["pows"][j + 1])))
        for u in units:
            u["ex"] = _dot(u["a_rb"], block_diag(u["x"].astype(bf16)))
        for u in units:
            x0, x1 = u["x"][:, :PAIR], u["x"][:, PAIR:]
            ex0, ex1 = u["ex"][:, :PAIR], u["ex"][:, PAIR:]
            ua = jnp.where(first, x0, x1)
            w2 = swap_heads(jnp.where(first, x1, x0))
            u["rq"] = u["r_t"] + jnp.where(first, ex0, ex1)
            u["yin"] = swap_heads(jnp.where(first, u["arkv"][:, PAIR:] + ex1, u["arkv"][:, :PAIR] + ex0))
            u["gmat"] = jnp.where(blockdiag, _dot_tn(ua, u["b_h"]), 0.0)
            u["qmat"] = jnp.where(blockdiag, _dot_tn(jnp.concatenate([w2, u["v"]], axis=0),
                                                     jnp.concatenate([u["b_h"], u["k_h"]], axis=0)), 0.0)
        state = [s_scr[p] for p in range(n_pairs)]
        for u in units:
            s0 = state[u["p"]]
            u["y"] = _dot_nt(u["rq"], s0) + u["yin"]
            state[u["p"]] = s0 * u["w_end"] + _dot(s0, u["gmat"]) + u["qmat"]
        for p in range(n_pairs):
            s_scr[p] = state[p]
        for u in units:
            y = u["y"]
            mu = jnp.dot(y.astype(bf16), head_mean, preferred_element_type=f32) * (1.0 / HEAD_DIM)
            u["d"] = y - mu
        for u in units:
            d = u["d"]
            var = jnp.dot((d * d).astype(bf16), head_mean, preferred_element_type=f32) * (1.0 / HEAD_DIM)
            yn = d * lax.rsqrt(var + GN_EPS) * lnw_ref[:, u["pc"]] + lnb_ref[:, u["pc"]]
            y_ref[0, u["rows"], u["pc"]] = ((yn + bv_s[0, u["rows"], u["pc"]].astype(f32))
                                            * g_s[0, u["rows"], u["pc"]].astype(f32))

    for gi in range(tt // (C * SCAN_GROUP)):
        group(gi)


def _rwkv(r, lw, k, v, kk, b, g, bv, lnw, lnb, tt):
    B, S, W = r.shape
    tok = pl.BlockSpec((1, tt, W), lambda bb, i: (bb, i, 0))
    const = pl.BlockSpec((1, W), lambda bb, i: (0, 0))
    return pl.pallas_call(
        functools.partial(_rwkv_body, tt=tt),
        grid=(B, S // tt),
        in_specs=[tok] * 8 + [const, const],
        out_specs=tok,
        out_shape=jax.ShapeDtypeStruct((B, S, W), f32),
        scratch_shapes=[pltpu.VMEM((W // PAIR, PAIR, PAIR), f32)],
        compiler_params=pltpu.CompilerParams(
            dimension_semantics=("arbitrary", "arbitrary"), vmem_limit_bytes=VMEM_LIMIT),
        name="rwkv",
    )(r, lw, k, v, kk, b, g, bv, lnw, lnb)


def _post_body(x_ref, yn_ref, yr_ref, p_ref, wo_ref, gpost_ref, gpre_ref, gmlp_ref, wup_ref, wdn_ref,
               wpg_ref, wple_ref, o_ref):
    y = jnp.concatenate([yn_ref[0], yr_ref[0]], axis=1).astype(bf16)
    mix = jnp.dot(y, wo_ref[...], preferred_element_type=f32)
    x1 = x_ref[0] + _rms(mix, gpost_ref[...])
    h = _rms(x1, gpre_ref[...]).astype(bf16)
    acc = None
    for c in range(D_FF // D_MODEL):
        cs = slice(c * D_MODEL, (c + 1) * D_MODEL)
        u = jnp.dot(h, wup_ref[:, cs], preferred_element_type=f32)
        u = jnp.square(jnp.maximum(u, 0.0)).astype(bf16)
        part = jnp.dot(u, wdn_ref[cs, :], preferred_element_type=f32)
        acc = part if acc is None else acc + part
    x2 = x1 + _rms(acc, gmlp_ref[...])
    gate = _sigmoid(jnp.dot(x2.astype(bf16), wpg_ref[...], preferred_element_type=f32))
    o_ref[0] = x2 + gate * jnp.dot(p_ref[0].astype(bf16), wple_ref[...], preferred_element_type=f32)


def _post(x, yn, yr, p, weights, tm):
    B, S, D = x.shape
    tok = lambda w: pl.BlockSpec((1, tm, w), lambda b, i: (b, i, 0))
    const = lambda a: pl.BlockSpec(a.shape, lambda b, i: (0,) * a.ndim, pipeline_mode=pl.Buffered(1))
    return pl.pallas_call(
        _post_body,
        grid=(B, S // tm),
        in_specs=[tok(D), tok(512), tok(512), tok(PLE_DIM)] + [const(a) for a in weights],
        out_specs=tok(D),
        out_shape=jax.ShapeDtypeStruct((B, S, D), f32),
        compiler_params=pltpu.CompilerParams(
            dimension_semantics=("arbitrary", "arbitrary"), vmem_limit_bytes=VMEM_LIMIT),
        name="post",
    )(x, yn, yr, p, *weights)


def _pack_inproj(w_in, gate_bias):
    wg = jnp.pad(w_in[:, 1280:1304], ((0, 0), (0, 128 - 24)))
    wcat = jnp.concatenate([w_in[:, :1280], wg, w_in[:, 1304:]], axis=1).astype(bf16)
    return wcat, jnp.pad(gate_bias, (0, 128 - 24)).reshape(1, 128)


def _pack_compress(pe, w1, b1, w2):
    eye2 = jnp.eye(NSA_KV_HEADS, dtype=f32)
    w1r = w1.reshape(CMP_LEN, HEAD_DIM, CMP_HIDDEN)
    halves = []
    for part in (w1r[:CMP_STRIDE], w1r[CMP_STRIDE:]):
        halves.append(jnp.einsum("jdc,gh->jgdhc", part, eye2).reshape(CMP_STRIDE * PAIR, 2 * CMP_HIDDEN))
    w = jnp.stack(halves).astype(bf16)
    per = jnp.broadcast_to(pe.reshape(2, CMP_STRIDE, 1, HEAD_DIM), (2, CMP_STRIDE, NSA_KV_HEADS, HEAD_DIM))
    per = per.reshape(2, CMP_STRIDE * PAIR)
    b1p = jnp.tile(b1, NSA_KV_HEADS).reshape(1, 2 * CMP_HIDDEN)
    w2p = jnp.einsum("cd,gh->gchd", w2, eye2).reshape(2 * CMP_HIDDEN, PAIR).astype(bf16)
    return per, w, b1p, w2p


def _sel_map_t(seq):
    nc, ns = seq // CMP_STRIDE, seq // SEL_BLOCK
    c0 = np.arange(nc) * CMP_STRIDE
    s0 = np.arange(ns) * SEL_BLOCK
    ov = (np.minimum(c0[:, None] + CMP_LEN - 1, s0[None, :] + SEL_BLOCK - 1)
          - np.maximum(c0[:, None], s0[None, :]) + 1)
    m = np.clip(ov, 0, None).astype(np.float32) / CMP_STRIDE
    m[nc - 1] = 0.0
    return jnp.asarray(m.T, dtype=bf16)


def _row(a):
    return a.reshape(1, -1)


def _mixers(x, g_mix_pre, w_in, nsa_gate_bias, cmp_k, cmp_v, shift_mu, w0, w_lora_up, a0, a_lora_up, g_lora_up,
            k_k, k_a, r_k, lnx_w, lnx_b):
    S = x.shape[1]
    bd = jnp.asarray(np.kron(np.eye(RWKV_WIDTH // HEAD_DIM), np.ones((HEAD_DIM, HEAD_DIM))), dtype=bf16)
    wcat, gbias = _pack_inproj(w_in, nsa_gate_bias)
    wl = jnp.concatenate([w_lora_up, jnp.zeros((ICLR_LORA, RWKV_WIDTH), f32)], axis=0).astype(bf16)
    al = jnp.concatenate([jnp.zeros((DECAY_LORA, RWKV_WIDTH), f32), a_lora_up], axis=0).astype(bf16)
    rwkv_params = (_row(shift_mu), _row(w0), wl, _row(a0), al, g_lora_up.astype(bf16),
                   _row(k_k), _row(k_a), _row(r_k), bd)
    q, kc, vc, ks, vs, kw, vw, gates, r, lw, k2, v, kk, b, g, bv = _inproj(
        x, _row(g_mix_pre), wcat, gbias, rwkv_params, tm=INPROJ_TM)
    kcmp, vcmp = _compress(kc, vc, _pack_compress(*cmp_k), _pack_compress(*cmp_v))
    y_nsa = _nsa(q, kcmp, vcmp, ks, vs, kw, vw, gates, _sel_map_t(S), tq=NSA_TQ, tk=NSA_TK)
    y_rwkv = _rwkv(r, lw, k2, v, kk, b, g, bv, _row(lnx_w), _row(lnx_b), tt=RWKV_TT)
    return r, y_nsa, y_rwkv


def kernel(x, p, g_mix_pre, g_mix_post, g_mlp_pre, g_mlp_post, w_in, nsa_gate_bias, cmp_pe_k, cmp_k_w1, cmp_k_b1, cmp_k_w2, cmp_pe_v, cmp_v_w1, cmp_v_b1, cmp_v_w2, shift_mu, w0, w_lora_up, a0, a_lora_up, g_lora_up, k_k, k_a, r_k, lnx_w, lnx_b, w_out, w_up, w_down, w_ple, w_ple_gate):
    D = x.shape[-1]
    for i in range(p.shape[0]):
        _, y_nsa, y_rwkv = _mixers(
            x, g_mix_pre[i], w_in[i], nsa_gate_bias[i],
            (cmp_pe_k[i], cmp_k_w1[i], cmp_k_b1[i], cmp_k_w2[i]), (cmp_pe_v[i], cmp_v_w1[i], cmp_v_b1[i], cmp_v_w2[i]),
            shift_mu[i], w0[i], w_lora_up[i], a0[i], a_lora_up[i], g_lora_up[i], k_k[i], k_a[i], r_k[i],
            lnx_w[i], lnx_b[i])
        wo_nsa = w_out[i][:512].reshape(NSA_KV_HEADS, NSA_GROUP, HEAD_DIM, D).transpose(1, 0, 2, 3).reshape(512, D)
        wo = jnp.concatenate([wo_nsa, w_out[i][512:]], axis=0).astype(bf16)
        weights = (wo, _row(g_mix_post[i]), _row(g_mlp_pre[i]), _row(g_mlp_post[i]), w_up[i].astype(bf16),
                   w_down[i].astype(bf16), w_ple_gate[i].astype(bf16), w_ple[i].astype(bf16))
        x = _post(x, y_nsa, y_rwkv, p[i], weights, tm=POST_TM)
    return x
```

```python
import functools

import jax
import jax.numpy as jnp
import numpy as np
from jax import lax
from jax.experimental import pallas as pl
from jax.experimental.pallas import tpu as pltpu

f32 = jnp.float32
bf16 = jnp.bfloat16

D_MODEL = 1024
HEAD_DIM = 64
NSA_HEADS = 8
NSA_KV_HEADS = 2
NSA_GROUP = NSA_HEADS // NSA_KV_HEADS
CMP_LEN = 32
CMP_STRIDE = 16
CMP_HIDDEN = 2 * HEAD_DIM
SEL_BLOCK = 64
SEL_TOPK = 16
WINDOW = 512
RWKV_WIDTH = 512
RWKV_COLS = 1792
DECAY_LORA = 64
ICLR_LORA = 64
D_FF = 4 * D_MODEL
PLE_DIM = 256
NORM_EPS = 1e-6
GN_EPS = 64e-5
NEG_INF = -1e30
FORCE_SCORE = 1e4

PAIR = 2 * HEAD_DIM
CHUNK = 64
SCAN_GROUP = 4

INPROJ_TM = 512
NSA_TQ = 256
NSA_TK = 512
RWKV_TT = CHUNK * SCAN_GROUP
POST_TM = 512
VMEM_LIMIT = 56 * 1024 * 1024

_Q0, _KV0, _GATE0, _RW0, _WCOLS = 0, 512, 1280, 1408, 3200


def _dot(a, b):
    return jnp.dot(a.astype(bf16), b.astype(bf16), preferred_element_type=f32)


def _dot_nt(a, b):
    return lax.dot_general(a.astype(bf16), b.astype(bf16), (((1,), (1,)), ((), ())), preferred_element_type=f32)


def _dot_tn(a, b):
    return lax.dot_general(a.astype(bf16), b.astype(bf16), (((0,), (0,)), ((), ())), preferred_element_type=f32)


def _split3_dot(w, x):
    hi = x.astype(bf16)
    r1 = x - hi.astype(f32)
    mid = r1.astype(bf16)
    lo = (r1 - mid.astype(f32)).astype(bf16)
    return (jnp.dot(w, hi, preferred_element_type=f32) + jnp.dot(w, mid, preferred_element_type=f32)
            + jnp.dot(w, lo, preferred_element_type=f32))


def _rms(x, g):
    ms = jnp.mean(x * x, axis=-1, keepdims=True)
    return x * lax.rsqrt(ms + NORM_EPS) * g


def _sigmoid(x):
    return 1.0 / (1.0 + jnp.exp(-x))


def _inproj_body(x_ref, g_ref, w_ref, gb_ref, mu_ref, w0_ref, wl_ref, a0_ref, al_ref, gl_ref, kkw_ref, ka_ref,
                 rk_ref, bd_ref, q_ref, kc_ref, vc_ref, ks_ref, vs_ref, kw_ref, vw_ref, gate_ref,
                 r_ref, lw_ref, k_ref, v_ref, kk_ref, b_ref, g_out_ref, bv_ref, zlast_scr):
    h = _rms(x_ref[0], g_ref[...]).astype(bf16)
    z = jnp.dot(h, w_ref[:, _RW0:_WCOLS], preferred_element_type=f32)
    tm = z.shape[0]
    prev_row = jnp.where(pl.program_id(1) > 0, zlast_scr[7:8, :], 0.0)
    zlast_scr[...] = z[tm - 8:tm, :]
    q = jnp.dot(h, w_ref[:, _Q0:_KV0], preferred_element_type=f32) * (HEAD_DIM ** -0.5)
    first = lax.broadcasted_iota(jnp.int32, (1, PAIR), 1) < HEAD_DIM
    for hd in range(NSA_HEADS):
        two = q[:, (hd // 2) * PAIR:(hd // 2 + 1) * PAIR]
        if (hd % 2) != (hd // NSA_GROUP):
            two = pltpu.roll(two, HEAD_DIM, axis=1)
        keep = first if hd < NSA_GROUP else jnp.logical_not(first)
        q_ref[0, hd] = jnp.where(keep, two, 0.0).astype(bf16)
    kv = jnp.dot(h, w_ref[:, _KV0:_GATE0], preferred_element_type=f32)
    kc_ref[0] = kv[:, 0:128]
    vc_ref[0] = kv[:, 128:256]
    ks_ref[0] = kv[:, 256:384].astype(bf16)
    vs_ref[0] = kv[:, 384:512].astype(bf16)
    kw_ref[0] = kv[:, 512:640].astype(bf16)
    vw_ref[0] = kv[:, 640:768].astype(bf16)
    gl = jnp.dot(h, w_ref[:, _GATE0:_RW0], preferred_element_type=f32)
    gate_ref[0] = _sigmoid(gl + gb_ref[...])

    row = lax.broadcasted_iota(jnp.int32, (tm, 1), 0)
    z_prev = jnp.where(row == 0, prev_row, pltpu.roll(z, 1, axis=0))
    zs = z + (z_prev - z) * mu_ref[...]
    r = zs[:, 0:512]
    k = zs[:, 512:1024]
    v = zs[:, 1024:1536]
    lora = zs[:, 1536:1664]
    gd = zs[:, 1664:1792]
    wlog = w0_ref[...] + _dot(jnp.tanh(lora), wl_ref[...])
    sp = jnp.maximum(-wlog, 0.0) + jnp.log(1.0 + jnp.exp(-jnp.abs(wlog)))
    a = _sigmoid(a0_ref[...] + _dot(lora, al_ref[...]))
    kk = k * kkw_ref[...]
    kk = kk * lax.rsqrt(jnp.maximum(_dot(kk * kk, bd_ref[...]), 1e-24))
    k2 = k * (1.0 + (a - 1.0) * ka_ref[...])
    r_ref[0] = r
    lw_ref[0] = -jnp.exp(-sp - 0.5)
    k_ref[0] = k2
    v_ref[0] = v
    kk_ref[0] = kk
    b_ref[0] = kk * a
    g_out_ref[0] = _dot(_sigmoid(gd), gl_ref[...])
    bv_ref[0] = _dot(r * k2 * rk_ref[...], bd_ref[...]) * v


def _inproj(x, g, wcat, gbias, rwkv_params, tm):
    B, S, D = x.shape
    tok = lambda w: pl.BlockSpec((1, tm, w), lambda b, i: (b, i, 0))
    const = lambda a: pl.BlockSpec(a.shape, lambda b, i: (0,) * a.ndim)
    consts = (g, wcat, gbias) + tuple(rwkv_params)
    return pl.pallas_call(
        _inproj_body,
        grid=(B, S // tm),
        in_specs=[tok(D)] + [const(a) for a in consts],
        out_specs=[pl.BlockSpec((1, NSA_HEADS, tm, PAIR), lambda b, i: (b, 0, i, 0))]
        + [tok(128)] * 7 + [tok(RWKV_WIDTH)] * 8,
        out_shape=[jax.ShapeDtypeStruct((B, NSA_HEADS, S, PAIR), bf16),
                   jax.ShapeDtypeStruct((B, S, 128), f32), jax.ShapeDtypeStruct((B, S, 128), f32)]
        + [jax.ShapeDtypeStruct((B, S, 128), bf16)] * 4
        + [jax.ShapeDtypeStruct((B, S, 128), f32)] + [jax.ShapeDtypeStruct((B, S, RWKV_WIDTH), f32)] * 8,
        scratch_shapes=[pltpu.VMEM((8, RWKV_COLS), f32)],
        compiler_params=pltpu.CompilerParams(
            dimension_semantics=("arbitrary", "arbitrary"), vmem_limit_bytes=VMEM_LIMIT),
        name="inproj",
    )(x, *consts)


def _gelu_tanh(x):
    return x * (0.5 * (1.0 + jnp.tanh(np.sqrt(2.0 / np.pi) * (x + 0.044715 * (x * x * x)))))


def _compress_one(x_ref, pe_ref, w_ref, b1_ref, w2_ref):
    n = x_ref.shape[1] // CMP_STRIDE
    x = jnp.concatenate([x_ref[0, pl.ds(j, n, stride=CMP_STRIDE), :] for j in range(CMP_STRIDE)], axis=1)
    lo = _dot(x + pe_ref[0:1, :], w_ref[0])
    hi = _dot(x + pe_ref[1:2, :], w_ref[1])
    pre = lo + pltpu.roll(hi, n - 1, axis=0) + b1_ref[...]
    return _dot(_gelu_tanh(pre), w2_ref[...])


def _compress_body(xk_ref, xv_ref, pek_ref, wk_ref, bk_ref, w2k_ref, pev_ref, wv_ref, bv_ref, w2v_ref,
                   kc_ref, vc_ref):
    kc_ref[0] = _compress_one(xk_ref, pek_ref, wk_ref, bk_ref, w2k_ref).astype(bf16)
    vc_ref[0] = _compress_one(xv_ref, pev_ref, wv_ref, bv_ref, w2v_ref).astype(bf16)


def _compress(xk, xv, kparams, vparams):
    B, S, W = xk.shape
    NC = S // CMP_STRIDE
    const = lambda a: pl.BlockSpec(a.shape, lambda b: (0,) * a.ndim)
    seq = pl.BlockSpec((1, S, W), lambda b: (b, 0, 0))
    out = pl.BlockSpec((1, NC, PAIR), lambda b: (b, 0, 0))
    return pl.pallas_call(
        _compress_body,
        grid=(B,),
        in_specs=[seq, seq] + [const(a) for a in kparams] + [const(a) for a in vparams],
        out_specs=[out, out],
        out_shape=[jax.ShapeDtypeStruct((B, NC, PAIR), bf16)] * 2,
        compiler_params=pltpu.CompilerParams(dimension_semantics=("arbitrary",), vmem_limit_bytes=VMEM_LIMIT),
        name="compress",
    )(xk, xv, *kparams, *vparams)


def _alibi_key_columns(pos, ns, onehot):
    a = np.zeros((pos.shape[0], PAIR), np.float32)
    if onehot:
        a[np.arange(pos.shape[0]), pos // SEL_BLOCK] = 1.0
    a[:, ns] = -1.0
    a[:, ns + 1] = -1.0
    a[:, ns + 2] = pos // SEL_BLOCK
    a[:, ns + 3] = pos % SEL_BLOCK
    return jnp.asarray(a, dtype=bf16)


def _nsa_body(q_ref, kc_ref, vc_ref, ks_ref, vs_ref, kw_ref, vw_ref, gate_ref, selT_ref, auxk_ref, auxc_ref,
              o_ref, m_scr, acc_scr, sa_scr, sb_scr, rank_scr, *, tq, tk, seq):
    R, G, H = NSA_GROUP, NSA_KV_HEADS, NSA_HEADS
    t0 = pl.program_id(1) * tq
    nc = seq // CMP_STRIDE
    ns = seq // SEL_BLOCK
    gm = R * tq
    tok = (t0 + lax.broadcasted_iota(jnp.int32, (tq, 1), 0)).astype(f32)
    lane = lax.broadcasted_iota(jnp.int32, (1, PAIR), 1)
    first = lane < HEAD_DIM

    tl = t0 + lax.broadcasted_iota(jnp.int32, (8, tq), 1)
    rid = lax.broadcasted_iota(jnp.int32, (8, tq), 0)
    alibi_rows = jnp.where(rid == 0, ((tl // SEL_BLOCK) * SEL_BLOCK).astype(f32),
                           jnp.where(rid == 1, (tl % SEL_BLOCK).astype(f32),
                                     jnp.where(rid == 2, float(SEL_BLOCK), jnp.where(rid == 3, 1.0, 0.0))))
    aux0 = jnp.concatenate([jnp.zeros((ns, tq), f32), alibi_rows, jnp.zeros((PAIR - ns - 8, tq), f32)], axis=0).T

    def augment(q8, aux_by_group):
        aux8 = jnp.concatenate([aux_by_group[h // R] * jnp.where(lane < ns, 1.0, 2.0 ** (-(h + 1)))
                                for h in range(H)], axis=0)
        return jnp.concatenate([q8, aux8.astype(bf16)], axis=1)

    def per_head(a):
        return a.reshape(H, tq, a.shape[-1])

    def with_ones(v):
        one = jnp.ones((), v.dtype)
        return jnp.where(first, v, one), jnp.where(first, one, v)

    def weighted_values(p, v):
        v0, v1 = with_ones(v)
        return jnp.concatenate([jnp.dot(p[:gm], v0, preferred_element_type=f32),
                                jnp.dot(p[gm:], v1, preferred_element_type=f32)], axis=0)

    def split_sum(acc):
        top, bot = acc[:gm], acc[gm:]
        return jnp.concatenate([top / top[:, HEAD_DIM:HEAD_DIM + 1], bot / bot[:, 0:1]], axis=0)

    q8 = q_ref[0].reshape(H * tq, PAIR)
    qa = augment(q8, [aux0, aux0])

    kc_aug = jnp.concatenate([kc_ref[0], auxc_ref[...]], axis=1)
    sc = per_head(_dot_nt(qa, kc_aug))
    w0 = pl.multiple_of(jnp.maximum(t0 - WINDOW, 0), tq)
    wrows = pl.ds(w0, WINDOW + tq)
    kw_aug = jnp.concatenate([kw_ref[0, wrows, :], auxk_ref[wrows, :]], axis=1)
    sw = per_head(_dot_nt(qa, kw_aug))

    cend = (lax.broadcasted_iota(jnp.int32, (1, nc), 1) * CMP_STRIDE + (CMP_LEN - 1)).astype(f32)
    vis_c = tok >= cend
    any_c = (tok >= float(CMP_LEN - 1)).astype(f32)
    s = jnp.where(vis_c[None], sc, NEG_INF)
    e = jnp.exp(s - jnp.max(s, axis=-1, keepdims=True))
    p_cmp = e * (any_c[None] / jnp.sum(e, axis=-1, keepdims=True))
    o_cmp = _dot(p_cmp.reshape(H * tq, nc), vc_ref[0])

    jrow = lax.broadcasted_iota(jnp.int32, (ns, tq), 0)
    cur = (t0 + lax.broadcasted_iota(jnp.int32, (ns, tq), 1)) // SEL_BLOCK
    forced = (jrow == 0) | (jrow == cur) | (jrow == cur - 1)
    keys_t = []
    for g in range(G):
        psum = p_cmp[R * g]
        for r in range(1, R):
            psum = psum + p_cmp[R * g + r]
        imp_t = lax.dot_general(selT_ref[...], psum.astype(bf16), (((1,), (1,)), ((), ())),
                                preferred_element_type=f32)
        p_lo = (psum - psum.astype(bf16).astype(f32)).astype(bf16)
        imp_t = imp_t + lax.dot_general(selT_ref[...], p_lo, (((1,), (1,)), ((), ())),
                                        preferred_element_type=f32)
        score = jnp.where(forced, FORCE_SCORE, jnp.where(jrow <= cur, imp_t, -1.0))
        keys_t.append(jnp.where(score < 0.0, -1, pltpu.bitcast(score, jnp.int32)))
        rank_scr[g] = keys_t[g]

    def rank_rows(i8, cnts):
        base = pl.multiple_of(i8 * 8, 8)
        out = []
        for g in range(G):
            rows8 = rank_scr[g, pl.ds(base, 8), :]
            cnt = cnts[g]
            for k in range(8):
                si = rows8[k:k + 1, :]
                ahead = jnp.where(jrow > base + k, si + 1, si) > keys_t[g]
                cnt = cnt + jnp.where(ahead, 1.0, 0.0)
            out.append(cnt)
        return tuple(out)

    n_rows8 = jnp.minimum((t0 + tq - 1) // SEL_BLOCK // 8 + 1, ns // 8)
    cnts = lax.fori_loop(0, n_rows8, rank_rows, tuple(jnp.zeros((ns, tq), f32) for _ in range(G)))
    aux_sel = []
    for g in range(G):
        bias_t = jnp.where(cnts[g] < float(SEL_TOPK), 0.0, NEG_INF)
        aux_sel.append(aux0 + jnp.concatenate([bias_t, jnp.zeros((PAIR - ns, tq), f32)], axis=0).T)
    qs = augment(q8, aux_sel)

    m_scr[...] = jnp.full(m_scr.shape, NEG_INF, f32)
    acc_scr[...] = jnp.zeros(acc_scr.shape, f32)

    def key_rows(kt):
        return pl.ds(pl.multiple_of(kt * tk, tk), tk)

    def scores(kt, dst):
        rows_ = key_rows(kt)
        dst[...] = _dot_nt(qs, jnp.concatenate([ks_ref[0, rows_, :], auxk_ref[rows_, :]], axis=1))

    def attend(src, kt, diag):
        s8 = src[...]
        if diag:
            kpos = (kt * tk + lax.broadcasted_iota(jnp.int32, (1, tk), 1)).astype(f32)
            s8 = jnp.where((tok >= kpos)[None], per_head(s8), NEG_INF).reshape(H * tq, tk)
        m_prev = m_scr[...][:, 0:1]
        m_new = jnp.maximum(m_prev, jnp.max(s8, axis=-1, keepdims=True))
        p = jnp.exp(s8 - m_new).astype(bf16)
        acc_scr[...] = jnp.exp(m_prev - m_new) * acc_scr[...] + weighted_values(p, vs_ref[0, key_rows(kt), :])
        m_scr[...] = jnp.broadcast_to(m_new, m_scr.shape)

    n_full = t0 // tk
    scores(n_full, sa_scr)

    kpos = w0 + lax.broadcasted_iota(jnp.int32, (1, WINDOW + tq), 1)
    dist_w = tok - kpos.astype(f32)
    vis_w = jnp.abs(dist_w - (WINDOW - 1) / 2.0) < WINDOW / 2.0
    s = jnp.where(vis_w[None], sw, NEG_INF)
    e = jnp.exp(s - jnp.max(s, axis=-1, keepdims=True)).astype(bf16).reshape(H * tq, WINDOW + tq)
    acc_win = weighted_values(e, vw_ref[0, wrows, :])

    scores(0, sb_scr)
    attend(sa_scr, n_full, True)

    def two_tiles(j, carry):
        scores(2 * j + 1, sa_scr)
        attend(sb_scr, 2 * j, False)
        scores(2 * j + 2, sb_scr)
        attend(sa_scr, 2 * j + 1, False)
        return carry

    lax.fori_loop(0, n_full // 2, two_tiles, 0)

    @pl.when(n_full % 2 == 1)
    def _():
        attend(sb_scr, n_full - 1, False)

    o_sel = split_sum(acc_scr[...])
    o_win = split_sum(acc_win)

    gate = gate_ref[0]
    o_cmp, o_sel, o_win = per_head(o_cmp), per_head(o_sel), per_head(o_win)
    for r in range(R):
        pair = []
        for g in range(G):
            h = R * g + r
            pair.append(gate[:, 3 * h:3 * h + 1] * o_cmp[h] + gate[:, 3 * h + 1:3 * h + 2] * o_sel[h]
                        + gate[:, 3 * h + 2:3 * h + 3] * o_win[h])
        o_ref[0, :, r * PAIR:(r + 1) * PAIR] = jnp.where(first, pair[0], pair[1])


def _nsa(q, kc, vc, ks, vs, kw, vw, gates, sel_t, tq, tk):
    B, H, S, _ = q.shape
    nc, ns = S // CMP_STRIDE, S // SEL_BLOCK
    aux_k = _alibi_key_columns(np.arange(S), ns, onehot=True)
    aux_c = _alibi_key_columns(np.arange(nc) * CMP_STRIDE + (CMP_LEN - 1), ns, onehot=False)
    full = lambda n: pl.BlockSpec((1, n, PAIR), lambda b, i: (b, 0, 0))
    const = lambda a: pl.BlockSpec(a.shape, lambda b, i: (0, 0))
    body = functools.partial(_nsa_body, tq=tq, tk=tk, seq=S)
    return pl.pallas_call(
        body,
        grid=(B, S // tq),
        in_specs=[pl.BlockSpec((1, H, tq, PAIR), lambda b, i: (b, 0, i, 0)),
                  full(nc), full(nc), full(S), full(S), full(S), full(S),
                  pl.BlockSpec((1, tq, 128), lambda b, i: (b, i, 0)),
                  const(sel_t), const(aux_k), const(aux_c)],
        out_specs=pl.BlockSpec((1, tq, NSA_GROUP * PAIR), lambda b, i: (b, i, 0)),
        out_shape=jax.ShapeDtypeStruct((B, S, NSA_GROUP * PAIR), f32),
        scratch_shapes=[pltpu.VMEM((H * tq, PAIR), f32)] * 2 + [pltpu.VMEM((H * tq, tk), f32)] * 2
        + [pltpu.VMEM((NSA_KV_HEADS, ns, tq), jnp.int32)],
        compiler_params=pltpu.CompilerParams(
            dimension_semantics=("arbitrary", "arbitrary"), vmem_limit_bytes=VMEM_LIMIT),
        name="nsa",
    )(q, kc, vc, ks, vs, kw, vw, gates, sel_t, aux_k, aux_c)


def _rwkv_body(r_s, lw_s, k_s, v_s, kk_s, b_s, g_s, bv_s, lnw_ref, lnb_ref, y_ref, s_scr, *, tt):
    C = CHUNK
    n_pairs = RWKV_WIDTH // PAIR

    @pl.when(pl.program_id(1) == 0)
    def _():
        s_scr[...] = jnp.zeros(s_scr.shape, f32)

    ri = lax.broadcasted_iota(jnp.int32, (C, C), 0)
    ci = lax.broadcasted_iota(jnp.int32, (C, C), 1)
    cum_mat = jnp.where(ri >= ci, 1.0, 0.0).astype(bf16)
    ri2 = lax.broadcasted_iota(jnp.int32, (C, 2 * C), 0)
    ci2 = lax.broadcasted_iota(jnp.int32, (C, 2 * C), 1) % C
    tri2_incl = ri2 >= ci2
    tri2_strict = ri2 > ci2
    lane = lax.broadcasted_iota(jnp.int32, (1, PAIR), 1)
    first = lane < HEAD_DIM
    blockdiag = (lax.broadcasted_iota(jnp.int32, (PAIR, PAIR), 0) // HEAD_DIM
                 == lax.broadcasted_iota(jnp.int32, (PAIR, PAIR), 1) // HEAD_DIM)
    head_mean = jnp.where(blockdiag, 1.0, 0.0).astype(bf16)

    def swap_heads(x):
        return pltpu.roll(x, HEAD_DIM, axis=1)

    def by_head_rows(x):
        return jnp.concatenate([jnp.where(first, x, 0.0), jnp.where(first, 0.0, x)], axis=0)

    def block_diag(x2):
        zero = jnp.zeros((x2.shape[0], PAIR), x2.dtype)
        return jnp.concatenate([jnp.concatenate([x2[:, :PAIR], zero], axis=1),
                                jnp.concatenate([zero, x2[:, PAIR:]], axis=1)], axis=0)

    def group(gi):
        units = []
        for cc in range(SCAN_GROUP):
            rows = pl.ds((gi * SCAN_GROUP + cc) * C, C)
            r_, lw_, k_, v_, kk_, b_ = (s[0, rows, :] for s in (r_s, lw_s, k_s, v_s, kk_s, b_s))
            cum = _split3_dot(cum_mat, lw_)
            cum_end = cum[C - 1:C, :]
            e_neg = jnp.exp(-cum)
            e_end = jnp.exp(cum_end - cum)
            a_t = -kk_ * jnp.exp(cum - lw_)
            r_t = r_ * jnp.exp(cum)
            b_t = b_ * e_neg
            k_t = k_ * e_neg
            b_h = b_ * e_end
            k_h = k_ * e_end
            w_end = jnp.exp(cum_end)
            for p in range(n_pairs):
                pc = slice(p * PAIR, (p + 1) * PAIR)
                units.append(dict(p=p, rows=rows, pc=pc, a_t=a_t[:, pc],
                                  r_t=r_t[:, pc], b_t=b_t[:, pc], k_t=k_t[:, pc], b_h=b_h[:, pc], k_h=k_h[:, pc],
                                  v=v_[:, pc], w_end=w_end[:, pc]))

        for u in units:
            lhs = jnp.concatenate([u["a_t"], u["r_t"]], axis=0)
            u["ab"] = _dot_nt(lhs, by_head_rows(u["b_t"]))
            u["ak"] = _dot_nt(lhs, by_head_rows(u["k_t"]))
        for u in units:
            ab, ak = u["ab"], u["ak"]
            u["a"] = jnp.where(tri2_strict, ab[:C], 0.0)
            u["a_rb"] = jnp.where(tri2_incl, ab[C:], 0.0)
            u["akrk"] = jnp.concatenate([jnp.where(tri2_strict, ak[:C], 0.0), jnp.where(tri2_incl, ak[C:], 0.0)],
                                        axis=0)
        for u in units:
            v_sw = swap_heads(u["v"]).astype(bf16)
            x1 = _dot(u["akrk"], block_diag(jnp.concatenate([v_sw, v_sw], axis=1)))
            u["x"] = jnp.concatenate([jnp.where(first, u["a_t"], x1[:C, :PAIR]),
                                      jnp.where(first, x1[:C, PAIR:], u["a_t"])], axis=1)
            u["arkv"] = x1[C:]
            u["pows"] = [u["a"], _dot(u["a"], by_head_rows(u["a"]))]
        for j in range(6):
            for u in units:
                u["x"] = u["x"] + _dot(u["pows"][j], block_diag(u["x"].astype(bf16)))
            if j + 2 <= 5:
                for u in units:
                    u["pows"].append(_dot(u["pows"][j + 1], by_head_rows(u["pows"][j + 1])))
        for u in units:
            u["ex"] = _dot(u["a_rb"], block_diag(u["x"].astype(bf16)))
        for u in units:
            x0, x1 = u["x"][:, :PAIR], u["x"][:, PAIR:]
            ex0, ex1 = u["ex"][:, :PAIR], u["ex"][:, PAIR:]
            ua = jnp.where(first, x0, x1)
            w2 = swap_heads(jnp.where(first, x1, x0))
            u["rq"] = u["r_t"] + jnp.where(first, ex0, ex1)
            u["yin"] = swap_heads(jnp.where(first, u["arkv"][:, PAIR:] + ex1, u["arkv"][:, :PAIR] + ex0))
            u["gmat"] = jnp.where(blockdiag, _dot_tn(ua, u["b_h"]), 0.0)
            u["qmat"] = jnp.where(blockdiag, _dot_tn(jnp.concatenate([w2, u["v"]], axis=0),
                                                     jnp.concatenate([u["b_h"], u["k_h"]], axis=0)), 0.0)
        state = [s_scr[p] for p in range(n_pairs)]
        for u in units:
            s0 = state[u["p"]]
            u["y"] = _dot_nt(u["rq"], s0) + u["yin"]
            state[u["p"]] = s0 * u["w_end"] + _dot(s0, u["gmat"]) + u["qmat"]
        for p in range(n_pairs):
            s_scr[p] = state[p]
        for u in units:
            y = u["y"]
            mu = jnp.dot(y.astype(bf16), head_mean, preferred_element_type=f32) * (1.0 / HEAD_DIM)
            u["d"] = y - mu
        for u in units:
            d = u["d"]
            var = jnp.dot((d * d).astype(bf16), head_mean, preferred_element_type=f32) * (1.0 / HEAD_DIM)
            yn = d * lax.rsqrt(var + GN_EPS) * lnw_ref[:, u["pc"]] + lnb_ref[:, u["pc"]]
            y_ref[0, u["rows"], u["pc"]] = (yn + bv_s[0, u["rows"], u["pc"]]) * g_s[0, u["rows"], u["pc"]]

    for gi in range(tt // (C * SCAN_GROUP)):
        group(gi)


def _rwkv(r, lw, k, v, kk, b, g, bv, lnw, lnb, tt):
    B, S, W = r.shape
    tok = pl.BlockSpec((1, tt, W), lambda bb, i: (bb, i, 0))
    const = pl.BlockSpec((1, W), lambda bb, i: (0, 0))
    return pl.pallas_call(
        functools.partial(_rwkv_body, tt=tt),
        grid=(B, S // tt),
        in_specs=[tok] * 8 + [const, const],
        out_specs=tok,
        out_shape=jax.ShapeDtypeStruct((B, S, W), f32),
        scratch_shapes=[pltpu.VMEM((W // PAIR, PAIR, PAIR), f32)],
        compiler_params=pltpu.CompilerParams(
            dimension_semantics=("arbitrary", "arbitrary"), vmem_limit_bytes=VMEM_LIMIT),
        name="rwkv",
    )(r, lw, k, v, kk, b, g, bv, lnw, lnb)


def _post_body(x_ref, yn_ref, yr_ref, p_ref, wo_ref, gpost_ref, gpre_ref, gmlp_ref, wup_ref, wdn_ref,
               wpg_ref, wple_ref, o_ref):
    y = jnp.concatenate([yn_ref[0], yr_ref[0]], axis=1).astype(bf16)
    mix = jnp.dot(y, wo_ref[...], preferred_element_type=f32)
    x1 = x_ref[0] + _rms(mix, gpost_ref[...])
    h = _rms(x1, gpre_ref[...]).astype(bf16)
    acc = None
    for c in range(D_FF // D_MODEL):
        cs = slice(c * D_MODEL, (c + 1) * D_MODEL)
        u = jnp.dot(h, wup_ref[:, cs], preferred_element_type=f32)
        u = jnp.square(jnp.maximum(u, 0.0)).astype(bf16)
        part = jnp.dot(u, wdn_ref[cs, :], preferred_element_type=f32)
        acc = part if acc is None else acc + part
    x2 = x1 + _rms(acc, gmlp_ref[...])
    gate = _sigmoid(jnp.dot(x2.astype(bf16), wpg_ref[...], preferred_element_type=f32))
    o_ref[0] = x2 + gate * jnp.dot(p_ref[0].astype(bf16), wple_ref[...], preferred_element_type=f32)


def _post(x, yn, yr, p, weights, tm):
    B, S, D = x.shape
    tok = lambda w: pl.BlockSpec((1, tm, w), lambda b, i: (b, i, 0))
    const = lambda a: pl.BlockSpec(a.shape, lambda b, i: (0,) * a.ndim, pipeline_mode=pl.Buffered(1))
    return pl.pallas_call(
        _post_body,
        grid=(B, S // tm),
        in_specs=[tok(D), tok(512), tok(512), tok(PLE_DIM)] + [const(a) for a in weights],
        out_specs=tok(D),
        out_shape=jax.ShapeDtypeStruct((B, S, D), f32),
        compiler_params=pltpu.CompilerParams(
            dimension_semantics=("arbitrary", "arbitrary"), vmem_limit_bytes=VMEM_LIMIT),
        name="post",
    )(x, yn, yr, p, *weights)


def _pack_inproj(w_in, gate_bias):
    wg = jnp.pad(w_in[:, 1280:1304], ((0, 0), (0, 128 - 24)))
    wcat = jnp.concatenate([w_in[:, :1280], wg, w_in[:, 1304:]], axis=1).astype(bf16)
    return wcat, jnp.pad(gate_bias, (0, 128 - 24)).reshape(1, 128)


def _pack_compress(pe, w1, b1, w2):
    eye2 = jnp.eye(NSA_KV_HEADS, dtype=f32)
    w1r = w1.reshape(CMP_LEN, HEAD_DIM, CMP_HIDDEN)
    halves = []
    for part in (w1r[:CMP_STRIDE], w1r[CMP_STRIDE:]):
        halves.append(jnp.einsum("jdc,gh->jgdhc", part, eye2).reshape(CMP_STRIDE * PAIR, 2 * CMP_HIDDEN))
    w = jnp.stack(halves).astype(bf16)
    per = jnp.broadcast_to(pe.reshape(2, CMP_STRIDE, 1, HEAD_DIM), (2, CMP_STRIDE, NSA_KV_HEADS, HEAD_DIM))
    per = per.reshape(2, CMP_STRIDE * PAIR)
    b1p = jnp.tile(b1, NSA_KV_HEADS).reshape(1, 2 * CMP_HIDDEN)
    w2p = jnp.einsum("cd,gh->gchd", w2, eye2).reshape(2 * CMP_HIDDEN, PAIR).astype(bf16)
    return per, w, b1p, w2p


def _sel_map_t(seq):
    nc, ns = seq // CMP_STRIDE, seq // SEL_BLOCK
    c0 = np.arange(nc) * CMP_STRIDE
    s0 = np.arange(ns) * SEL_BLOCK
    ov = (np.minimum(c0[:, None] + CMP_LEN - 1, s0[None, :] + SEL_BLOCK - 1)
          - np.maximum(c0[:, None], s0[None, :]) + 1)
    m = np.clip(ov, 0, None).astype(np.float32) / CMP_STRIDE
    m[nc - 1] = 0.0
    return jnp.asarray(m.T, dtype=bf16)


def _row(a):
    return a.reshape(1, -1)


def _mixers(x, g_mix_pre, w_in, nsa_gate_bias, cmp_k, cmp_v, shift_mu, w0, w_lora_up, a0, a_lora_up, g_lora_up,
            k_k, k_a, r_k, lnx_w, lnx_b):
    S = x.shape[1]
    bd = jnp.asarray(np.kron(np.eye(RWKV_WIDTH // HEAD_DIM), np.ones((HEAD_DIM, HEAD_DIM))), dtype=bf16)
    wcat, gbias = _pack_inproj(w_in, nsa_gate_bias)
    wl = jnp.concatenate([w_lora_up, jnp.zeros((ICLR_LORA, RWKV_WIDTH), f32)], axis=0).astype(bf16)
    al = jnp.concatenate([jnp.zeros((DECAY_LORA, RWKV_WIDTH), f32), a_lora_up], axis=0).astype(bf16)
    rwkv_params = (_row(shift_mu), _row(w0), wl, _row(a0), al, g_lora_up.astype(bf16),
                   _row(k_k), _row(k_a), _row(r_k), bd)
    q, kc, vc, ks, vs, kw, vw, gates, r, lw, k2, v, kk, b, g, bv = _inproj(
        x, _row(g_mix_pre), wcat, gbias, rwkv_params, tm=INPROJ_TM)
    kcmp, vcmp = _compress(kc, vc, _pack_compress(*cmp_k), _pack_compress(*cmp_v))
    y_nsa = _nsa(q, kcmp, vcmp, ks, vs, kw, vw, gates, _sel_map_t(S), tq=NSA_TQ, tk=NSA_TK)
    y_rwkv = _rwkv(r, lw, k2, v, kk, b, g, bv, _row(lnx_w), _row(lnx_b), tt=RWKV_TT)
    return r, y_nsa, y_rwkv


def kernel(x, p, g_mix_pre, g_mix_post, g_mlp_pre, g_mlp_post, w_in, nsa_gate_bias, cmp_pe_k, cmp_k_w1, cmp_k_b1, cmp_k_w2, cmp_pe_v, cmp_v_w1, cmp_v_b1, cmp_v_w2, shift_mu, w0, w_lora_up, a0, a_lora_up, g_lora_up, k_k, k_a, r_k, lnx_w, lnx_b, w_out, w_up, w_down, w_ple, w_ple_gate):
    D = x.shape[-1]
    for i in range(p.shape[0]):
        _, y_nsa, y_rwkv = _mixers(
            x, g_mix_pre[i], w_in[i], nsa_gate_bias[i],
            (cmp_pe_k[i], cmp_k_w1[i], cmp_k_b1[i], cmp_k_w2[i]), (cmp_pe_v[i], cmp_v_w1[i], cmp_v_b1[i], cmp_v_w2[i]),
            shift_mu[i], w0[i], w_lora_up[i], a0[i], a_lora_up[i], g_lora_up[i], k_k[i], k_a[i], r_k[i],
            lnx_w[i], lnx_b[i])
        wo_nsa = w_out[i][:512].reshape(NSA_KV_HEADS, NSA_GROUP, HEAD_DIM, D).transpose(1, 0, 2, 3).reshape(512, D)
        wo = jnp.concatenate([wo_nsa, w_out[i][512:]], axis=0).astype(bf16)
        weights = (wo, _row(g_mix_post[i]), _row(g_mlp_pre[i]), _row(g_mlp_post[i]), w_up[i].astype(bf16),
                   w_down[i].astype(bf16), w_ple_gate[i].astype(bf16), w_ple[i].astype(bf16))
        x = _post(x, y_nsa, y_rwkv, p[i], weights, tm=POST_TM)
    return x
```

```python
import functools

import jax
import jax.numpy as jnp
import numpy as np
from jax import lax
from jax.experimental import pallas as pl
from jax.experimental.pallas import tpu as pltpu

f32 = jnp.float32
bf16 = jnp.bfloat16

D_MODEL = 1024
HEAD_DIM = 64
NSA_HEADS = 8
NSA_KV_HEADS = 2
NSA_GROUP = NSA_HEADS // NSA_KV_HEADS
CMP_LEN = 32
CMP_STRIDE = 16
CMP_HIDDEN = 2 * HEAD_DIM
SEL_BLOCK = 64
SEL_TOPK = 16
WINDOW = 512
RWKV_WIDTH = 512
RWKV_COLS = 1792
DECAY_LORA = 64
ICLR_LORA = 64
D_FF = 4 * D_MODEL
PLE_DIM = 256
NORM_EPS = 1e-6
GN_EPS = 64e-5
NEG_INF = -1e30
FORCE_SCORE = 1e4

PAIR = 2 * HEAD_DIM
CHUNK = 64
SCAN_GROUP = 4

INPROJ_TM = 512
NSA_TQ = 256
NSA_TK = 512
RWKV_TT = CHUNK * SCAN_GROUP
POST_TM = 512
VMEM_LIMIT = 56 * 1024 * 1024

_Q0, _KV0, _GATE0, _RW0, _WCOLS = 0, 512, 1280, 1408, 3200


def _dot(a, b):
    return jnp.dot(a.astype(bf16), b.astype(bf16), preferred_element_type=f32)


def _dot_nt(a, b):
    return lax.dot_general(a.astype(bf16), b.astype(bf16), (((1,), (1,)), ((), ())), preferred_element_type=f32)


def _dot_tn(a, b):
    return lax.dot_general(a.astype(bf16), b.astype(bf16), (((0,), (0,)), ((), ())), preferred_element_type=f32)


def _split3_dot(w, x):
    hi = x.astype(bf16)
    r1 = x - hi.astype(f32)
    mid = r1.astype(bf16)
    lo = (r1 - mid.astype(f32)).astype(bf16)
    return (jnp.dot(w, hi, preferred_element_type=f32) + jnp.dot(w, mid, preferred_element_type=f32)
            + jnp.dot(w, lo, preferred_element_type=f32))


def _rms(x, g):
    ms = jnp.mean(x * x, axis=-1, keepdims=True)
    return x * lax.rsqrt(ms + NORM_EPS) * g


def _sigmoid(x):
    return 1.0 / (1.0 + jnp.exp(-x))


def _inproj_body(x_ref, g_ref, w_ref, gb_ref, mu_ref, w0_ref, wl_ref, a0_ref, al_ref, gl_ref, kkw_ref, ka_ref,
                 rk_ref, bd_ref, q_ref, kc_ref, vc_ref, ks_ref, vs_ref, kw_ref, vw_ref, gate_ref,
                 r_ref, lw_ref, k_ref, v_ref, kk_ref, b_ref, g_out_ref, bv_ref, zlast_scr):
    h = _rms(x_ref[0], g_ref[...]).astype(bf16)
    z = jnp.dot(h, w_ref[:, _RW0:_WCOLS], preferred_element_type=f32)
    tm = z.shape[0]
    prev_row = jnp.where(pl.program_id(1) > 0, zlast_scr[7:8, :], 0.0)
    zlast_scr[...] = z[tm - 8:tm, :]
    q = jnp.dot(h, w_ref[:, _Q0:_KV0], preferred_element_type=f32) * (HEAD_DIM ** -0.5)
    first = lax.broadcasted_iota(jnp.int32, (1, PAIR), 1) < HEAD_DIM
    for hd in range(NSA_HEADS):
        two = q[:, (hd // 2) * PAIR:(hd // 2 + 1) * PAIR]
        if (hd % 2) != (hd // NSA_GROUP):
            two = pltpu.roll(two, HEAD_DIM, axis=1)
        keep = first if hd < NSA_GROUP else jnp.logical_not(first)
        q_ref[0, hd] = jnp.where(keep, two, 0.0).astype(bf16)
    kv = jnp.dot(h, w_ref[:, _KV0:_GATE0], preferred_element_type=f32)
    kc_ref[0] = kv[:, 0:128]
    vc_ref[0] = kv[:, 128:256]
    ks_ref[0] = kv[:, 256:384].astype(bf16)
    vs_ref[0] = kv[:, 384:512].astype(bf16)
    kw_ref[0] = kv[:, 512:640].astype(bf16)
    vw_ref[0] = kv[:, 640:768].astype(bf16)
    gl = jnp.dot(h, w_ref[:, _GATE0:_RW0], preferred_element_type=f32)
    gate_ref[0] = _sigmoid(gl + gb_ref[...])

    row = lax.broadcasted_iota(jnp.int32, (tm, 1), 0)
    z_prev = jnp.where(row == 0, prev_row, pltpu.roll(z, 1, axis=0))
    zs = z + (z_prev - z) * mu_ref[...]
    r = zs[:, 0:512]
    k = zs[:, 512:1024]
    v = zs[:, 1024:1536]
    lora = zs[:, 1536:1664]
    gd = zs[:, 1664:1792]
    wlog = w0_ref[...] + _dot(jnp.tanh(lora), wl_ref[...])
    sp = jnp.maximum(-wlog, 0.0) + jnp.log(1.0 + jnp.exp(-jnp.abs(wlog)))
    a = _sigmoid(a0_ref[...] + _dot(lora, al_ref[...]))
    kk = k * kkw_ref[...]
    kk = kk * lax.rsqrt(jnp.maximum(_dot(kk * kk, bd_ref[...]), 1e-24))
    k2 = k * (1.0 + (a - 1.0) * ka_ref[...])
    r_ref[0] = r
    lw_ref[0] = -jnp.exp(-sp - 0.5)
    k_ref[0] = k2
    v_ref[0] = v
    kk_ref[0] = kk
    b_ref[0] = kk * a
    g_out_ref[0] = _dot(_sigmoid(gd), gl_ref[...])
    bv_ref[0] = _dot(r * k2 * rk_ref[...], bd_ref[...]) * v


def _inproj(x, g, wcat, gbias, rwkv_params, tm):
    B, S, D = x.shape
    tok = lambda w: pl.BlockSpec((1, tm, w), lambda b, i: (b, i, 0))
    const = lambda a: pl.BlockSpec(a.shape, lambda b, i: (0,) * a.ndim)
    consts = (g, wcat, gbias) + tuple(rwkv_params)
    return pl.pallas_call(
        _inproj_body,
        grid=(B, S // tm),
        in_specs=[tok(D)] + [const(a) for a in consts],
        out_specs=[pl.BlockSpec((1, NSA_HEADS, tm, PAIR), lambda b, i: (b, 0, i, 0))]
        + [tok(128)] * 7 + [tok(RWKV_WIDTH)] * 8,
        out_shape=[jax.ShapeDtypeStruct((B, NSA_HEADS, S, PAIR), bf16),
                   jax.ShapeDtypeStruct((B, S, 128), f32), jax.ShapeDtypeStruct((B, S, 128), f32)]
        + [jax.ShapeDtypeStruct((B, S, 128), bf16)] * 4
        + [jax.ShapeDtypeStruct((B, S, 128), f32)] + [jax.ShapeDtypeStruct((B, S, RWKV_WIDTH), f32)] * 8,
        scratch_shapes=[pltpu.VMEM((8, RWKV_COLS), f32)],
        compiler_params=pltpu.CompilerParams(
            dimension_semantics=("arbitrary", "arbitrary"), vmem_limit_bytes=VMEM_LIMIT),
        name="inproj",
    )(x, *consts)


def _gelu_tanh(x):
    return x * (0.5 * (1.0 + jnp.tanh(np.sqrt(2.0 / np.pi) * (x + 0.044715 * (x * x * x)))))


def _compress_one(x_ref, pe_ref, w_ref, b1_ref, w2_ref):
    n = x_ref.shape[1] // CMP_STRIDE
    x = jnp.concatenate([x_ref[0, pl.ds(j, n, stride=CMP_STRIDE), :] for j in range(CMP_STRIDE)], axis=1)
    lo = _dot(x + pe_ref[0:1, :], w_ref[0])
    hi = _dot(x + pe_ref[1:2, :], w_ref[1])
    pre = lo + pltpu.roll(hi, n - 1, axis=0) + b1_ref[...]
    return _dot(_gelu_tanh(pre), w2_ref[...])


def _compress_body(xk_ref, xv_ref, pek_ref, wk_ref, bk_ref, w2k_ref, pev_ref, wv_ref, bv_ref, w2v_ref,
                   kc_ref, vc_ref):
    kc_ref[0] = _compress_one(xk_ref, pek_ref, wk_ref, bk_ref, w2k_ref).astype(bf16)
    vc_ref[0] = _compress_one(xv_ref, pev_ref, wv_ref, bv_ref, w2v_ref).astype(bf16)


def _compress(xk, xv, kparams, vparams):
    B, S, W = xk.shape
    NC = S // CMP_STRIDE
    const = lambda a: pl.BlockSpec(a.shape, lambda b: (0,) * a.ndim)
    seq = pl.BlockSpec((1, S, W), lambda b: (b, 0, 0))
    out = pl.BlockSpec((1, NC, PAIR), lambda b: (b, 0, 0))
    return pl.pallas_call(
        _compress_body,
        grid=(B,),
        in_specs=[seq, seq] + [const(a) for a in kparams] + [const(a) for a in vparams],
        out_specs=[out, out],
        out_shape=[jax.ShapeDtypeStruct((B, NC, PAIR), bf16)] * 2,
        compiler_params=pltpu.CompilerParams(dimension_semantics=("arbitrary",), vmem_limit_bytes=VMEM_LIMIT),
        name="compress",
    )(xk, xv, *kparams, *vparams)


def _alibi_key_columns(pos, ns, onehot):
    a = np.zeros((pos.shape[0], PAIR), np.float32)
    if onehot:
        a[np.arange(pos.shape[0]), pos // SEL_BLOCK] = 1.0
    a[:, ns] = -1.0
    a[:, ns + 1] = -1.0
    a[:, ns + 2] = pos // SEL_BLOCK
    a[:, ns + 3] = pos % SEL_BLOCK
    return jnp.asarray(a, dtype=bf16)


def _nsa_body(q_ref, kc_ref, vc_ref, ks_ref, vs_ref, kw_ref, vw_ref, gate_ref, selT_ref, auxk_ref, auxc_ref,
              o_ref, m_scr, acc_scr, sa_scr, sb_scr, rank_scr, *, tq, tk, seq):
    R, G, H = NSA_GROUP, NSA_KV_HEADS, NSA_HEADS
    t0 = pl.program_id(1) * tq
    nc = seq // CMP_STRIDE
    ns = seq // SEL_BLOCK
    gm = R * tq
    tok = (t0 + lax.broadcasted_iota(jnp.int32, (tq, 1), 0)).astype(f32)
    lane = lax.broadcasted_iota(jnp.int32, (1, PAIR), 1)
    first = lane < HEAD_DIM

    tl = t0 + lax.broadcasted_iota(jnp.int32, (8, tq), 1)
    rid = lax.broadcasted_iota(jnp.int32, (8, tq), 0)
    alibi_rows = jnp.where(rid == 0, ((tl // SEL_BLOCK) * SEL_BLOCK).astype(f32),
                           jnp.where(rid == 1, (tl % SEL_BLOCK).astype(f32),
                                     jnp.where(rid == 2, float(SEL_BLOCK), jnp.where(rid == 3, 1.0, 0.0))))
    aux0 = jnp.concatenate([jnp.zeros((ns, tq), f32), alibi_rows, jnp.zeros((PAIR - ns - 8, tq), f32)], axis=0).T

    def augment(q8, aux_by_group):
        aux8 = jnp.concatenate([aux_by_group[h // R] * jnp.where(lane < ns, 1.0, 2.0 ** (-(h + 1)))
                                for h in range(H)], axis=0)
        return jnp.concatenate([q8, aux8.astype(bf16)], axis=1)

    def per_head(a):
        return a.reshape(H, tq, a.shape[-1])

    def with_ones(v):
        one = jnp.ones((), v.dtype)
        return jnp.where(first, v, one), jnp.where(first, one, v)

    def weighted_values(p, v):
        v0, v1 = with_ones(v)
        return jnp.concatenate([jnp.dot(p[:gm], v0, preferred_element_type=f32),
                                jnp.dot(p[gm:], v1, preferred_element_type=f32)], axis=0)

    def split_sum(acc):
        top, bot = acc[:gm], acc[gm:]
        return jnp.concatenate([top * (1.0 / top[:, HEAD_DIM:HEAD_DIM + 1]), bot * (1.0 / bot[:, 0:1])], axis=0)

    q8 = q_ref[0].reshape(H * tq, PAIR)
    qa = augment(q8, [aux0, aux0])

    kc_aug = jnp.concatenate([kc_ref[0], auxc_ref[...]], axis=1)
    sc = per_head(_dot_nt(qa, kc_aug))
    w0 = pl.multiple_of(jnp.maximum(t0 - WINDOW, 0), tq)
    wrows = pl.ds(w0, WINDOW + tq)
    kw_aug = jnp.concatenate([kw_ref[0, wrows, :], auxk_ref[wrows, :]], axis=1)
    sw = per_head(_dot_nt(qa, kw_aug))

    cend = (lax.broadcasted_iota(jnp.int32, (1, nc), 1) * CMP_STRIDE + (CMP_LEN - 1)).astype(f32)
    vis_c = tok >= cend
    any_c = (tok >= float(CMP_LEN - 1)).astype(f32)
    s = jnp.where(vis_c[None], sc, NEG_INF)
    e = jnp.exp(s - jnp.max(s, axis=-1, keepdims=True))
    p_cmp = e * (any_c[None] / jnp.sum(e, axis=-1, keepdims=True))
    o_cmp = _dot(p_cmp.reshape(H * tq, nc), vc_ref[0])

    jrow = lax.broadcasted_iota(jnp.int32, (ns, tq), 0)
    cur = (t0 + lax.broadcasted_iota(jnp.int32, (ns, tq), 1)) // SEL_BLOCK
    forced = (jrow == 0) | (jrow == cur) | (jrow == cur - 1)
    keys_t = []
    for g in range(G):
        psum = p_cmp[R * g]
        for r in range(1, R):
            psum = psum + p_cmp[R * g + r]
        imp_t = lax.dot_general(selT_ref[...], psum.astype(bf16), (((1,), (1,)), ((), ())),
                                preferred_element_type=f32)
        p_lo = (psum - psum.astype(bf16).astype(f32)).astype(bf16)
        imp_t = imp_t + lax.dot_general(selT_ref[...], p_lo, (((1,), (1,)), ((), ())),
                                        preferred_element_type=f32)
        score = jnp.where(forced, FORCE_SCORE, jnp.where(jrow <= cur, imp_t, -1.0))
        keys_t.append(jnp.where(score < 0.0, -1, pltpu.bitcast(score, jnp.int32)))
        rank_scr[g] = keys_t[g]

    def rank_rows(i8, cnts):
        base = pl.multiple_of(i8 * 8, 8)
        out = []
        for g in range(G):
            rows8 = rank_scr[g, pl.ds(base, 8), :]
            cnt = cnts[g]
            for k in range(8):
                si = rows8[k:k + 1, :]
                ahead = jnp.where(jrow > base + k, si + 1, si) > keys_t[g]
                cnt = cnt + jnp.where(ahead, 1.0, 0.0)
            out.append(cnt)
        return tuple(out)

    n_rows8 = jnp.minimum((t0 + tq - 1) // SEL_BLOCK // 8 + 1, ns // 8)
    cnts = lax.fori_loop(0, n_rows8, rank_rows, tuple(jnp.zeros((ns, tq), f32) for _ in range(G)))
    aux_sel = []
    for g in range(G):
        bias_t = jnp.where(cnts[g] < float(SEL_TOPK), 0.0, NEG_INF)
        aux_sel.append(aux0 + jnp.concatenate([bias_t, jnp.zeros((PAIR - ns, tq), f32)], axis=0).T)
    qs = augment(q8, aux_sel)

    def key_rows(kt):
        return pl.ds(pl.multiple_of(kt * tk, tk), tk)

    def scores(kt, dst):
        rows_ = key_rows(kt)
        dst[...] = _dot_nt(qs, jnp.concatenate([ks_ref[0, rows_, :], auxk_ref[rows_, :]], axis=1))

    def attend(src, kt, diag):
        s8 = src[...]
        if diag:
            kpos = (kt * tk + lax.broadcasted_iota(jnp.int32, (1, tk), 1)).astype(f32)
            s8 = jnp.where((tok >= kpos)[None], per_head(s8), NEG_INF).reshape(H * tq, tk)
            m_new = jnp.max(s8, axis=-1, keepdims=True)
            p = jnp.exp(s8 - m_new).astype(bf16)
            acc_scr[...] = weighted_values(p, vs_ref[0, key_rows(kt), :])
        else:
            m_prev = m_scr[...][:, 0:1]
            m_new = jnp.maximum(m_prev, jnp.max(s8, axis=-1, keepdims=True))
            p = jnp.exp(s8 - m_new).astype(bf16)
            acc_scr[...] = (jnp.exp(m_prev - m_new) * acc_scr[...]
                            + weighted_values(p, vs_ref[0, key_rows(kt), :]))
        m_scr[...] = jnp.broadcast_to(m_new, m_scr.shape)

    n_full = t0 // tk
    scores(n_full, sa_scr)

    kpos = w0 + lax.broadcasted_iota(jnp.int32, (1, WINDOW + tq), 1)
    dist_w = tok - kpos.astype(f32)
    vis_w = jnp.abs(dist_w - (WINDOW - 1) / 2.0) < WINDOW / 2.0
    s = jnp.where(vis_w[None], sw, NEG_INF)
    e = jnp.exp(s - jnp.max(s, axis=-1, keepdims=True)).astype(bf16).reshape(H * tq, WINDOW + tq)
    acc_win = weighted_values(e, vw_ref[0, wrows, :])

    scores(0, sb_scr)
    attend(sa_scr, n_full, True)

    def two_tiles(j, carry):
        scores(2 * j + 1, sa_scr)
        attend(sb_scr, 2 * j, False)
        scores(2 * j + 2, sb_scr)
        attend(sa_scr, 2 * j + 1, False)
        return carry

    lax.fori_loop(0, n_full // 2, two_tiles, 0)

    @pl.when(n_full % 2 == 1)
    def _():
        attend(sb_scr, n_full - 1, False)

    o_sel = split_sum(acc_scr[...])
    o_win = split_sum(acc_win)

    gate = gate_ref[0]
    o_cmp, o_sel, o_win = per_head(o_cmp), per_head(o_sel), per_head(o_win)
    for r in range(R):
        pair = []
        for g in range(G):
            h = R * g + r
            pair.append(gate[:, 3 * h:3 * h + 1] * o_cmp[h] + gate[:, 3 * h + 1:3 * h + 2] * o_sel[h]
                        + gate[:, 3 * h + 2:3 * h + 3] * o_win[h])
        o_ref[0, :, r * PAIR:(r + 1) * PAIR] = jnp.where(first, pair[0], pair[1])


def _nsa(q, kc, vc, ks, vs, kw, vw, gates, sel_t, tq, tk):
    B, H, S, _ = q.shape
    nc, ns = S // CMP_STRIDE, S // SEL_BLOCK
    aux_k = _alibi_key_columns(np.arange(S), ns, onehot=True)
    aux_c = _alibi_key_columns(np.arange(nc) * CMP_STRIDE + (CMP_LEN - 1), ns, onehot=False)
    full = lambda n: pl.BlockSpec((1, n, PAIR), lambda b, i: (b, 0, 0))
    const = lambda a: pl.BlockSpec(a.shape, lambda b, i: (0, 0))
    body = functools.partial(_nsa_body, tq=tq, tk=tk, seq=S)
    return pl.pallas_call(
        body,
        grid=(B, S // tq),
        in_specs=[pl.BlockSpec((1, H, tq, PAIR), lambda b, i: (b, 0, i, 0)),
                  full(nc), full(nc), full(S), full(S), full(S), full(S),
                  pl.BlockSpec((1, tq, 128), lambda b, i: (b, i, 0)),
                  const(sel_t), const(aux_k), const(aux_c)],
        out_specs=pl.BlockSpec((1, tq, NSA_GROUP * PAIR), lambda b, i: (b, i, 0)),
        out_shape=jax.ShapeDtypeStruct((B, S, NSA_GROUP * PAIR), f32),
        scratch_shapes=[pltpu.VMEM((H * tq, PAIR), f32)] * 2 + [pltpu.VMEM((H * tq, tk), f32)] * 2
        + [pltpu.VMEM((NSA_KV_HEADS, ns, tq), jnp.int32)],
        compiler_params=pltpu.CompilerParams(
            dimension_semantics=("arbitrary", "arbitrary"), vmem_limit_bytes=VMEM_LIMIT),
        name="nsa",
    )(q, kc, vc, ks, vs, kw, vw, gates, sel_t, aux_k, aux_c)


def _rwkv_body(r_s, lw_s, k_s, v_s, kk_s, b_s, g_s, bv_s, lnw_ref, lnb_ref, y_ref, s_scr, *, tt):
    C = CHUNK
    n_pairs = RWKV_WIDTH // PAIR

    @pl.when(pl.program_id(1) == 0)
    def _():
        s_scr[...] = jnp.zeros(s_scr.shape, f32)

    ri = lax.broadcasted_iota(jnp.int32, (C, C), 0)
    ci = lax.broadcasted_iota(jnp.int32, (C, C), 1)
    cum_mat = jnp.where(ri >= ci, 1.0, 0.0).astype(bf16)
    ri2 = lax.broadcasted_iota(jnp.int32, (C, 2 * C), 0)
    ci2 = lax.broadcasted_iota(jnp.int32, (C, 2 * C), 1) % C
    tri2_incl = ri2 >= ci2
    tri2_strict = ri2 > ci2
    lane = lax.broadcasted_iota(jnp.int32, (1, PAIR), 1)
    first = lane < HEAD_DIM
    blockdiag = (lax.broadcasted_iota(jnp.int32, (PAIR, PAIR), 0) // HEAD_DIM
                 == lax.broadcasted_iota(jnp.int32, (PAIR, PAIR), 1) // HEAD_DIM)
    head_mean = jnp.where(blockdiag, 1.0, 0.0).astype(bf16)

    def swap_heads(x):
        return pltpu.roll(x, HEAD_DIM, axis=1)

    def by_head_rows(x):
        return jnp.concatenate([jnp.where(first, x, 0.0), jnp.where(first, 0.0, x)], axis=0)

    def block_diag(x2):
        zero = jnp.zeros((x2.shape[0], PAIR), x2.dtype)
        return jnp.concatenate([jnp.concatenate([x2[:, :PAIR], zero], axis=1),
                                jnp.concatenate([zero, x2[:, PAIR:]], axis=1)], axis=0)

    def group(gi):
        units = []
        for cc in range(SCAN_GROUP):
            rows = pl.ds((gi * SCAN_GROUP + cc) * C, C)
            r_, lw_, k_, v_, kk_, b_ = (s[0, rows, :] for s in (r_s, lw_s, k_s, v_s, kk_s, b_s))
            cum = _split3_dot(cum_mat, lw_)
            cum_end = cum[C - 1:C, :]
            e_neg = jnp.exp(-cum)
            e_end = jnp.exp(cum_end - cum)
            a_t = -kk_ * jnp.exp(cum - lw_)
            r_t = r_ * jnp.exp(cum)
            b_t = b_ * e_neg
            k_t = k_ * e_neg
            b_h = b_ * e_end
            k_h = k_ * e_end
            w_end = jnp.exp(cum_end)
            for p in range(n_pairs):
                pc = slice(p * PAIR, (p + 1) * PAIR)
                units.append(dict(p=p, rows=rows, pc=pc, a_t=a_t[:, pc],
                                  r_t=r_t[:, pc], b_t=b_t[:, pc], k_t=k_t[:, pc], b_h=b_h[:, pc], k_h=k_h[:, pc],
                                  v=v_[:, pc], w_end=w_end[:, pc]))

        for u in units:
            lhs = jnp.concatenate([u["a_t"], u["r_t"]], axis=0)
            u["ab"] = _dot_nt(lhs, by_head_rows(u["b_t"]))
            u["ak"] = _dot_nt(lhs, by_head_rows(u["k_t"]))
        for u in units:
            ab, ak = u["ab"], u["ak"]
            u["a"] = jnp.where(tri2_strict, ab[:C], 0.0)
            u["a_rb"] = jnp.where(tri2_incl, ab[C:], 0.0)
            u["akrk"] = jnp.concatenate([jnp.where(tri2_strict, ak[:C], 0.0), jnp.where(tri2_incl, ak[C:], 0.0)],
                                        axis=0)
        for u in units:
            v_sw = swap_heads(u["v"]).astype(bf16)
            x1 = _dot(u["akrk"], block_diag(jnp.concatenate([v_sw, v_sw], axis=1)))
            u["x"] = jnp.concatenate([jnp.where(first, u["a_t"], x1[:C, :PAIR]),
                                      jnp.where(first, x1[:C, PAIR:], u["a_t"])], axis=1)
            u["arkv"] = x1[C:]
            u["pows"] = [u["a"], _dot(u["a"], by_head_rows(u["a"]))]
        for j in range(6):
            for u in units:
                u["x"] = u["x"] + _dot(u["pows"][j], block_diag(u["x"].astype(bf16)))
            if j + 2 <= 5:
                for u in units:
                    u["pows"].append(_dot(u["pows"][j + 1], by_head_rows(u["pows"][j + 1])))
        for u in units:
            u["ex"] = _dot(u["a_rb"], block_diag(u["x"].astype(bf16)))
        for u in units:
            x0, x1 = u["x"][:, :PAIR], u["x"][:, PAIR:]
            ex0, ex1 = u["ex"][:, :PAIR], u["ex"][:, PAIR:]
            ua = jnp.where(first, x0, x1)
            w2 = swap_heads(jnp.where(first, x1, x0))
            u["rq"] = u["r_t"] + jnp.where(first, ex0, ex1)
            u["yin"] = swap_heads(jnp.where(first, u["arkv"][:, PAIR:] + ex1, u["arkv"][:, :PAIR] + ex0))
            u["gmat"] = jnp.where(blockdiag, _dot_tn(ua, u["b_h"]), 0.0)
            u["qmat"] = jnp.where(blockdiag, _dot_tn(jnp.concatenate([w2, u["v"]], axis=0),
                                                     jnp.concatenate([u["b_h"], u["k_h"]], axis=0)), 0.0)
        state = [s_scr[p] for p in range(n_pairs)]
        for u in units:
            s0 = state[u["p"]]
            u["y"] = _dot_nt(u["rq"], s0) + u["yin"]
            state[u["p"]] = s0 * u["w_end"] + _dot(s0, u["gmat"]) + u["qmat"]
        for p in range(n_pairs):
            s_scr[p] = state[p]
        for u in units:
            y = u["y"]
            mu = jnp.dot(y.astype(bf16), head_mean, preferred_element_type=f32) * (1.0 / HEAD_DIM)
            u["d"] = y - mu
        for u in units:
            d = u["d"]
            var = jnp.dot((d * d).astype(bf16), head_mean, preferred_element_type=f32) * (1.0 / HEAD_DIM)
            yn = d * lax.rsqrt(var + GN_EPS) * lnw_ref[:, u["pc"]] + lnb_ref[:, u["pc"]]
            y_ref[0, u["rows"], u["pc"]] = (yn + bv_s[0, u["rows"], u["pc"]]) * g_s[0, u["rows"], u["pc"]]

    for gi in range(tt // (C * SCAN_GROUP)):
        group(gi)


def _rwkv(r, lw, k, v, kk, b, g, bv, lnw, lnb, tt):
    B, S, W = r.shape
    tok = pl.BlockSpec((1, tt, W), lambda bb, i: (bb, i, 0))
    const = pl.BlockSpec((1, W), lambda bb, i: (0, 0))
    return pl.pallas_call(
        functools.partial(_rwkv_body, tt=tt),
        grid=(B, S // tt),
        in_specs=[tok] * 8 + [const, const],
        out_specs=tok,
        out_shape=jax.ShapeDtypeStruct((B, S, W), f32),
        scratch_shapes=[pltpu.VMEM((W // PAIR, PAIR, PAIR), f32)],
        compiler_params=pltpu.CompilerParams(
            dimension_semantics=("arbitrary", "arbitrary"), vmem_limit_bytes=VMEM_LIMIT),
        name="rwkv",
    )(r, lw, k, v, kk, b, g, bv, lnw, lnb)


def _post_body(x_ref, yn_ref, yr_ref, p_ref, wo_ref, gpost_ref, gpre_ref, gmlp_ref, wup_ref, wdn_ref,
               wpg_ref, wple_ref, o_ref):
    y = jnp.concatenate([yn_ref[0], yr_ref[0]], axis=1).astype(bf16)
    mix = jnp.dot(y, wo_ref[...], preferred_element_type=f32)
    x1 = x_ref[0] + _rms(mix, gpost_ref[...])
    h = _rms(x1, gpre_ref[...]).astype(bf16)
    acc = None
    for c in range(D_FF // D_MODEL):
        cs = slice(c * D_MODEL, (c + 1) * D_MODEL)
        u = jnp.dot(h, wup_ref[:, cs], preferred_element_type=f32)
        u = jnp.square(jnp.maximum(u, 0.0)).astype(bf16)
        part = jnp.dot(u, wdn_ref[cs, :], preferred_element_type=f32)
        acc = part if acc is None else acc + part
    x2 = x1 + _rms(acc, gmlp_ref[...])
    gate = _sigmoid(jnp.dot(x2.astype(bf16), wpg_ref[...], preferred_element_type=f32))
    o_ref[0] = x2 + gate * jnp.dot(p_ref[0].astype(bf16), wple_ref[...], preferred_element_type=f32)


def _post(x, yn, yr, p, weights, tm):
    B, S, D = x.shape
    tok = lambda w: pl.BlockSpec((1, tm, w), lambda b, i: (b, i, 0))
    const = lambda a: pl.BlockSpec(a.shape, lambda b, i: (0,) * a.ndim, pipeline_mode=pl.Buffered(1))
    return pl.pallas_call(
        _post_body,
        grid=(B, S // tm),
        in_specs=[tok(D), tok(512), tok(512), tok(PLE_DIM)] + [const(a) for a in weights],
        out_specs=tok(D),
        out_shape=jax.ShapeDtypeStruct((B, S, D), f32),
        compiler_params=pltpu.CompilerParams(
            dimension_semantics=("arbitrary", "arbitrary"), vmem_limit_bytes=VMEM_LIMIT),
        name="post",
    )(x, yn, yr, p, *weights)


def _pack_inproj(w_in, gate_bias):
    wg = jnp.pad(w_in[:, 1280:1304], ((0, 0), (0, 128 - 24)))
    wcat = jnp.concatenate([w_in[:, :1280], wg, w_in[:, 1304:]], axis=1).astype(bf16)
    return wcat, jnp.pad(gate_bias, (0, 128 - 24)).reshape(1, 128)


def _pack_compress(pe, w1, b1, w2):
    eye2 = jnp.eye(NSA_KV_HEADS, dtype=f32)
    w1r = w1.reshape(CMP_LEN, HEAD_DIM, CMP_HIDDEN)
    halves = []
    for part in (w1r[:CMP_STRIDE], w1r[CMP_STRIDE:]):
        halves.append(jnp.einsum("jdc,gh->jgdhc", part, eye2).reshape(CMP_STRIDE * PAIR, 2 * CMP_HIDDEN))
    w = jnp.stack(halves).astype(bf16)
    per = jnp.broadcast_to(pe.reshape(2, CMP_STRIDE, 1, HEAD_DIM), (2, CMP_STRIDE, NSA_KV_HEADS, HEAD_DIM))
    per = per.reshape(2, CMP_STRIDE * PAIR)
    b1p = jnp.tile(b1, NSA_KV_HEADS).reshape(1, 2 * CMP_HIDDEN)
    w2p = jnp.einsum("cd,gh->gchd", w2, eye2).reshape(2 * CMP_HIDDEN, PAIR).astype(bf16)
    return per, w, b1p, w2p


def _sel_map_t(seq):
    nc, ns = seq // CMP_STRIDE, seq // SEL_BLOCK
    c0 = np.arange(nc) * CMP_STRIDE
    s0 = np.arange(ns) * SEL_BLOCK
    ov = (np.minimum(c0[:, None] + CMP_LEN - 1, s0[None, :] + SEL_BLOCK - 1)
          - np.maximum(c0[:, None], s0[None, :]) + 1)
    m = np.clip(ov, 0, None).astype(np.float32) / CMP_STRIDE
    m[nc - 1] = 0.0
    return jnp.asarray(m.T, dtype=bf16)


def _row(a):
    return a.reshape(1, -1)


def _mixers(x, g_mix_pre, w_in, nsa_gate_bias, cmp_k, cmp_v, shift_mu, w0, w_lora_up, a0, a_lora_up, g_lora_up,
            k_k, k_a, r_k, lnx_w, lnx_b):
    S = x.shape[1]
    bd = jnp.asarray(np.kron(np.eye(RWKV_WIDTH // HEAD_DIM), np.ones((HEAD_DIM, HEAD_DIM))), dtype=bf16)
    wcat, gbias = _pack_inproj(w_in, nsa_gate_bias)
    wl = jnp.concatenate([w_lora_up, jnp.zeros((ICLR_LORA, RWKV_WIDTH), f32)], axis=0).astype(bf16)
    al = jnp.concatenate([jnp.zeros((DECAY_LORA, RWKV_WIDTH), f32), a_lora_up], axis=0).astype(bf16)
    rwkv_params = (_row(shift_mu), _row(w0), wl, _row(a0), al, g_lora_up.astype(bf16),
                   _row(k_k), _row(k_a), _row(r_k), bd)
    q, kc, vc, ks, vs, kw, vw, gates, r, lw, k2, v, kk, b, g, bv = _inproj(
        x, _row(g_mix_pre), wcat, gbias, rwkv_params, tm=INPROJ_TM)
    kcmp, vcmp = _compress(kc, vc, _pack_compress(*cmp_k), _pack_compress(*cmp_v))
    y_nsa = _nsa(q, kcmp, vcmp, ks, vs, kw, vw, gates, _sel_map_t(S), tq=NSA_TQ, tk=NSA_TK)
    y_rwkv = _rwkv(r, lw, k2, v, kk, b, g, bv, _row(lnx_w), _row(lnx_b), tt=RWKV_TT)
    return r, y_nsa, y_rwkv


def kernel(x, p, g_mix_pre, g_mix_post, g_mlp_pre, g_mlp_post, w_in, nsa_gate_bias, cmp_pe_k, cmp_k_w1, cmp_k_b1, cmp_k_w2, cmp_pe_v, cmp_v_w1, cmp_v_b1, cmp_v_w2, shift_mu, w0, w_lora_up, a0, a_lora_up, g_lora_up, k_k, k_a, r_k, lnx_w, lnx_b, w_out, w_up, w_down, w_ple, w_ple_gate):
    D = x.shape[-1]
    for i in range(p.shape[0]):
        _, y_nsa, y_rwkv = _mixers(
            x, g_mix_pre[i], w_in[i], nsa_gate_bias[i],
            (cmp_pe_k[i], cmp_k_w1[i], cmp_k_b1[i], cmp_k_w2[i]), (cmp_pe_v[i], cmp_v_w1[i], cmp_v_b1[i], cmp_v_w2[i]),
            shift_mu[i], w0[i], w_lora_up[i], a0[i], a_lora_up[i], g_lora_up[i], k_k[i], k_a[i], r_k[i],
            lnx_w[i], lnx_b[i])
        wo_nsa = w_out[i][:512].reshape(NSA_KV_HEADS, NSA_GROUP, HEAD_DIM, D).transpose(1, 0, 2, 3).reshape(512, D)
        wo = jnp.concatenate([wo_nsa, w_out[i][512:]], axis=0).astype(bf16)
        weights = (wo, _row(g_mix_post[i]), _row(g_mlp_pre[i]), _row(g_mlp_post[i]), w_up[i].astype(bf16),
                   w_down[i].astype(bf16), w_ple_gate[i].astype(bf16), w_ple[i].astype(bf16))
        x = _post(x, y_nsa, y_rwkv, p[i], weights, tm=POST_TM)
    return x
```

```python
import functools

import jax
import jax.numpy as jnp
import numpy as np
from jax import lax
from jax.experimental import pallas as pl
from jax.experimental.pallas import tpu as pltpu

f32 = jnp.float32
bf16 = jnp.bfloat16

D_MODEL = 1024
HEAD_DIM = 64
NSA_HEADS = 8
NSA_KV_HEADS = 2
NSA_GROUP = NSA_HEADS // NSA_KV_HEADS
CMP_LEN = 32
CMP_STRIDE = 16
CMP_HIDDEN = 2 * HEAD_DIM
SEL_BLOCK = 64
SEL_TOPK = 16
WINDOW = 512
RWKV_WIDTH = 512
RWKV_COLS = 1792
DECAY_LORA = 64
ICLR_LORA = 64
D_FF = 4 * D_MODEL
PLE_DIM = 256
NORM_EPS = 1e-6
GN_EPS = 64e-5
NEG_INF = -1e30
FORCE_SCORE = 1e4

PAIR = 2 * HEAD_DIM
CHUNK = 64
SCAN_GROUP = 4

INPROJ_TM = 512
NSA_TQ = 256
NSA_TK = 512
RWKV_TT = CHUNK * SCAN_GROUP
POST_TM = 512
VMEM_LIMIT = 56 * 1024 * 1024

_Q0, _KV0, _GATE0, _RW0, _WCOLS = 0, 512, 1280, 1408, 3200


def _dot(a, b):
    return jnp.dot(a.astype(bf16), b.astype(bf16), preferred_element_type=f32)


def _dot_nt(a, b):
    return lax.dot_general(a.astype(bf16), b.astype(bf16), (((1,), (1,)), ((), ())), preferred_element_type=f32)


def _dot_tn(a, b):
    return lax.dot_general(a.astype(bf16), b.astype(bf16), (((0,), (0,)), ((), ())), preferred_element_type=f32)


def _split3_dot(w, x):
    hi = x.astype(bf16)
    r1 = x - hi.astype(f32)
    mid = r1.astype(bf16)
    lo = (r1 - mid.astype(f32)).astype(bf16)
    return (jnp.dot(w, hi, preferred_element_type=f32) + jnp.dot(w, mid, preferred_element_type=f32)
            + jnp.dot(w, lo, preferred_element_type=f32))


def _rms(x, g):
    ms = jnp.mean(x * x, axis=-1, keepdims=True)
    return x * lax.rsqrt(ms + NORM_EPS) * g


def _sigmoid(x):
    return 1.0 / (1.0 + jnp.exp(-x))


def _inproj_body(x_ref, g_ref, w_ref, gb_ref, mu_ref, w0_ref, wl_ref, a0_ref, al_ref, gl_ref, kkw_ref, ka_ref,
                 rk_ref, bd_ref, q_ref, kc_ref, vc_ref, ks_ref, vs_ref, kw_ref, vw_ref, gate_ref,
                 r_ref, lw_ref, k_ref, v_ref, kk_ref, b_ref, g_out_ref, bv_ref, zlast_scr):
    @pl.when(pl.program_id(1) == 0)
    def _():
        zlast_scr[...] = jnp.zeros(zlast_scr.shape, f32)

    h = _rms(x_ref[0], g_ref[...]).astype(bf16)
    z = jnp.dot(h, w_ref[:, _RW0:_WCOLS], preferred_element_type=f32)
    tm = z.shape[0]
    prev_row = zlast_scr[7:8, :]
    zlast_scr[...] = z[tm - 8:tm, :]
    q = jnp.dot(h, w_ref[:, _Q0:_KV0], preferred_element_type=f32) * (HEAD_DIM ** -0.5)
    first = lax.broadcasted_iota(jnp.int32, (1, PAIR), 1) < HEAD_DIM
    for hd in range(NSA_HEADS):
        two = q[:, (hd // 2) * PAIR:(hd // 2 + 1) * PAIR]
        if (hd % 2) != (hd // NSA_GROUP):
            two = pltpu.roll(two, HEAD_DIM, axis=1)
        keep = first if hd < NSA_GROUP else jnp.logical_not(first)
        q_ref[0, hd] = jnp.where(keep, two, 0.0).astype(bf16)
    kv = jnp.dot(h, w_ref[:, _KV0:_GATE0], preferred_element_type=f32)
    kc_ref[0] = kv[:, 0:128]
    vc_ref[0] = kv[:, 128:256]
    ks_ref[0] = kv[:, 256:384].astype(bf16)
    vs_ref[0] = kv[:, 384:512].astype(bf16)
    kw_ref[0] = kv[:, 512:640].astype(bf16)
    vw_ref[0] = kv[:, 640:768].astype(bf16)
    gl = jnp.dot(h, w_ref[:, _GATE0:_RW0], preferred_element_type=f32)
    gate_ref[0] = _sigmoid(gl + gb_ref[...])

    row = lax.broadcasted_iota(jnp.int32, (tm, 1), 0)
    z_prev = jnp.where(row == 0, prev_row, pltpu.roll(z, 1, axis=0))
    zs = z + (z_prev - z) * mu_ref[...]
    r = zs[:, 0:512]
    k = zs[:, 512:1024]
    v = zs[:, 1024:1536]
    lora = zs[:, 1536:1664]
    gd = zs[:, 1664:1792]
    wlog = w0_ref[...] + _dot(jnp.tanh(lora), wl_ref[...])
    sp = jnp.maximum(-wlog, 0.0) + jnp.log(1.0 + jnp.exp(-jnp.abs(wlog)))
    a = _sigmoid(a0_ref[...] + _dot(lora, al_ref[...]))
    kk = k * kkw_ref[...]
    kk = kk * lax.rsqrt(jnp.maximum(_dot(kk * kk, bd_ref[...]), 1e-24))
    k2 = k * (1.0 + (a - 1.0) * ka_ref[...])
    r_ref[0] = r
    lw_ref[0] = -jnp.exp(-sp - 0.5)
    k_ref[0] = k2
    v_ref[0] = v
    kk_ref[0] = kk
    b_ref[0] = kk * a
    g_out_ref[0] = _dot(_sigmoid(gd), gl_ref[...])
    bv_ref[0] = _dot(r * k2 * rk_ref[...], bd_ref[...]) * v


def _inproj(x, g, wcat, gbias, rwkv_params, tm):
    B, S, D = x.shape
    tok = lambda w: pl.BlockSpec((1, tm, w), lambda b, i: (b, i, 0))
    const = lambda a: pl.BlockSpec(a.shape, lambda b, i: (0,) * a.ndim)
    consts = (g, wcat, gbias) + tuple(rwkv_params)
    return pl.pallas_call(
        _inproj_body,
        grid=(B, S // tm),
        in_specs=[tok(D)] + [const(a) for a in consts],
        out_specs=[pl.BlockSpec((1, NSA_HEADS, tm, PAIR), lambda b, i: (b, 0, i, 0))]
        + [tok(128)] * 7 + [tok(RWKV_WIDTH)] * 8,
        out_shape=[jax.ShapeDtypeStruct((B, NSA_HEADS, S, PAIR), bf16),
                   jax.ShapeDtypeStruct((B, S, 128), f32), jax.ShapeDtypeStruct((B, S, 128), f32)]
        + [jax.ShapeDtypeStruct((B, S, 128), bf16)] * 4
        + [jax.ShapeDtypeStruct((B, S, 128), f32)] + [jax.ShapeDtypeStruct((B, S, RWKV_WIDTH), f32)] * 8,
        scratch_shapes=[pltpu.VMEM((8, RWKV_COLS), f32)],
        compiler_params=pltpu.CompilerParams(
            dimension_semantics=("arbitrary", "arbitrary"), vmem_limit_bytes=VMEM_LIMIT),
        name="inproj",
    )(x, *consts)


def _gelu_tanh(x):
    return x * (0.5 * (1.0 + jnp.tanh(np.sqrt(2.0 / np.pi) * (x + 0.044715 * (x * x * x)))))


def _compress_one(x_ref, pe_ref, w_ref, b1_ref, w2_ref):
    n = x_ref.shape[1] // CMP_STRIDE
    x = jnp.concatenate([x_ref[0, pl.ds(j, n, stride=CMP_STRIDE), :] for j in range(CMP_STRIDE)], axis=1)
    lo = _dot(x + pe_ref[0:1, :], w_ref[0])
    hi = _dot(x + pe_ref[1:2, :], w_ref[1])
    pre = lo + pltpu.roll(hi, n - 1, axis=0) + b1_ref[...]
    return _dot(_gelu_tanh(pre), w2_ref[...])


def _compress_body(xk_ref, xv_ref, pek_ref, wk_ref, bk_ref, w2k_ref, pev_ref, wv_ref, bv_ref, w2v_ref,
                   kc_ref, vc_ref):
    kc_ref[0] = _compress_one(xk_ref, pek_ref, wk_ref, bk_ref, w2k_ref).astype(bf16)
    vc_ref[0] = _compress_one(xv_ref, pev_ref, wv_ref, bv_ref, w2v_ref).astype(bf16)


def _compress(xk, xv, kparams, vparams):
    B, S, W = xk.shape
    NC = S // CMP_STRIDE
    const = lambda a: pl.BlockSpec(a.shape, lambda b: (0,) * a.ndim)
    seq = pl.BlockSpec((1, S, W), lambda b: (b, 0, 0))
    out = pl.BlockSpec((1, NC, PAIR), lambda b: (b, 0, 0))
    return pl.pallas_call(
        _compress_body,
        grid=(B,),
        in_specs=[seq, seq] + [const(a) for a in kparams] + [const(a) for a in vparams],
        out_specs=[out, out],
        out_shape=[jax.ShapeDtypeStruct((B, NC, PAIR), bf16)] * 2,
        compiler_params=pltpu.CompilerParams(dimension_semantics=("arbitrary",), vmem_limit_bytes=VMEM_LIMIT),
        name="compress",
    )(xk, xv, *kparams, *vparams)


def _alibi_key_columns(pos, ns, onehot):
    a = np.zeros((pos.shape[0], PAIR), np.float32)
    if onehot:
        a[np.arange(pos.shape[0]), pos // SEL_BLOCK] = 1.0
    a[:, ns] = -1.0
    a[:, ns + 1] = -1.0
    a[:, ns + 2] = pos // SEL_BLOCK
    a[:, ns + 3] = pos % SEL_BLOCK
    return jnp.asarray(a, dtype=bf16)


def _nsa_body(q_ref, kc_ref, vc_ref, ks_ref, vs_ref, kw_ref, vw_ref, gate_ref, selT_ref, auxk_ref, auxc_ref,
              o_ref, m_scr, acc_scr, sa_scr, sb_scr, rank_scr, *, tq, tk, seq):
    R, G, H = NSA_GROUP, NSA_KV_HEADS, NSA_HEADS
    t0 = pl.program_id(1) * tq
    nc = seq // CMP_STRIDE
    ns = seq // SEL_BLOCK
    gm = R * tq
    tok = (t0 + lax.broadcasted_iota(jnp.int32, (tq, 1), 0)).astype(f32)
    lane = lax.broadcasted_iota(jnp.int32, (1, PAIR), 1)
    first = lane < HEAD_DIM

    tl = t0 + lax.broadcasted_iota(jnp.int32, (8, tq), 1)
    rid = lax.broadcasted_iota(jnp.int32, (8, tq), 0)
    alibi_rows = jnp.where(rid == 0, ((tl // SEL_BLOCK) * SEL_BLOCK).astype(f32),
                           jnp.where(rid == 1, (tl % SEL_BLOCK).astype(f32),
                                     jnp.where(rid == 2, float(SEL_BLOCK), jnp.where(rid == 3, 1.0, 0.0))))
    aux0 = jnp.concatenate([jnp.zeros((ns, tq), f32), alibi_rows, jnp.zeros((PAIR - ns - 8, tq), f32)], axis=0).T

    def augment(q8, aux_by_group):
        aux8 = jnp.concatenate([aux_by_group[h // R] * jnp.where(lane < ns, 1.0, 2.0 ** (-(h + 1)))
                                for h in range(H)], axis=0)
        return jnp.concatenate([q8, aux8.astype(bf16)], axis=1)

    def per_head(a):
        return a.reshape(H, tq, a.shape[-1])

    def with_ones(v):
        one = jnp.ones((), v.dtype)
        return jnp.where(first, v, one), jnp.where(first, one, v)

    def weighted_values(p, v):
        v0, v1 = with_ones(v)
        return jnp.concatenate([jnp.dot(p[:gm], v0, preferred_element_type=f32),
                                jnp.dot(p[gm:], v1, preferred_element_type=f32)], axis=0)

    def split_sum(acc):
        top, bot = acc[:gm], acc[gm:]
        return jnp.concatenate([top * (1.0 / top[:, HEAD_DIM:HEAD_DIM + 1]), bot * (1.0 / bot[:, 0:1])], axis=0)

    q8 = q_ref[0].reshape(H * tq, PAIR)
    qa = augment(q8, [aux0, aux0])

    kc_aug = jnp.concatenate([kc_ref[0], auxc_ref[...]], axis=1)
    sc = per_head(_dot_nt(qa, kc_aug))
    w0 = pl.multiple_of(jnp.maximum(t0 - WINDOW, 0), tq)
    wrows = pl.ds(w0, WINDOW + tq)
    kw_aug = jnp.concatenate([kw_ref[0, wrows, :], auxk_ref[wrows, :]], axis=1)
    sw = per_head(_dot_nt(qa, kw_aug))

    cend = (lax.broadcasted_iota(jnp.int32, (1, nc), 1) * CMP_STRIDE + (CMP_LEN - 1)).astype(f32)
    vis_c = tok >= cend
    any_c = (tok >= float(CMP_LEN - 1)).astype(f32)
    s = jnp.where(vis_c[None], sc, NEG_INF)
    e = jnp.exp(s - jnp.max(s, axis=-1, keepdims=True))
    p_cmp = e * (any_c[None] / jnp.sum(e, axis=-1, keepdims=True))
    o_cmp = _dot(p_cmp.reshape(H * tq, nc), vc_ref[0])

    jrow = lax.broadcasted_iota(jnp.int32, (ns, tq), 0)
    cur = (t0 + lax.broadcasted_iota(jnp.int32, (ns, tq), 1)) // SEL_BLOCK
    forced = (jrow == 0) | (jrow == cur) | (jrow == cur - 1)
    keys_t = []
    for g in range(G):
        psum = p_cmp[R * g]
        for r in range(1, R):
            psum = psum + p_cmp[R * g + r]
        imp_t = lax.dot_general(selT_ref[...], psum.astype(bf16), (((1,), (1,)), ((), ())),
                                preferred_element_type=f32)
        p_lo = (psum - psum.astype(bf16).astype(f32)).astype(bf16)
        imp_t = imp_t + lax.dot_general(selT_ref[...], p_lo, (((1,), (1,)), ((), ())),
                                        preferred_element_type=f32)
        score = jnp.where(forced, FORCE_SCORE, jnp.where(jrow <= cur, imp_t, -1.0))
        keys_t.append(jnp.where(score < 0.0, -1, pltpu.bitcast(score, jnp.int32)))
        rank_scr[g] = keys_t[g]

    def rank_rows(i8, cnts):
        base = pl.multiple_of(i8 * 8, 8)
        out = []
        for g in range(G):
            rows8 = rank_scr[g, pl.ds(base, 8), :]
            cnt = cnts[g]
            for k in range(8):
                si = rows8[k:k + 1, :]
                ahead = jnp.where(jrow > base + k, si + 1, si) > keys_t[g]
                cnt = cnt + jnp.where(ahead, 1.0, 0.0)
            out.append(cnt)
        return tuple(out)

    n_rows8 = jnp.minimum((t0 + tq - 1) // SEL_BLOCK // 8 + 1, ns // 8)
    cnts = lax.fori_loop(0, n_rows8, rank_rows, tuple(jnp.zeros((ns, tq), f32) for _ in range(G)))
    aux_sel = []
    for g in range(G):
        bias_t = jnp.where(cnts[g] < float(SEL_TOPK), 0.0, NEG_INF)
        aux_sel.append(aux0 + jnp.concatenate([bias_t, jnp.zeros((PAIR - ns, tq), f32)], axis=0).T)
    qs = augment(q8, aux_sel)

    def key_rows(kt):
        return pl.ds(pl.multiple_of(kt * tk, tk), tk)

    def scores(kt, dst):
        rows_ = key_rows(kt)
        dst[...] = _dot_nt(qs, jnp.concatenate([ks_ref[0, rows_, :], auxk_ref[rows_, :]], axis=1))

    def attend(src, kt, diag):
        s8 = src[...]
        if diag:
            kpos = (kt * tk + lax.broadcasted_iota(jnp.int32, (1, tk), 1)).astype(f32)
            s8 = jnp.where((tok >= kpos)[None], per_head(s8), NEG_INF).reshape(H * tq, tk)
            m_new = jnp.max(s8, axis=-1, keepdims=True)
            p = jnp.exp(s8 - m_new).astype(bf16)
            acc_scr[...] = weighted_values(p, vs_ref[0, key_rows(kt), :])
        else:
            m_prev = m_scr[...][:, 0:1]
            m_new = jnp.maximum(m_prev, jnp.max(s8, axis=-1, keepdims=True))
            p = jnp.exp(s8 - m_new).astype(bf16)
            acc_scr[...] = (jnp.exp(m_prev - m_new) * acc_scr[...]
                            + weighted_values(p, vs_ref[0, key_rows(kt), :]))
        m_scr[...] = jnp.broadcast_to(m_new, m_scr.shape)

    n_full = t0 // tk
    scores(n_full, sa_scr)

    kpos = w0 + lax.broadcasted_iota(jnp.int32, (1, WINDOW + tq), 1)
    dist_w = tok - kpos.astype(f32)
    vis_w = jnp.abs(dist_w - (WINDOW - 1) / 2.0) < WINDOW / 2.0
    s = jnp.where(vis_w[None], sw, NEG_INF)
    e = jnp.exp(s - jnp.max(s, axis=-1, keepdims=True)).astype(bf16).reshape(H * tq, WINDOW + tq)
    acc_win = weighted_values(e, vw_ref[0, wrows, :])

    scores(0, sb_scr)
    attend(sa_scr, n_full, True)

    def two_tiles(j, carry):
        scores(2 * j + 1, sa_scr)
        attend(sb_scr, 2 * j, False)
        scores(2 * j + 2, sb_scr)
        attend(sa_scr, 2 * j + 1, False)
        return carry

    lax.fori_loop(0, n_full // 2, two_tiles, 0)

    @pl.when(n_full % 2 == 1)
    def _():
        attend(sb_scr, n_full - 1, False)

    o_sel = split_sum(acc_scr[...])
    o_win = split_sum(acc_win)

    gate = gate_ref[0]
    o_cmp, o_sel, o_win = per_head(o_cmp), per_head(o_sel), per_head(o_win)
    for r in range(R):
        pair = []
        for g in range(G):
            h = R * g + r
            pair.append(gate[:, 3 * h:3 * h + 1] * o_cmp[h] + gate[:, 3 * h + 1:3 * h + 2] * o_sel[h]
                        + gate[:, 3 * h + 2:3 * h + 3] * o_win[h])
        o_ref[0, :, r * PAIR:(r + 1) * PAIR] = jnp.where(first, pair[0], pair[1])


def _nsa(q, kc, vc, ks, vs, kw, vw, gates, sel_t, tq, tk):
    B, H, S, _ = q.shape
    nc, ns = S // CMP_STRIDE, S // SEL_BLOCK
    aux_k = _alibi_key_columns(np.arange(S), ns, onehot=True)
    aux_c = _alibi_key_columns(np.arange(nc) * CMP_STRIDE + (CMP_LEN - 1), ns, onehot=False)
    full = lambda n: pl.BlockSpec((1, n, PAIR), lambda b, i: (b, 0, 0))
    const = lambda a: pl.BlockSpec(a.shape, lambda b, i: (0, 0))
    body = functools.partial(_nsa_body, tq=tq, tk=tk, seq=S)
    return pl.pallas_call(
        body,
        grid=(B, S // tq),
        in_specs=[pl.BlockSpec((1, H, tq, PAIR), lambda b, i: (b, 0, i, 0)),
                  full(nc), full(nc), full(S), full(S), full(S), full(S),
                  pl.BlockSpec((1, tq, 128), lambda b, i: (b, i, 0)),
                  const(sel_t), const(aux_k), const(aux_c)],
        out_specs=pl.BlockSpec((1, tq, NSA_GROUP * PAIR), lambda b, i: (b, i, 0)),
        out_shape=jax.ShapeDtypeStruct((B, S, NSA_GROUP * PAIR), f32),
        scratch_shapes=[pltpu.VMEM((H * tq, PAIR), f32)] * 2 + [pltpu.VMEM((H * tq, tk), f32)] * 2
        + [pltpu.VMEM((NSA_KV_HEADS, ns, tq), jnp.int32)],
        compiler_params=pltpu.CompilerParams(
            dimension_semantics=("arbitrary", "arbitrary"), vmem_limit_bytes=VMEM_LIMIT),
        name="nsa",
    )(q, kc, vc, ks, vs, kw, vw, gates, sel_t, aux_k, aux_c)


def _rwkv_body(r_s, lw_s, k_s, v_s, kk_s, b_s, g_s, bv_s, lnw_ref, lnb_ref, y_ref, s_scr, *, tt):
    C = CHUNK
    n_pairs = RWKV_WIDTH // PAIR

    @pl.when(pl.program_id(1) == 0)
    def _():
        s_scr[...] = jnp.zeros(s_scr.shape, f32)

    ri = lax.broadcasted_iota(jnp.int32, (C, C), 0)
    ci = lax.broadcasted_iota(jnp.int32, (C, C), 1)
    cum_mat = jnp.where(ri >= ci, 1.0, 0.0).astype(bf16)
    ri2 = lax.broadcasted_iota(jnp.int32, (C, 2 * C), 0)
    ci2 = lax.broadcasted_iota(jnp.int32, (C, 2 * C), 1) % C
    tri2_incl = ri2 >= ci2
    tri2_strict = ri2 > ci2
    lane = lax.broadcasted_iota(jnp.int32, (1, PAIR), 1)
    first = lane < HEAD_DIM
    blockdiag = (lax.broadcasted_iota(jnp.int32, (PAIR, PAIR), 0) // HEAD_DIM
                 == lax.broadcasted_iota(jnp.int32, (PAIR, PAIR), 1) // HEAD_DIM)
    head_mean = jnp.where(blockdiag, 1.0, 0.0).astype(bf16)

    def swap_heads(x):
        return pltpu.roll(x, HEAD_DIM, axis=1)

    def by_head_rows(x):
        return jnp.concatenate([jnp.where(first, x, 0.0), jnp.where(first, 0.0, x)], axis=0)

    def block_diag(x2):
        zero = jnp.zeros((x2.shape[0], PAIR), x2.dtype)
        return jnp.concatenate([jnp.concatenate([x2[:, :PAIR], zero], axis=1),
                                jnp.concatenate([zero, x2[:, PAIR:]], axis=1)], axis=0)

    def group(gi):
        units = []
        for cc in range(SCAN_GROUP):
            rows = pl.ds((gi * SCAN_GROUP + cc) * C, C)
            r_, lw_, k_, v_, kk_, b_ = (s[0, rows, :] for s in (r_s, lw_s, k_s, v_s, kk_s, b_s))
            cum = _split3_dot(cum_mat, lw_)
            cum_end = cum[C - 1:C, :]
            e_neg = jnp.exp(-cum)
            e_end = jnp.exp(cum_end - cum)
            a_t = -kk_ * jnp.exp(cum - lw_)
            r_t = r_ * jnp.exp(cum)
            b_t = b_ * e_neg
            k_t = k_ * e_neg
            b_h = b_ * e_end
            k_h = k_ * e_end
            w_end = jnp.exp(cum_end)
            for p in range(n_pairs):
                pc = slice(p * PAIR, (p + 1) * PAIR)
                units.append(dict(p=p, rows=rows, pc=pc, a_t=a_t[:, pc],
                                  r_t=r_t[:, pc], b_t=b_t[:, pc], k_t=k_t[:, pc], b_h=b_h[:, pc], k_h=k_h[:, pc],
                                  v=v_[:, pc], w_end=w_end[:, pc]))

        for u in units:
            lhs = jnp.concatenate([u["a_t"], u["r_t"]], axis=0)
            u["ab"] = _dot_nt(lhs, by_head_rows(u["b_t"]))
            u["ak"] = _dot_nt(lhs, by_head_rows(u["k_t"]))
        for u in units:
            ab, ak = u["ab"], u["ak"]
            u["a"] = jnp.where(tri2_strict, ab[:C], 0.0)
            u["a_rb"] = jnp.where(tri2_incl, ab[C:], 0.0)
            u["akrk"] = jnp.concatenate([jnp.where(tri2_strict, ak[:C], 0.0), jnp.where(tri2_incl, ak[C:], 0.0)],
                                        axis=0)
        for u in units:
            v_sw = swap_heads(u["v"]).astype(bf16)
            x1 = _dot(u["akrk"], block_diag(jnp.concatenate([v_sw, v_sw], axis=1)))
            u["x"] = jnp.concatenate([jnp.where(first, u["a_t"], x1[:C, :PAIR]),
                                      jnp.where(first, x1[:C, PAIR:], u["a_t"])], axis=1)
            u["arkv"] = x1[C:]
            u["pows"] = [u["a"], _dot(u["a"], by_head_rows(u["a"]))]
        for j in range(6):
            for u in units:
                u["x"] = u["x"] + _dot(u["pows"][j], block_diag(u["x"].astype(bf16)))
            if j + 2 <= 5:
                for u in units:
                    u["pows"].append(_dot(u["pows"][j + 1], by_head_rows(u["pows"][j + 1])))
        for u in units:
            u["ex"] = _dot(u["a_rb"], block_diag(u["x"].astype(bf16)))
        for u in units:
            x0, x1 = u["x"][:, :PAIR], u["x"][:, PAIR:]
            ex0, ex1 = u["ex"][:, :PAIR], u["ex"][:, PAIR:]
            ua = jnp.where(first, x0, x1)
            w2 = swap_heads(jnp.where(first, x1, x0))
            u["rq"] = u["r_t"] + jnp.where(first, ex0, ex1)
            u["yin"] = swap_heads(jnp.where(first, u["arkv"][:, PAIR:] + ex1, u["arkv"][:, :PAIR] + ex0))
            u["gmat"] = jnp.where(blockdiag, _dot_tn(ua, u["b_h"]), 0.0)
            u["qmat"] = jnp.where(blockdiag, _dot_tn(jnp.concatenate([w2, u["v"]], axis=0),
                                                     jnp.concatenate([u["b_h"], u["k_h"]], axis=0)), 0.0)
        state = [s_scr[p] for p in range(n_pairs)]
        for u in units:
            s0 = state[u["p"]]
            u["y"] = _dot_nt(u["rq"], s0) + u["yin"]
            state[u["p"]] = s0 * u["w_end"] + _dot(s0, u["gmat"]) + u["qmat"]
        for p in range(n_pairs):
            s_scr[p] = state[p]
        for u in units:
            y = u["y"]
            mu = jnp.dot(y.astype(bf16), head_mean, preferred_element_type=f32) * (1.0 / HEAD_DIM)
            u["d"] = y - mu
        for u in units:
            d = u["d"]
            var = jnp.dot((d * d).astype(bf16), head_mean, preferred_element_type=f32) * (1.0 / HEAD_DIM)
            yn = d * lax.rsqrt(var + GN_EPS) * lnw_ref[:, u["pc"]] + lnb_ref[:, u["pc"]]
            y_ref[0, u["rows"], u["pc"]] = (yn + bv_s[0, u["rows"], u["pc"]]) * g_s[0, u["rows"], u["pc"]]

    for gi in range(tt // (C * SCAN_GROUP)):
        group(gi)


def _rwkv(r, lw, k, v, kk, b, g, bv, lnw, lnb, tt):
    B, S, W = r.shape
    tok = pl.BlockSpec((1, tt, W), lambda bb, i: (bb, i, 0))
    const = pl.BlockSpec((1, W), lambda bb, i: (0, 0))
    return pl.pallas_call(
        functools.partial(_rwkv_body, tt=tt),
        grid=(B, S // tt),
        in_specs=[tok] * 8 + [const, const],
        out_specs=tok,
        out_shape=jax.ShapeDtypeStruct((B, S, W), f32),
        scratch_shapes=[pltpu.VMEM((W // PAIR, PAIR, PAIR), f32)],
        compiler_params=pltpu.CompilerParams(
            dimension_semantics=("arbitrary", "arbitrary"), vmem_limit_bytes=VMEM_LIMIT),
        name="rwkv",
    )(r, lw, k, v, kk, b, g, bv, lnw, lnb)


def _post_body(x_ref, yn_ref, yr_ref, p_ref, wo_ref, gpost_ref, gpre_ref, gmlp_ref, wup_ref, wdn_ref,
               wpg_ref, wple_ref, o_ref):
    y = jnp.concatenate([yn_ref[0], yr_ref[0]], axis=1).astype(bf16)
    mix = jnp.dot(y, wo_ref[...], preferred_element_type=f32)
    x1 = x_ref[0] + _rms(mix, gpost_ref[...])
    h = _rms(x1, gpre_ref[...]).astype(bf16)
    acc = None
    for c in range(D_FF // D_MODEL):
        cs = slice(c * D_MODEL, (c + 1) * D_MODEL)
        u = jnp.dot(h, wup_ref[:, cs], preferred_element_type=f32)
        u = jnp.square(jnp.maximum(u, 0.0)).astype(bf16)
        part = jnp.dot(u, wdn_ref[cs, :], preferred_element_type=f32)
        acc = part if acc is None else acc + part
    x2 = x1 + _rms(acc, gmlp_ref[...])
    gate = _sigmoid(jnp.dot(x2.astype(bf16), wpg_ref[...], preferred_element_type=f32))
    o_ref[0] = x2 + gate * jnp.dot(p_ref[0].astype(bf16), wple_ref[...], preferred_element_type=f32)


def _post(x, yn, yr, p, weights, tm):
    B, S, D = x.shape
    tok = lambda w: pl.BlockSpec((1, tm, w), lambda b, i: (b, i, 0))
    const = lambda a: pl.BlockSpec(a.shape, lambda b, i: (0,) * a.ndim, pipeline_mode=pl.Buffered(1))
    return pl.pallas_call(
        _post_body,
        grid=(B, S // tm),
        in_specs=[tok(D), tok(512), tok(512), tok(PLE_DIM)] + [const(a) for a in weights],
        out_specs=tok(D),
        out_shape=jax.ShapeDtypeStruct((B, S, D), f32),
        compiler_params=pltpu.CompilerParams(
            dimension_semantics=("arbitrary", "arbitrary"), vmem_limit_bytes=VMEM_LIMIT),
        name="post",
    )(x, yn, yr, p, *weights)


def _pack_inproj(w_in, gate_bias):
    wg = jnp.pad(w_in[:, 1280:1304], ((0, 0), (0, 128 - 24)))
    wcat = jnp.concatenate([w_in[:, :1280], wg, w_in[:, 1304:]], axis=1).astype(bf16)
    return wcat, jnp.pad(gate_bias, (0, 128 - 24)).reshape(1, 128)


def _pack_compress(pe, w1, b1, w2):
    eye2 = jnp.eye(NSA_KV_HEADS, dtype=f32)
    w1r = w1.reshape(CMP_LEN, HEAD_DIM, CMP_HIDDEN)
    halves = []
    for part in (w1r[:CMP_STRIDE], w1r[CMP_STRIDE:]):
        halves.append(jnp.einsum("jdc,gh->jgdhc", part, eye2).reshape(CMP_STRIDE * PAIR, 2 * CMP_HIDDEN))
    w = jnp.stack(halves).astype(bf16)
    per = jnp.broadcast_to(pe.reshape(2, CMP_STRIDE, 1, HEAD_DIM), (2, CMP_STRIDE, NSA_KV_HEADS, HEAD_DIM))
    per = per.reshape(2, CMP_STRIDE * PAIR)
    b1p = jnp.tile(b1, NSA_KV_HEADS).reshape(1, 2 * CMP_HIDDEN)
    w2p = jnp.einsum("cd,gh->gchd", w2, eye2).reshape(2 * CMP_HIDDEN, PAIR).astype(bf16)
    return per, w, b1p, w2p


def _sel_map_t(seq):
    nc, ns = seq // CMP_STRIDE, seq // SEL_BLOCK
    c0 = np.arange(nc) * CMP_STRIDE
    s0 = np.arange(ns) * SEL_BLOCK
    ov = (np.minimum(c0[:, None] + CMP_LEN - 1, s0[None, :] + SEL_BLOCK - 1)
          - np.maximum(c0[:, None], s0[None, :]) + 1)
    m = np.clip(ov, 0, None).astype(np.float32) / CMP_STRIDE
    m[nc - 1] = 0.0
    return jnp.asarray(m.T, dtype=bf16)


def _row(a):
    return a.reshape(1, -1)


def _mixers(x, g_mix_pre, w_in, nsa_gate_bias, cmp_k, cmp_v, shift_mu, w0, w_lora_up, a0, a_lora_up, g_lora_up,
            k_k, k_a, r_k, lnx_w, lnx_b):
    S = x.shape[1]
    bd = jnp.asarray(np.kron(np.eye(RWKV_WIDTH // HEAD_DIM), np.ones((HEAD_DIM, HEAD_DIM))), dtype=bf16)
    wcat, gbias = _pack_inproj(w_in, nsa_gate_bias)
    wl = jnp.concatenate([w_lora_up, jnp.zeros((ICLR_LORA, RWKV_WIDTH), f32)], axis=0).astype(bf16)
    al = jnp.concatenate([jnp.zeros((DECAY_LORA, RWKV_WIDTH), f32), a_lora_up], axis=0).astype(bf16)
    rwkv_params = (_row(shift_mu), _row(w0), wl, _row(a0), al, g_lora_up.astype(bf16),
                   _row(k_k), _row(k_a), _row(r_k), bd)
    q, kc, vc, ks, vs, kw, vw, gates, r, lw, k2, v, kk, b, g, bv = _inproj(
        x, _row(g_mix_pre), wcat, gbias, rwkv_params, tm=INPROJ_TM)
    kcmp, vcmp = _compress(kc, vc, _pack_compress(*cmp_k), _pack_compress(*cmp_v))
    y_nsa = _nsa(q, kcmp, vcmp, ks, vs, kw, vw, gates, _sel_map_t(S), tq=NSA_TQ, tk=NSA_TK)
    y_rwkv = _rwkv(r, lw, k2, v, kk, b, g, bv, _row(lnx_w), _row(lnx_b), tt=RWKV_TT)
    return r, y_nsa, y_rwkv


def kernel(x, p, g_mix_pre, g_mix_post, g_mlp_pre, g_mlp_post, w_in, nsa_gate_bias, cmp_pe_k, cmp_k_w1, cmp_k_b1, cmp_k_w2, cmp_pe_v, cmp_v_w1, cmp_v_b1, cmp_v_w2, shift_mu, w0, w_lora_up, a0, a_lora_up, g_lora_up, k_k, k_a, r_k, lnx_w, lnx_b, w_out, w_up, w_down, w_ple, w_ple_gate):
    D = x.shape[-1]
    for i in range(p.shape[0]):
        _, y_nsa, y_rwkv = _mixers(
            x, g_mix_pre[i], w_in[i], nsa_gate_bias[i],
            (cmp_pe_k[i], cmp_k_w1[i], cmp_k_b1[i], cmp_k_w2[i]), (cmp_pe_v[i], cmp_v_w1[i], cmp_v_b1[i], cmp_v_w2[i]),
            shift_mu[i], w0[i], w_lora_up[i], a0[i], a_lora_up[i], g_lora_up[i], k_k[i], k_a[i], r_k[i],
            lnx_w[i], lnx_b[i])
        wo_nsa = w_out[i][:512].reshape(NSA_KV_HEADS, NSA_GROUP, HEAD_DIM, D).transpose(1, 0, 2, 3).reshape(512, D)
        wo = jnp.concatenate([wo_nsa, w_out[i][512:]], axis=0).astype(bf16)
        weights = (wo, _row(g_mix_post[i]), _row(g_mlp_pre[i]), _row(g_mlp_post[i]), w_up[i].astype(bf16),
                   w_down[i].astype(bf16), w_ple_gate[i].astype(bf16), w_ple[i].astype(bf16))
        x = _post(x, y_nsa, y_rwkv, p[i], weights, tm=POST_TM)
    return x
```

```python
import functools

import jax
import jax.numpy as jnp
import numpy as np
from jax import lax
from jax.experimental import pallas as pl
from jax.experimental.pallas import tpu as pltpu

f32 = jnp.float32
bf16 = jnp.bfloat16

D_MODEL = 1024
HEAD_DIM = 64
NSA_HEADS = 8
NSA_KV_HEADS = 2
NSA_GROUP = NSA_HEADS // NSA_KV_HEADS
CMP_LEN = 32
CMP_STRIDE = 16
CMP_HIDDEN = 2 * HEAD_DIM
SEL_BLOCK = 64
SEL_TOPK = 16
WINDOW = 512
RWKV_WIDTH = 512
RWKV_COLS = 1792
DECAY_LORA = 64
ICLR_LORA = 64
D_FF = 4 * D_MODEL
PLE_DIM = 256
NORM_EPS = 1e-6
GN_EPS = 64e-5
NEG_INF = -1e30
FORCE_SCORE = 1e4

PAIR = 2 * HEAD_DIM
CHUNK = 64
SCAN_GROUP = 4

INPROJ_TM = 512
NSA_TQ = 256
NSA_TK = 512
RWKV_TT = CHUNK * SCAN_GROUP
POST_TM = 512
VMEM_LIMIT = 56 * 1024 * 1024

_Q0, _KV0, _GATE0, _RW0, _WCOLS = 0, 512, 1280, 1408, 3200


def _dot(a, b):
    return jnp.dot(a.astype(bf16), b.astype(bf16), preferred_element_type=f32)


def _dot_nt(a, b):
    return lax.dot_general(a.astype(bf16), b.astype(bf16), (((1,), (1,)), ((), ())), preferred_element_type=f32)


def _dot_tn(a, b):
    return lax.dot_general(a.astype(bf16), b.astype(bf16), (((0,), (0,)), ((), ())), preferred_element_type=f32)


def _split3_dot(w, x):
    hi = x.astype(bf16)
    r1 = x - hi.astype(f32)
    mid = r1.astype(bf16)
    lo = (r1 - mid.astype(f32)).astype(bf16)
    return (jnp.dot(w, hi, preferred_element_type=f32) + jnp.dot(w, mid, preferred_element_type=f32)
            + jnp.dot(w, lo, preferred_element_type=f32))


def _rms(x, g):
    ms = jnp.mean(x * x, axis=-1, keepdims=True)
    return x * lax.rsqrt(ms + NORM_EPS) * g


def _sigmoid(x):
    return 1.0 / (1.0 + jnp.exp(-x))


def _inproj_body(x_ref, g_ref, w_ref, gb_ref, mu_ref, w0_ref, wl_ref, a0_ref, al_ref, gl_ref, kkw_ref, ka_ref,
                 rk_ref, bd_ref, q_ref, kc_ref, vc_ref, ks_ref, vs_ref, kw_ref, vw_ref, gate_ref,
                 r_ref, lw_ref, k_ref, v_ref, kk_ref, b_ref, g_out_ref, bv_ref, zlast_scr):
    @pl.when(pl.program_id(1) == 0)
    def _():
        zlast_scr[...] = jnp.zeros(zlast_scr.shape, f32)

    h = _rms(x_ref[0], g_ref[...]).astype(bf16)
    z = jnp.dot(h, w_ref[:, _RW0:_WCOLS], preferred_element_type=f32)
    tm = z.shape[0]
    prev_row = zlast_scr[7:8, :]
    zlast_scr[...] = z[tm - 8:tm, :]
    q = jnp.dot(h, w_ref[:, _Q0:_KV0], preferred_element_type=f32) * (HEAD_DIM ** -0.5)
    first = lax.broadcasted_iota(jnp.int32, (1, PAIR), 1) < HEAD_DIM
    for hd in range(NSA_HEADS):
        two = q[:, (hd // 2) * PAIR:(hd // 2 + 1) * PAIR]
        if (hd % 2) != (hd // NSA_GROUP):
            two = pltpu.roll(two, HEAD_DIM, axis=1)
        keep = first if hd < NSA_GROUP else jnp.logical_not(first)
        q_ref[0, hd] = jnp.where(keep, two, 0.0).astype(bf16)
    kv = jnp.dot(h, w_ref[:, _KV0:_GATE0], preferred_element_type=f32)
    kc_ref[0] = kv[:, 0:128]
    vc_ref[0] = kv[:, 128:256]
    ks_ref[0] = kv[:, 256:384].astype(bf16)
    vs_ref[0] = kv[:, 384:512].astype(bf16)
    kw_ref[0] = kv[:, 512:640].astype(bf16)
    vw_ref[0] = kv[:, 640:768].astype(bf16)
    gl = jnp.dot(h, w_ref[:, _GATE0:_RW0], preferred_element_type=f32)
    gate_ref[0] = _sigmoid(gl + gb_ref[...])

    row = lax.broadcasted_iota(jnp.int32, (tm, 1), 0)
    z_prev = jnp.where(row == 0, prev_row, pltpu.roll(z, 1, axis=0))
    zs = z + (z_prev - z) * mu_ref[...]
    r = zs[:, 0:512]
    k = zs[:, 512:1024]
    v = zs[:, 1024:1536]
    lora = zs[:, 1536:1664]
    gd = zs[:, 1664:1792]
    wlog = w0_ref[...] + _dot(jnp.tanh(lora), wl_ref[...])
    sp = jnp.maximum(-wlog, 0.0) + jnp.log(1.0 + jnp.exp(-jnp.abs(wlog)))
    a = _sigmoid(a0_ref[...] + _dot(lora, al_ref[...]))
    kk = k * kkw_ref[...]
    kk = kk * lax.rsqrt(jnp.maximum(_dot(kk * kk, bd_ref[...]), 1e-24))
    k2 = k * (1.0 + (a - 1.0) * ka_ref[...])
    r_ref[0] = r
    lw_ref[0] = -jnp.exp(-sp - 0.5)
    k_ref[0] = k2
    v_ref[0] = v
    kk_ref[0] = kk
    b_ref[0] = kk * a
    g_out_ref[0] = _dot(_sigmoid(gd), gl_ref[...])
    bv_ref[0] = _dot(r * k2 * rk_ref[...], bd_ref[...]) * v


def _inproj(x, g, wcat, gbias, rwkv_params, tm):
    B, S, D = x.shape
    tok = lambda w: pl.BlockSpec((1, tm, w), lambda b, i: (b, i, 0))
    const = lambda a: pl.BlockSpec(a.shape, lambda b, i: (0,) * a.ndim)
    consts = (g, wcat, gbias) + tuple(rwkv_params)
    return pl.pallas_call(
        _inproj_body,
        grid=(B, S // tm),
        in_specs=[tok(D)] + [const(a) for a in consts],
        out_specs=[pl.BlockSpec((1, NSA_HEADS, tm, PAIR), lambda b, i: (b, 0, i, 0))]
        + [tok(128)] * 7 + [tok(RWKV_WIDTH)] * 8,
        out_shape=[jax.ShapeDtypeStruct((B, NSA_HEADS, S, PAIR), bf16),
                   jax.ShapeDtypeStruct((B, S, 128), f32), jax.ShapeDtypeStruct((B, S, 128), f32)]
        + [jax.ShapeDtypeStruct((B, S, 128), bf16)] * 4
        + [jax.ShapeDtypeStruct((B, S, 128), f32)] + [jax.ShapeDtypeStruct((B, S, RWKV_WIDTH), f32)] * 8,
        scratch_shapes=[pltpu.VMEM((8, RWKV_COLS), f32)],
        compiler_params=pltpu.CompilerParams(
            dimension_semantics=("arbitrary", "arbitrary"), vmem_limit_bytes=VMEM_LIMIT),
        name="inproj",
    )(x, *consts)


def _gelu_tanh(x):
    return x * (0.5 * (1.0 + jnp.tanh(np.sqrt(2.0 / np.pi) * (x + 0.044715 * (x * x * x)))))


def _compress_one(x_ref, pe_ref, w_ref, b1_ref, w2_ref):
    n = x_ref.shape[1] // CMP_STRIDE
    x = jnp.concatenate([x_ref[0, pl.ds(j, n, stride=CMP_STRIDE), :] for j in range(CMP_STRIDE)], axis=1)
    lo = _dot(x + pe_ref[0:1, :], w_ref[0])
    hi = _dot(x + pe_ref[1:2, :], w_ref[1])
    pre = lo + pltpu.roll(hi, n - 1, axis=0) + b1_ref[...]
    return _dot(_gelu_tanh(pre), w2_ref[...])


def _compress_body(xk_ref, xv_ref, pek_ref, wk_ref, bk_ref, w2k_ref, pev_ref, wv_ref, bv_ref, w2v_ref,
                   kc_ref, vc_ref):
    kc_ref[0] = _compress_one(xk_ref, pek_ref, wk_ref, bk_ref, w2k_ref).astype(bf16)
    vc_ref[0] = _compress_one(xv_ref, pev_ref, wv_ref, bv_ref, w2v_ref).astype(bf16)


def _compress(xk, xv, kparams, vparams):
    B, S, W = xk.shape
    NC = S // CMP_STRIDE
    const = lambda a: pl.BlockSpec(a.shape, lambda b: (0,) * a.ndim)
    seq = pl.BlockSpec((1, S, W), lambda b: (b, 0, 0))
    out = pl.BlockSpec((1, NC, PAIR), lambda b: (b, 0, 0))
    return pl.pallas_call(
        _compress_body,
        grid=(B,),
        in_specs=[seq, seq] + [const(a) for a in kparams] + [const(a) for a in vparams],
        out_specs=[out, out],
        out_shape=[jax.ShapeDtypeStruct((B, NC, PAIR), bf16)] * 2,
        compiler_params=pltpu.CompilerParams(dimension_semantics=("arbitrary",), vmem_limit_bytes=VMEM_LIMIT),
        name="compress",
    )(xk, xv, *kparams, *vparams)


def _alibi_key_columns(pos, ns, onehot):
    a = np.zeros((pos.shape[0], PAIR), np.float32)
    if onehot:
        a[np.arange(pos.shape[0]), pos // SEL_BLOCK] = 1.0
    a[:, ns] = -1.0
    a[:, ns + 1] = -1.0
    a[:, ns + 2] = pos // SEL_BLOCK
    a[:, ns + 3] = pos % SEL_BLOCK
    return jnp.asarray(a, dtype=bf16)


def _nsa_body(q_ref, kc_ref, vc_ref, ks_ref, vs_ref, kw_ref, vw_ref, gate_ref, selT_ref, auxk_ref, auxc_ref,
              o_ref, m_scr, acc_scr, sa_scr, sb_scr, rank_scr, *, tq, tk, seq):
    R, G, H = NSA_GROUP, NSA_KV_HEADS, NSA_HEADS
    t0 = pl.program_id(1) * tq
    nc = seq // CMP_STRIDE
    ns = seq // SEL_BLOCK
    gm = R * tq
    tok = (t0 + lax.broadcasted_iota(jnp.int32, (tq, 1), 0)).astype(f32)
    lane = lax.broadcasted_iota(jnp.int32, (1, PAIR), 1)
    first = lane < HEAD_DIM

    tl = t0 + lax.broadcasted_iota(jnp.int32, (8, tq), 1)
    rid = lax.broadcasted_iota(jnp.int32, (8, tq), 0)
    alibi_rows = jnp.where(rid == 0, ((tl // SEL_BLOCK) * SEL_BLOCK).astype(f32),
                           jnp.where(rid == 1, (tl % SEL_BLOCK).astype(f32),
                                     jnp.where(rid == 2, float(SEL_BLOCK), jnp.where(rid == 3, 1.0, 0.0))))
    aux0 = jnp.concatenate([jnp.zeros((ns, tq), f32), alibi_rows, jnp.zeros((PAIR - ns - 8, tq), f32)], axis=0).T

    def augment(q8, aux_by_group):
        aux8 = jnp.concatenate([aux_by_group[h // R] * jnp.where(lane < ns, 1.0, 2.0 ** (-(h + 1)))
                                for h in range(H)], axis=0)
        return jnp.concatenate([q8, aux8.astype(bf16)], axis=1)

    def per_head(a):
        return a.reshape(H, tq, a.shape[-1])

    def with_ones(v):
        one = jnp.ones((), v.dtype)
        return jnp.where(first, v, one), jnp.where(first, one, v)

    def weighted_values(p, v):
        v0, v1 = with_ones(v)
        return jnp.concatenate([jnp.dot(p[:gm], v0, preferred_element_type=f32),
                                jnp.dot(p[gm:], v1, preferred_element_type=f32)], axis=0)

    def split_sum(acc):
        top, bot = acc[:gm], acc[gm:]
        return jnp.concatenate([top * (1.0 / top[:, HEAD_DIM:HEAD_DIM + 1]), bot * (1.0 / bot[:, 0:1])], axis=0)

    q8 = q_ref[0].reshape(H * tq, PAIR)
    qa = augment(q8, [aux0, aux0])

    kc_aug = jnp.concatenate([kc_ref[0], auxc_ref[...]], axis=1)
    sc = per_head(_dot_nt(qa, kc_aug))
    w0 = pl.multiple_of(jnp.maximum(t0 - WINDOW, 0), tq)
    wrows = pl.ds(w0, WINDOW + tq)
    kw_aug = jnp.concatenate([kw_ref[0, wrows, :], auxk_ref[wrows, :]], axis=1)
    sw = per_head(_dot_nt(qa, kw_aug))

    cend = (lax.broadcasted_iota(jnp.int32, (1, nc), 1) * CMP_STRIDE + (CMP_LEN - 1)).astype(f32)
    vis_c = tok >= cend
    any_c = (tok >= float(CMP_LEN - 1)).astype(f32)
    s = jnp.where(vis_c[None], sc, NEG_INF)
    e = jnp.exp(s - jnp.max(s, axis=-1, keepdims=True))
    p_cmp = e * (any_c[None] / jnp.sum(e, axis=-1, keepdims=True))
    o_cmp = _dot(p_cmp.reshape(H * tq, nc), vc_ref[0])

    jrow = lax.broadcasted_iota(jnp.int32, (ns, tq), 0)
    cur = (t0 + lax.broadcasted_iota(jnp.int32, (ns, tq), 1)) // SEL_BLOCK
    forced = (jrow == 0) | (jrow == cur) | (jrow == cur - 1)
    keys_t = []
    for g in range(G):
        psum = p_cmp[R * g]
        for r in range(1, R):
            psum = psum + p_cmp[R * g + r]
        imp_t = lax.dot_general(selT_ref[...], psum.astype(bf16), (((1,), (1,)), ((), ())),
                                preferred_element_type=f32)
        p_lo = (psum - psum.astype(bf16).astype(f32)).astype(bf16)
        imp_t = imp_t + lax.dot_general(selT_ref[...], p_lo, (((1,), (1,)), ((), ())),
                                        preferred_element_type=f32)
        score = jnp.where(forced, FORCE_SCORE, jnp.where(jrow <= cur, imp_t, -1.0))
        keys_t.append(jnp.where(score < 0.0, -1, pltpu.bitcast(score, jnp.int32)))
        rank_scr[g] = keys_t[g]

    def rank_rows(i8, cnts):
        base = pl.multiple_of(i8 * 8, 8)
        out = []
        for g in range(G):
            rows8 = rank_scr[g, pl.ds(base, 8), :]
            cnt = cnts[g]
            for k in range(8):
                si = rows8[k:k + 1, :]
                ahead = jnp.where(jrow > base + k, si + 1, si) > keys_t[g]
                cnt = cnt + jnp.where(ahead, 1.0, 0.0)
            out.append(cnt)
        return tuple(out)

    n_rows8 = jnp.minimum((t0 + tq - 1) // SEL_BLOCK // 8 + 1, ns // 8)
    cnts = lax.fori_loop(0, n_rows8, rank_rows, tuple(jnp.zeros((ns, tq), f32) for _ in range(G)))
    aux_sel = []
    for g in range(G):
        bias_t = jnp.where(cnts[g] < float(SEL_TOPK), 0.0, NEG_INF)
        aux_sel.append(aux0 + jnp.concatenate([bias_t, jnp.zeros((PAIR - ns, tq), f32)], axis=0).T)
    qs = augment(q8, aux_sel)

    def key_rows(kt):
        return pl.ds(pl.multiple_of(kt * tk, tk), tk)

    def scores(kt, dst):
        rows_ = key_rows(kt)
        dst[...] = _dot_nt(qs, jnp.concatenate([ks_ref[0, rows_, :], auxk_ref[rows_, :]], axis=1))

    def attend(src, kt, diag):
        s8 = src[...]
        if diag:
            kpos = (kt * tk + lax.broadcasted_iota(jnp.int32, (1, tk), 1)).astype(f32)
            s8 = jnp.where((tok >= kpos)[None], per_head(s8), NEG_INF).reshape(H * tq, tk)
            m_new = jnp.max(s8, axis=-1, keepdims=True)
            p = jnp.exp(s8 - m_new).astype(bf16)
            acc_scr[...] = weighted_values(p, vs_ref[0, key_rows(kt), :])
        else:
            m_prev = m_scr[...][:, 0:1]
            m_new = jnp.maximum(m_prev, jnp.max(s8, axis=-1, keepdims=True))
            p = jnp.exp(s8 - m_new).astype(bf16)
            acc_scr[...] = (jnp.exp(m_prev - m_new) * acc_scr[...]
                            + weighted_values(p, vs_ref[0, key_rows(kt), :]))
        m_scr[...] = jnp.broadcast_to(m_new, m_scr.shape)

    n_full = t0 // tk
    scores(n_full, sa_scr)

    kpos = w0 + lax.broadcasted_iota(jnp.int32, (1, WINDOW + tq), 1)
    dist_w = tok - kpos.astype(f32)
    vis_w = jnp.abs(dist_w - (WINDOW - 1) / 2.0) < WINDOW / 2.0
    s = jnp.where(vis_w[None], sw, NEG_INF)
    e = jnp.exp(s - jnp.max(s, axis=-1, keepdims=True)).astype(bf16).reshape(H * tq, WINDOW + tq)
    acc_win = weighted_values(e, vw_ref[0, wrows, :])

    scores(0, sb_scr)
    attend(sa_scr, n_full, True)

    def two_tiles(j, carry):
        scores(2 * j + 1, sa_scr)
        attend(sb_scr, 2 * j, False)
        scores(2 * j + 2, sb_scr)
        attend(sa_scr, 2 * j + 1, False)
        return carry

    lax.fori_loop(0, n_full // 2, two_tiles, 0)

    @pl.when(n_full % 2 == 1)
    def _():
        attend(sb_scr, n_full - 1, False)

    o_sel = split_sum(acc_scr[...])
    o_win = split_sum(acc_win)

    gate = gate_ref[0]
    o_cmp, o_sel, o_win = per_head(o_cmp), per_head(o_sel), per_head(o_win)
    for r in range(R):
        pair = []
        for g in range(G):
            h = R * g + r
            pair.append(gate[:, 3 * h:3 * h + 1] * o_cmp[h] + gate[:, 3 * h + 1:3 * h + 2] * o_sel[h]
                        + gate[:, 3 * h + 2:3 * h + 3] * o_win[h])
        o_ref[0, :, r * PAIR:(r + 1) * PAIR] = jnp.where(first, pair[0], pair[1])


def _nsa(q, kc, vc, ks, vs, kw, vw, gates, sel_t, tq, tk):
    B, H, S, _ = q.shape
    nc, ns = S // CMP_STRIDE, S // SEL_BLOCK
    aux_k = _alibi_key_columns(np.arange(S), ns, onehot=True)
    aux_c = _alibi_key_columns(np.arange(nc) * CMP_STRIDE + (CMP_LEN - 1), ns, onehot=False)
    full = lambda n: pl.BlockSpec((1, n, PAIR), lambda b, i: (b, 0, 0))
    const = lambda a: pl.BlockSpec(a.shape, lambda b, i: (0, 0))
    body = functools.partial(_nsa_body, tq=tq, tk=tk, seq=S)
    return pl.pallas_call(
        body,
        grid=(B, S // tq),
        in_specs=[pl.BlockSpec((1, H, tq, PAIR), lambda b, i: (b, 0, i, 0)),
                  full(nc), full(nc), full(S), full(S), full(S), full(S),
                  pl.BlockSpec((1, tq, 128), lambda b, i: (b, i, 0)),
                  const(sel_t), const(aux_k), const(aux_c)],
        out_specs=pl.BlockSpec((1, tq, NSA_GROUP * PAIR), lambda b, i: (b, i, 0)),
        out_shape=jax.ShapeDtypeStruct((B, S, NSA_GROUP * PAIR), f32),
        scratch_shapes=[pltpu.VMEM((H * tq, PAIR), f32)] * 2 + [pltpu.VMEM((H * tq, tk), f32)] * 2
        + [pltpu.VMEM((NSA_KV_HEADS, ns, tq), jnp.int32)],
        compiler_params=pltpu.CompilerParams(
            dimension_semantics=("arbitrary", "arbitrary"), vmem_limit_bytes=VMEM_LIMIT),
        name="nsa",
    )(q, kc, vc, ks, vs, kw, vw, gates, sel_t, aux_k, aux_c)


def _rwkv_body(r_s, lw_s, k_s, v_s, kk_s, b_s, g_s, bv_s, lnw_ref, lnb_ref, y_ref, s_scr, *, tt):
    C = CHUNK
    n_pairs = RWKV_WIDTH // PAIR

    @pl.when(pl.program_id(1) == 0)
    def _():
        s_scr[...] = jnp.zeros(s_scr.shape, f32)

    ri = lax.broadcasted_iota(jnp.int32, (C, C), 0)
    ci = lax.broadcasted_iota(jnp.int32, (C, C), 1)
    cum_mat = jnp.where(ri >= ci, 1.0, 0.0).astype(bf16)
    ri2 = lax.broadcasted_iota(jnp.int32, (C, 2 * C), 0)
    ci2 = lax.broadcasted_iota(jnp.int32, (C, 2 * C), 1) % C
    tri2_incl = ri2 >= ci2
    tri2_strict = ri2 > ci2
    eye2 = jnp.where(ri2 == ci2, 1.0, 0.0)
    lane = lax.broadcasted_iota(jnp.int32, (1, PAIR), 1)
    first = lane < HEAD_DIM
    blockdiag = (lax.broadcasted_iota(jnp.int32, (PAIR, PAIR), 0) // HEAD_DIM
                 == lax.broadcasted_iota(jnp.int32, (PAIR, PAIR), 1) // HEAD_DIM)
    head_mean = jnp.where(blockdiag, 1.0, 0.0).astype(bf16)

    def swap_heads(x):
        return pltpu.roll(x, HEAD_DIM, axis=1)

    def by_head_rows(x):
        return jnp.concatenate([jnp.where(first, x, 0.0), jnp.where(first, 0.0, x)], axis=0)

    def block_diag(x2):
        zero = jnp.zeros((x2.shape[0], PAIR), x2.dtype)
        return jnp.concatenate([jnp.concatenate([x2[:, :PAIR], zero], axis=1),
                                jnp.concatenate([zero, x2[:, PAIR:]], axis=1)], axis=0)

    def group(gi):
        units = []
        for cc in range(SCAN_GROUP):
            rows = pl.ds((gi * SCAN_GROUP + cc) * C, C)
            r_, lw_, k_, v_, kk_, b_ = (s[0, rows, :] for s in (r_s, lw_s, k_s, v_s, kk_s, b_s))
            cum = _split3_dot(cum_mat, lw_)
            cum_end = cum[C - 1:C, :]
            e_neg = jnp.exp(-cum)
            e_end = jnp.exp(cum_end - cum)
            a_t = -kk_ * jnp.exp(cum - lw_)
            r_t = r_ * jnp.exp(cum)
            b_t = b_ * e_neg
            k_t = k_ * e_neg
            b_h = b_ * e_end
            k_h = k_ * e_end
            w_end = jnp.exp(cum_end)
            for p in range(n_pairs):
                pc = slice(p * PAIR, (p + 1) * PAIR)
                units.append(dict(p=p, rows=rows, pc=pc, a_t=a_t[:, pc],
                                  r_t=r_t[:, pc], b_t=b_t[:, pc], k_t=k_t[:, pc], b_h=b_h[:, pc], k_h=k_h[:, pc],
                                  v=v_[:, pc], w_end=w_end[:, pc]))

        for u in units:
            lhs = jnp.concatenate([u["a_t"], u["r_t"]], axis=0)
            abk = _dot_nt(lhs, jnp.concatenate([by_head_rows(u["b_t"]), by_head_rows(u["k_t"])], axis=0))
            u["ab"], u["ak"] = abk[:, :2 * C], abk[:, 2 * C:]
        for u in units:
            ab, ak = u["ab"], u["ak"]
            u["a"] = jnp.where(tri2_strict, ab[:C], 0.0)
            u["a_rb"] = jnp.where(tri2_incl, ab[C:], 0.0)
            u["akrk"] = jnp.concatenate([jnp.where(tri2_strict, ak[:C], 0.0), jnp.where(tri2_incl, ak[C:], 0.0)],
                                        axis=0)
        for u in units:
            v_sw = swap_heads(u["v"]).astype(bf16)
            x1 = _dot(u["akrk"], block_diag(jnp.concatenate([v_sw, v_sw], axis=1)))
            u["x"] = jnp.concatenate([jnp.where(first, u["a_t"], x1[:C, :PAIR]),
                                      jnp.where(first, x1[:C, PAIR:], u["a_t"])], axis=1)
            u["arkv"] = x1[C:]
            u["t"] = eye2 + u["a"]
            u["pw"] = _dot(u["a"], by_head_rows(u["a"]))
        for j in range(1, 5):
            for u in units:
                both = _dot(jnp.concatenate([u["pw"], u["t"]], axis=0), by_head_rows(u["pw"]))
                u["pw"] = both[:C]
                u["t"] = u["t"] + both[C:]
        for u in units:
            u["t"] = u["t"] + _dot(u["t"], by_head_rows(u["pw"]))
        for u in units:
            u["x"] = _dot(u["t"], block_diag(u["x"].astype(bf16)))
        for u in units:
            u["ex"] = _dot(u["a_rb"], block_diag(u["x"].astype(bf16)))
        for u in units:
            x0, x1 = u["x"][:, :PAIR], u["x"][:, PAIR:]
            ex0, ex1 = u["ex"][:, :PAIR], u["ex"][:, PAIR:]
            ua = jnp.where(first, x0, x1)
            w2 = swap_heads(jnp.where(first, x1, x0))
            u["rq"] = u["r_t"] + jnp.where(first, ex0, ex1)
            u["yin"] = swap_heads(jnp.where(first, u["arkv"][:, PAIR:] + ex1, u["arkv"][:, :PAIR] + ex0))
            u["gmat"] = jnp.where(blockdiag, _dot_tn(ua, u["b_h"]), 0.0)
            u["qmat"] = jnp.where(blockdiag, _dot_tn(jnp.concatenate([w2, u["v"]], axis=0),
                                                     jnp.concatenate([u["b_h"], u["k_h"]], axis=0)), 0.0)
        state = [s_scr[p] for p in range(n_pairs)]
        for u in units:
            s0 = state[u["p"]]
            u["y"] = _dot_nt(u["rq"], s0) + u["yin"]
            state[u["p"]] = s0 * u["w_end"] + _dot(s0, u["gmat"]) + u["qmat"]
        for p in range(n_pairs):
            s_scr[p] = state[p]
        y_all = jnp.concatenate([u["y"] for u in units], axis=0)
        d_all = y_all - jnp.dot(y_all.astype(bf16), head_mean, preferred_element_type=f32) * (1.0 / HEAD_DIM)
        var_all = jnp.dot((d_all * d_all).astype(bf16), head_mean, preferred_element_type=f32) * (1.0 / HEAD_DIM)
        yn_all = d_all * lax.rsqrt(var_all + GN_EPS)
        for n, u in enumerate(units):
            yn = yn_all[n * C:(n + 1) * C] * lnw_ref[:, u["pc"]] + lnb_ref[:, u["pc"]]
            y_ref[0, u["rows"], u["pc"]] = (yn + bv_s[0, u["rows"], u["pc"]]) * g_s[0, u["rows"], u["pc"]]

    for gi in range(tt // (C * SCAN_GROUP)):
        group(gi)


def _rwkv(r, lw, k, v, kk, b, g, bv, lnw, lnb, tt):
    B, S, W = r.shape
    tok = pl.BlockSpec((1, tt, W), lambda bb, i: (bb, i, 0))
    const = pl.BlockSpec((1, W), lambda bb, i: (0, 0))
    return pl.pallas_call(
        functools.partial(_rwkv_body, tt=tt),
        grid=(B, S // tt),
        in_specs=[tok] * 8 + [const, const],
        out_specs=tok,
        out_shape=jax.ShapeDtypeStruct((B, S, W), f32),
        scratch_shapes=[pltpu.VMEM((W // PAIR, PAIR, PAIR), f32)],
        compiler_params=pltpu.CompilerParams(
            dimension_semantics=("arbitrary", "arbitrary"), vmem_limit_bytes=VMEM_LIMIT),
        name="rwkv",
    )(r, lw, k, v, kk, b, g, bv, lnw, lnb)


def _post_body(x_ref, yn_ref, yr_ref, p_ref, wo_ref, gpost_ref, gpre_ref, gmlp_ref, wup_ref, wdn_ref,
               wpg_ref, wple_ref, o_ref):
    y = jnp.concatenate([yn_ref[0], yr_ref[0]], axis=1).astype(bf16)
    mix = jnp.dot(y, wo_ref[...], preferred_element_type=f32)
    x1 = x_ref[0] + _rms(mix, gpost_ref[...])
    h = _rms(x1, gpre_ref[...]).astype(bf16)
    acc = None
    for c in range(D_FF // D_MODEL):
        cs = slice(c * D_MODEL, (c + 1) * D_MODEL)
        u = jnp.dot(h, wup_ref[:, cs], preferred_element_type=f32)
        u = jnp.square(jnp.maximum(u, 0.0)).astype(bf16)
        part = jnp.dot(u, wdn_ref[cs, :], preferred_element_type=f32)
        acc = part if acc is None else acc + part
    x2 = x1 + _rms(acc, gmlp_ref[...])
    gate = _sigmoid(jnp.dot(x2.astype(bf16), wpg_ref[...], preferred_element_type=f32))
    o_ref[0] = x2 + gate * jnp.dot(p_ref[0].astype(bf16), wple_ref[...], preferred_element_type=f32)


def _post(x, yn, yr, p, weights, tm):
    B, S, D = x.shape
    tok = lambda w: pl.BlockSpec((1, tm, w), lambda b, i: (b, i, 0))
    const = lambda a: pl.BlockSpec(a.shape, lambda b, i: (0,) * a.ndim, pipeline_mode=pl.Buffered(1))
    return pl.pallas_call(
        _post_body,
        grid=(B, S // tm),
        in_specs=[tok(D), tok(512), tok(512), tok(PLE_DIM)] + [const(a) for a in weights],
        out_specs=tok(D),
        out_shape=jax.ShapeDtypeStruct((B, S, D), f32),
        compiler_params=pltpu.CompilerParams(
            dimension_semantics=("arbitrary", "arbitrary"), vmem_limit_bytes=VMEM_LIMIT),
        name="post",
    )(x, yn, yr, p, *weights)


def _pack_inproj(w_in, gate_bias):
    wg = jnp.pad(w_in[:, 1280:1304], ((0, 0), (0, 128 - 24)))
    wcat = jnp.concatenate([w_in[:, :1280], wg, w_in[:, 1304:]], axis=1).astype(bf16)
    return wcat, jnp.pad(gate_bias, (0, 128 - 24)).reshape(1, 128)


def _pack_compress(pe, w1, b1, w2):
    eye2 = jnp.eye(NSA_KV_HEADS, dtype=f32)
    w1r = w1.reshape(CMP_LEN, HEAD_DIM, CMP_HIDDEN)
    halves = []
    for part in (w1r[:CMP_STRIDE], w1r[CMP_STRIDE:]):
        halves.append(jnp.einsum("jdc,gh->jgdhc", part, eye2).reshape(CMP_STRIDE * PAIR, 2 * CMP_HIDDEN))
    w = jnp.stack(halves).astype(bf16)
    per = jnp.broadcast_to(pe.reshape(2, CMP_STRIDE, 1, HEAD_DIM), (2, CMP_STRIDE, NSA_KV_HEADS, HEAD_DIM))
    per = per.reshape(2, CMP_STRIDE * PAIR)
    b1p = jnp.tile(b1, NSA_KV_HEADS).reshape(1, 2 * CMP_HIDDEN)
    w2p = jnp.einsum("cd,gh->gchd", w2, eye2).reshape(2 * CMP_HIDDEN, PAIR).astype(bf16)
    return per, w, b1p, w2p


def _sel_map_t(seq):
    nc, ns = seq // CMP_STRIDE, seq // SEL_BLOCK
    c0 = np.arange(nc) * CMP_STRIDE
    s0 = np.arange(ns) * SEL_BLOCK
    ov = (np.minimum(c0[:, None] + CMP_LEN - 1, s0[None, :] + SEL_BLOCK - 1)
          - np.maximum(c0[:, None], s0[None, :]) + 1)
    m = np.clip(ov, 0, None).astype(np.float32) / CMP_STRIDE
    m[nc - 1] = 0.0
    return jnp.asarray(m.T, dtype=bf16)


def _row(a):
    return a.reshape(1, -1)


def _mixers(x, g_mix_pre, w_in, nsa_gate_bias, cmp_k, cmp_v, shift_mu, w0, w_lora_up, a0, a_lora_up, g_lora_up,
            k_k, k_a, r_k, lnx_w, lnx_b):
    S = x.shape[1]
    bd = jnp.asarray(np.kron(np.eye(RWKV_WIDTH // HEAD_DIM), np.ones((HEAD_DIM, HEAD_DIM))), dtype=bf16)
    wcat, gbias = _pack_inproj(w_in, nsa_gate_bias)
    wl = jnp.concatenate([w_lora_up, jnp.zeros((ICLR_LORA, RWKV_WIDTH), f32)], axis=0).astype(bf16)
    al = jnp.concatenate([jnp.zeros((DECAY_LORA, RWKV_WIDTH), f32), a_lora_up], axis=0).astype(bf16)
    rwkv_params = (_row(shift_mu), _row(w0), wl, _row(a0), al, g_lora_up.astype(bf16),
                   _row(k_k), _row(k_a), _row(r_k), bd)
    q, kc, vc, ks, vs, kw, vw, gates, r, lw, k2, v, kk, b, g, bv = _inproj(
        x, _row(g_mix_pre), wcat, gbias, rwkv_params, tm=INPROJ_TM)
    kcmp, vcmp = _compress(kc, vc, _pack_compress(*cmp_k), _pack_compress(*cmp_v))
    y_nsa = _nsa(q, kcmp, vcmp, ks, vs, kw, vw, gates, _sel_map_t(S), tq=NSA_TQ, tk=NSA_TK)
    y_rwkv = _rwkv(r, lw, k2, v, kk, b, g, bv, _row(lnx_w), _row(lnx_b), tt=RWKV_TT)
    return r, y_nsa, y_rwkv


def kernel(x, p, g_mix_pre, g_mix_post, g_mlp_pre, g_mlp_post, w_in, nsa_gate_bias, cmp_pe_k, cmp_k_w1, cmp_k_b1, cmp_k_w2, cmp_pe_v, cmp_v_w1, cmp_v_b1, cmp_v_w2, shift_mu, w0, w_lora_up, a0, a_lora_up, g_lora_up, k_k, k_a, r_k, lnx_w, lnx_b, w_out, w_up, w_down, w_ple, w_ple_gate):
    D = x.shape[-1]
    for i in range(p.shape[0]):
        _, y_nsa, y_rwkv = _mixers(
            x, g_mix_pre[i], w_in[i], nsa_gate_bias[i],
            (cmp_pe_k[i], cmp_k_w1[i], cmp_k_b1[i], cmp_k_w2[i]), (cmp_pe_v[i], cmp_v_w1[i], cmp_v_b1[i], cmp_v_w2[i]),
            shift_mu[i], w0[i], w_lora_up[i], a0[i], a_lora_up[i], g_lora_up[i], k_k[i], k_a[i], r_k[i],
            lnx_w[i], lnx_b[i])
        wo_nsa = w_out[i][:512].reshape(NSA_KV_HEADS, NSA_GROUP, HEAD_DIM, D).transpose(1, 0, 2, 3).reshape(512, D)
        wo = jnp.concatenate([wo_nsa, w_out[i][512:]], axis=0).astype(bf16)
        weights = (wo, _row(g_mix_post[i]), _row(g_mlp_pre[i]), _row(g_mlp_post[i]), w_up[i].astype(bf16),
                   w_down[i].astype(bf16), w_ple_gate[i].astype(bf16), w_ple[i].astype(bf16))
        x = _post(x, y_nsa, y_rwkv, p[i], weights, tm=POST_TM)
    return x
```

```python
import functools

import jax
import jax.numpy as jnp
import numpy as np
from jax import lax
from jax.experimental import pallas as pl
from jax.experimental.pallas import tpu as pltpu

f32 = jnp.float32
bf16 = jnp.bfloat16

D_MODEL = 1024
HEAD_DIM = 64
NSA_HEADS = 8
NSA_KV_HEADS = 2
NSA_GROUP = NSA_HEADS // NSA_KV_HEADS
CMP_LEN = 32
CMP_STRIDE = 16
CMP_HIDDEN = 2 * HEAD_DIM
SEL_BLOCK = 64
SEL_TOPK = 16
WINDOW = 512
RWKV_WIDTH = 512
RWKV_COLS = 1792
DECAY_LORA = 64
ICLR_LORA = 64
D_FF = 4 * D_MODEL
PLE_DIM = 256
NORM_EPS = 1e-6
GN_EPS = 64e-5
NEG_INF = -1e30
FORCE_SCORE = 1e4

PAIR = 2 * HEAD_DIM
CHUNK = 64
SCAN_GROUP = 4

INPROJ_TM = 512
NSA_TQ = 256
NSA_TK = 512
RWKV_TT = CHUNK * SCAN_GROUP
POST_TM = 512
VMEM_LIMIT = 56 * 1024 * 1024

_Q0, _KV0, _GATE0, _RW0, _WCOLS = 0, 512, 1280, 1408, 3200


def _dot(a, b):
    return jnp.dot(a.astype(bf16), b.astype(bf16), preferred_element_type=f32)


def _dot_nt(a, b):
    return lax.dot_general(a.astype(bf16), b.astype(bf16), (((1,), (1,)), ((), ())), preferred_element_type=f32)


def _dot_tn(a, b):
    return lax.dot_general(a.astype(bf16), b.astype(bf16), (((0,), (0,)), ((), ())), preferred_element_type=f32)


def _split3_dot(w, x):
    hi = x.astype(bf16)
    r1 = x - hi.astype(f32)
    mid = r1.astype(bf16)
    lo = (r1 - mid.astype(f32)).astype(bf16)
    return (jnp.dot(w, hi, preferred_element_type=f32) + jnp.dot(w, mid, preferred_element_type=f32)
            + jnp.dot(w, lo, preferred_element_type=f32))


def _rms(x, g):
    ms = jnp.mean(x * x, axis=-1, keepdims=True)
    return x * lax.rsqrt(ms + NORM_EPS) * g


def _sigmoid(x):
    return 1.0 / (1.0 + jnp.exp(-x))


def _inproj_body(x_ref, g_ref, w_ref, gb_ref, mu_ref, w0_ref, wl_ref, a0_ref, al_ref, gl_ref, kkw_ref, ka_ref,
                 rk_ref, bd_ref, q_ref, kc_ref, vc_ref, ks_ref, vs_ref, kw_ref, vw_ref, gate_ref,
                 r_ref, lw_ref, k_ref, v_ref, kk_ref, b_ref, g_out_ref, bv_ref, zlast_scr):
    @pl.when(pl.program_id(1) == 0)
    def _():
        zlast_scr[...] = jnp.zeros(zlast_scr.shape, f32)

    h = _rms(x_ref[0], g_ref[...]).astype(bf16)
    z = jnp.dot(h, w_ref[:, _RW0:_WCOLS], preferred_element_type=f32)
    tm = z.shape[0]
    prev_row = zlast_scr[7:8, :]
    zlast_scr[...] = z[tm - 8:tm, :]
    q = jnp.dot(h, w_ref[:, _Q0:_KV0], preferred_element_type=f32) * (HEAD_DIM ** -0.5)
    first = lax.broadcasted_iota(jnp.int32, (1, PAIR), 1) < HEAD_DIM
    for hd in range(NSA_HEADS):
        two = q[:, (hd // 2) * PAIR:(hd // 2 + 1) * PAIR]
        if (hd % 2) != (hd // NSA_GROUP):
            two = pltpu.roll(two, HEAD_DIM, axis=1)
        keep = first if hd < NSA_GROUP else jnp.logical_not(first)
        q_ref[0, hd] = jnp.where(keep, two, 0.0).astype(bf16)
    kv = jnp.dot(h, w_ref[:, _KV0:_GATE0], preferred_element_type=f32)
    kc_ref[0] = kv[:, 0:128]
    vc_ref[0] = kv[:, 128:256]
    ks_ref[0] = kv[:, 256:384].astype(bf16)
    vs_ref[0] = kv[:, 384:512].astype(bf16)
    kw_ref[0] = kv[:, 512:640].astype(bf16)
    vw_ref[0] = kv[:, 640:768].astype(bf16)
    gl = jnp.dot(h, w_ref[:, _GATE0:_RW0], preferred_element_type=f32)
    gate_ref[0] = _sigmoid(gl + gb_ref[...])

    row = lax.broadcasted_iota(jnp.int32, (tm, 1), 0)
    z_prev = jnp.where(row == 0, prev_row, pltpu.roll(z, 1, axis=0))
    zs = z + (z_prev - z) * mu_ref[...]
    r = zs[:, 0:512]
    k = zs[:, 512:1024]
    v = zs[:, 1024:1536]
    lora = zs[:, 1536:1664]
    gd = zs[:, 1664:1792]
    wlog = w0_ref[...] + _dot(jnp.tanh(lora), wl_ref[...])
    sp = jnp.maximum(-wlog, 0.0) + jnp.log(1.0 + jnp.exp(-jnp.abs(wlog)))
    a = _sigmoid(a0_ref[...] + _dot(lora, al_ref[...]))
    kk = k * kkw_ref[...]
    kk = kk * lax.rsqrt(jnp.maximum(_dot(kk * kk, bd_ref[...]), 1e-24))
    k2 = k * (1.0 + (a - 1.0) * ka_ref[...])
    r_ref[0] = r
    lw_ref[0] = -jnp.exp(-sp - 0.5)
    k_ref[0] = k2
    v_ref[0] = v
    kk_ref[0] = kk
    b_ref[0] = kk * a
    g_out_ref[0] = _dot(_sigmoid(gd), gl_ref[...])
    bv_ref[0] = _dot(r * k2 * rk_ref[...], bd_ref[...]) * v


def _inproj(x, g, wcat, gbias, rwkv_params, tm):
    B, S, D = x.shape
    tok = lambda w: pl.BlockSpec((1, tm, w), lambda b, i: (b, i, 0))
    const = lambda a: pl.BlockSpec(a.shape, lambda b, i: (0,) * a.ndim)
    consts = (g, wcat, gbias) + tuple(rwkv_params)
    return pl.pallas_call(
        _inproj_body,
        grid=(B, S // tm),
        in_specs=[tok(D)] + [const(a) for a in consts],
        out_specs=[pl.BlockSpec((1, NSA_HEADS, tm, PAIR), lambda b, i: (b, 0, i, 0))]
        + [tok(128)] * 7 + [tok(RWKV_WIDTH)] * 8,
        out_shape=[jax.ShapeDtypeStruct((B, NSA_HEADS, S, PAIR), bf16),
                   jax.ShapeDtypeStruct((B, S, 128), f32), jax.ShapeDtypeStruct((B, S, 128), f32)]
        + [jax.ShapeDtypeStruct((B, S, 128), bf16)] * 4
        + [jax.ShapeDtypeStruct((B, S, 128), f32)] + [jax.ShapeDtypeStruct((B, S, RWKV_WIDTH), f32)] * 8,
        scratch_shapes=[pltpu.VMEM((8, RWKV_COLS), f32)],
        compiler_params=pltpu.CompilerParams(
            dimension_semantics=("arbitrary", "arbitrary"), vmem_limit_bytes=VMEM_LIMIT),
        name="inproj",
    )(x, *consts)


def _gelu_tanh(x):
    return x * (0.5 * (1.0 + jnp.tanh(np.sqrt(2.0 / np.pi) * (x + 0.044715 * (x * x * x)))))


def _compress_one(x_ref, pe_ref, w_ref, b1_ref, w2_ref):
    n = x_ref.shape[1] // CMP_STRIDE
    x = jnp.concatenate([x_ref[0, pl.ds(j, n, stride=CMP_STRIDE), :] for j in range(CMP_STRIDE)], axis=1)
    lo = _dot(x + pe_ref[0:1, :], w_ref[0])
    hi = _dot(x + pe_ref[1:2, :], w_ref[1])
    pre = lo + pltpu.roll(hi, n - 1, axis=0) + b1_ref[...]
    return _dot(_gelu_tanh(pre), w2_ref[...])


def _compress_body(xk_ref, xv_ref, pek_ref, wk_ref, bk_ref, w2k_ref, pev_ref, wv_ref, bv_ref, w2v_ref,
                   kc_ref, vc_ref):
    kc_ref[0] = _compress_one(xk_ref, pek_ref, wk_ref, bk_ref, w2k_ref).astype(bf16)
    vc_ref[0] = _compress_one(xv_ref, pev_ref, wv_ref, bv_ref, w2v_ref).astype(bf16)


def _compress(xk, xv, kparams, vparams):
    B, S, W = xk.shape
    NC = S // CMP_STRIDE
    const = lambda a: pl.BlockSpec(a.shape, lambda b: (0,) * a.ndim)
    seq = pl.BlockSpec((1, S, W), lambda b: (b, 0, 0))
    out = pl.BlockSpec((1, NC, PAIR), lambda b: (b, 0, 0))
    return pl.pallas_call(
        _compress_body,
        grid=(B,),
        in_specs=[seq, seq] + [const(a) for a in kparams] + [const(a) for a in vparams],
        out_specs=[out, out],
        out_shape=[jax.ShapeDtypeStruct((B, NC, PAIR), bf16)] * 2,
        compiler_params=pltpu.CompilerParams(dimension_semantics=("arbitrary",), vmem_limit_bytes=VMEM_LIMIT),
        name="compress",
    )(xk, xv, *kparams, *vparams)


def _alibi_key_columns(pos, ns, onehot):
    a = np.zeros((pos.shape[0], PAIR), np.float32)
    if onehot:
        a[np.arange(pos.shape[0]), pos // SEL_BLOCK] = 1.0
    a[:, ns] = -1.0
    a[:, ns + 1] = -1.0
    a[:, ns + 2] = pos // SEL_BLOCK
    a[:, ns + 3] = pos % SEL_BLOCK
    return jnp.asarray(a, dtype=bf16)


def _nsa_body(q_ref, kc_ref, vc_ref, ks_ref, vs_ref, kw_ref, vw_ref, gate_ref, selT_ref, auxk_ref, auxc_ref,
              o_ref, m_scr, acc_scr, sa_scr, sb_scr, rank_scr, p_scr, al_scr, *, tq, tk, seq):
    R, G, H = NSA_GROUP, NSA_KV_HEADS, NSA_HEADS
    t0 = pl.program_id(1) * tq
    nc = seq // CMP_STRIDE
    ns = seq // SEL_BLOCK
    gm = R * tq
    tok = (t0 + lax.broadcasted_iota(jnp.int32, (tq, 1), 0)).astype(f32)
    lane = lax.broadcasted_iota(jnp.int32, (1, PAIR), 1)
    first = lane < HEAD_DIM

    tl = t0 + lax.broadcasted_iota(jnp.int32, (8, tq), 1)
    rid = lax.broadcasted_iota(jnp.int32, (8, tq), 0)
    alibi_rows = jnp.where(rid == 0, ((tl // SEL_BLOCK) * SEL_BLOCK).astype(f32),
                           jnp.where(rid == 1, (tl % SEL_BLOCK).astype(f32),
                                     jnp.where(rid == 2, float(SEL_BLOCK), jnp.where(rid == 3, 1.0, 0.0))))
    aux0 = jnp.concatenate([jnp.zeros((ns, tq), f32), alibi_rows, jnp.zeros((PAIR - ns - 8, tq), f32)], axis=0).T

    def augment(q8, aux_by_group):
        aux8 = jnp.concatenate([aux_by_group[h // R] * jnp.where(lane < ns, 1.0, 2.0 ** (-(h + 1)))
                                for h in range(H)], axis=0)
        return jnp.concatenate([q8, aux8.astype(bf16)], axis=1)

    def per_head(a):
        return a.reshape(H, tq, a.shape[-1])

    def with_ones(v):
        one = jnp.ones((), v.dtype)
        return jnp.where(first, v, one), jnp.where(first, one, v)

    def weighted_values(p, v):
        v0, v1 = with_ones(v)
        return jnp.concatenate([jnp.dot(p[:gm], v0, preferred_element_type=f32),
                                jnp.dot(p[gm:], v1, preferred_element_type=f32)], axis=0)

    def split_sum(acc):
        top, bot = acc[:gm], acc[gm:]
        return jnp.concatenate([top * (1.0 / top[:, HEAD_DIM:HEAD_DIM + 1]), bot * (1.0 / bot[:, 0:1])], axis=0)

    q8 = q_ref[0].reshape(H * tq, PAIR)
    qa = augment(q8, [aux0, aux0])

    kc_aug = jnp.concatenate([kc_ref[0], auxc_ref[...]], axis=1)
    sc = per_head(_dot_nt(qa, kc_aug))
    w0 = pl.multiple_of(jnp.maximum(t0 - WINDOW, 0), tq)
    wrows = pl.ds(w0, WINDOW + tq)
    kw_aug = jnp.concatenate([kw_ref[0, wrows, :], auxk_ref[wrows, :]], axis=1)
    sw = per_head(_dot_nt(qa, kw_aug))

    cend = (lax.broadcasted_iota(jnp.int32, (1, nc), 1) * CMP_STRIDE + (CMP_LEN - 1)).astype(f32)
    vis_c = tok >= cend
    any_c = (tok >= float(CMP_LEN - 1)).astype(f32)
    s = jnp.where(vis_c[None], sc, NEG_INF)
    e = jnp.exp(s - jnp.max(s, axis=-1, keepdims=True))
    p_cmp = e * (any_c[None] / jnp.sum(e, axis=-1, keepdims=True))
    o_cmp = _dot(p_cmp.reshape(H * tq, nc), vc_ref[0])

    jrow = lax.broadcasted_iota(jnp.int32, (ns, tq), 0)
    cur = (t0 + lax.broadcasted_iota(jnp.int32, (ns, tq), 1)) // SEL_BLOCK
    forced = (jrow == 0) | (jrow == cur) | (jrow == cur - 1)
    keys_t = []
    for g in range(G):
        psum = p_cmp[R * g]
        for r in range(1, R):
            psum = psum + p_cmp[R * g + r]
        imp_t = lax.dot_general(selT_ref[...], psum.astype(bf16), (((1,), (1,)), ((), ())),
                                preferred_element_type=f32)
        p_lo = (psum - psum.astype(bf16).astype(f32)).astype(bf16)
        imp_t = imp_t + lax.dot_general(selT_ref[...], p_lo, (((1,), (1,)), ((), ())),
                                        preferred_element_type=f32)
        score = jnp.where(forced, FORCE_SCORE, jnp.where(jrow <= cur, imp_t, -1.0))
        keys_t.append(jnp.where(score < 0.0, -1, pltpu.bitcast(score, jnp.int32)))
        rank_scr[g] = keys_t[g]

    def rank_rows(i8, cnts):
        base = pl.multiple_of(i8 * 8, 8)
        out = []
        for g in range(G):
            rows8 = rank_scr[g, pl.ds(base, 8), :]
            cnt = cnts[g]
            for k in range(8):
                si = rows8[k:k + 1, :]
                ahead = jnp.where(jrow > base + k, si + 1, si) > keys_t[g]
                cnt = cnt + jnp.where(ahead, 1.0, 0.0)
            out.append(cnt)
        return tuple(out)

    n_rows8 = jnp.minimum((t0 + tq - 1) // SEL_BLOCK // 8 + 1, ns // 8)
    cnts = lax.fori_loop(0, n_rows8, rank_rows, tuple(jnp.zeros((ns, tq), f32) for _ in range(G)))
    aux_sel = []
    for g in range(G):
        bias_t = jnp.where(cnts[g] < float(SEL_TOPK), 0.0, NEG_INF)
        aux_sel.append(aux0 + jnp.concatenate([bias_t, jnp.zeros((PAIR - ns, tq), f32)], axis=0).T)
    qs = augment(q8, aux_sel)

    def key_rows(kt):
        return pl.ds(pl.multiple_of(kt * tk, tk), tk)

    def scores(kt, dst):
        rows_ = key_rows(kt)
        dst[...] = _dot_nt(qs, jnp.concatenate([ks_ref[0, rows_, :], auxk_ref[rows_, :]], axis=1))

    def attend(src, kt, diag):
        if diag:
            s8 = src[...]
            kpos = (kt * tk + lax.broadcasted_iota(jnp.int32, (1, tk), 1)).astype(f32)
            s8 = jnp.where((tok >= kpos)[None], per_head(s8), NEG_INF).reshape(H * tq, tk)
            m_new = jnp.max(s8, axis=-1, keepdims=True)
            p = jnp.exp(s8 - m_new).astype(bf16)
            acc_scr[...] = weighted_values(p, vs_ref[0, key_rows(kt), :])
            m_scr[...] = jnp.broadcast_to(m_new, m_scr.shape)
            return
        for h in range(H):
            rows_h = slice(h * tq, (h + 1) * tq)
            s1 = src[rows_h, :]
            m_prev = m_scr[rows_h, :][:, 0:1]
            m_new = jnp.maximum(m_prev, jnp.max(s1, axis=-1, keepdims=True))
            al_scr[rows_h, :] = jnp.broadcast_to(jnp.exp(m_prev - m_new), (tq, PAIR))
            p_scr[rows_h, :] = jnp.exp(s1 - m_new).astype(bf16)
            m_scr[rows_h, :] = jnp.broadcast_to(m_new, (tq, PAIR))
        acc_scr[...] = al_scr[...] * acc_scr[...] + weighted_values(p_scr[...], vs_ref[0, key_rows(kt), :])

    n_full = t0 // tk
    scores(n_full, sa_scr)

    kpos = w0 + lax.broadcasted_iota(jnp.int32, (1, WINDOW + tq), 1)
    dist_w = tok - kpos.astype(f32)
    vis_w = jnp.abs(dist_w - (WINDOW - 1) / 2.0) < WINDOW / 2.0
    s = jnp.where(vis_w[None], sw, NEG_INF)
    e = jnp.exp(s - jnp.max(s, axis=-1, keepdims=True)).astype(bf16).reshape(H * tq, WINDOW + tq)
    acc_win = weighted_values(e, vw_ref[0, wrows, :])

    scores(0, sb_scr)
    attend(sa_scr, n_full, True)

    def two_tiles(j, carry):
        scores(2 * j + 1, sa_scr)
        attend(sb_scr, 2 * j, False)
        scores(2 * j + 2, sb_scr)
        attend(sa_scr, 2 * j + 1, False)
        return carry

    lax.fori_loop(0, n_full // 2, two_tiles, 0)

    @pl.when(n_full % 2 == 1)
    def _():
        attend(sb_scr, n_full - 1, False)

    o_sel = split_sum(acc_scr[...])
    o_win = split_sum(acc_win)

    gate = gate_ref[0]
    o_cmp, o_sel, o_win = per_head(o_cmp), per_head(o_sel), per_head(o_win)
    for r in range(R):
        pair = []
        for g in range(G):
            h = R * g + r
            pair.append(gate[:, 3 * h:3 * h + 1] * o_cmp[h] + gate[:, 3 * h + 1:3 * h + 2] * o_sel[h]
                        + gate[:, 3 * h + 2:3 * h + 3] * o_win[h])
        o_ref[0, :, r * PAIR:(r + 1) * PAIR] = jnp.where(first, pair[0], pair[1])


def _nsa(q, kc, vc, ks, vs, kw, vw, gates, sel_t, tq, tk):
    B, H, S, _ = q.shape
    nc, ns = S // CMP_STRIDE, S // SEL_BLOCK
    aux_k = _alibi_key_columns(np.arange(S), ns, onehot=True)
    aux_c = _alibi_key_columns(np.arange(nc) * CMP_STRIDE + (CMP_LEN - 1), ns, onehot=False)
    full = lambda n: pl.BlockSpec((1, n, PAIR), lambda b, i: (b, 0, 0))
    const = lambda a: pl.BlockSpec(a.shape, lambda b, i: (0, 0))
    body = functools.partial(_nsa_body, tq=tq, tk=tk, seq=S)
    return pl.pallas_call(
        body,
        grid=(B, S // tq),
        in_specs=[pl.BlockSpec((1, H, tq, PAIR), lambda b, i: (b, 0, i, 0)),
                  full(nc), full(nc), full(S), full(S), full(S), full(S),
                  pl.BlockSpec((1, tq, 128), lambda b, i: (b, i, 0)),
                  const(sel_t), const(aux_k), const(aux_c)],
        out_specs=pl.BlockSpec((1, tq, NSA_GROUP * PAIR), lambda b, i: (b, i, 0)),
        out_shape=jax.ShapeDtypeStruct((B, S, NSA_GROUP * PAIR), f32),
        scratch_shapes=[pltpu.VMEM((H * tq, PAIR), f32)] * 2 + [pltpu.VMEM((H * tq, tk), f32)] * 2
        + [pltpu.VMEM((NSA_KV_HEADS, ns, tq), jnp.int32), pltpu.VMEM((H * tq, tk), bf16),
           pltpu.VMEM((H * tq, PAIR), f32)],
        compiler_params=pltpu.CompilerParams(
            dimension_semantics=("arbitrary", "arbitrary"), vmem_limit_bytes=VMEM_LIMIT),
        name="nsa",
    )(q, kc, vc, ks, vs, kw, vw, gates, sel_t, aux_k, aux_c)


def _rwkv_body(r_s, lw_s, k_s, v_s, kk_s, b_s, g_s, bv_s, lnw_ref, lnb_ref, y_ref, s_scr, *, tt):
    C = CHUNK
    n_pairs = RWKV_WIDTH // PAIR

    @pl.when(pl.program_id(1) == 0)
    def _():
        s_scr[...] = jnp.zeros(s_scr.shape, f32)

    ri = lax.broadcasted_iota(jnp.int32, (C, C), 0)
    ci = lax.broadcasted_iota(jnp.int32, (C, C), 1)
    cum_mat = jnp.where(ri >= ci, 1.0, 0.0).astype(bf16)
    ri2 = lax.broadcasted_iota(jnp.int32, (C, 2 * C), 0)
    ci2 = lax.broadcasted_iota(jnp.int32, (C, 2 * C), 1) % C
    tri2_incl = ri2 >= ci2
    tri2_strict = ri2 > ci2
    eye2 = jnp.where(ri2 == ci2, 1.0, 0.0)
    lane = lax.broadcasted_iota(jnp.int32, (1, PAIR), 1)
    first = lane < HEAD_DIM
    blockdiag = (lax.broadcasted_iota(jnp.int32, (PAIR, PAIR), 0) // HEAD_DIM
                 == lax.broadcasted_iota(jnp.int32, (PAIR, PAIR), 1) // HEAD_DIM)
    head_mean = jnp.where(blockdiag, 1.0, 0.0).astype(bf16)

    def swap_heads(x):
        return pltpu.roll(x, HEAD_DIM, axis=1)

    def by_head_rows(x):
        return jnp.concatenate([jnp.where(first, x, 0.0), jnp.where(first, 0.0, x)], axis=0)

    def block_diag(x2):
        zero = jnp.zeros((x2.shape[0], PAIR), x2.dtype)
        return jnp.concatenate([jnp.concatenate([x2[:, :PAIR], zero], axis=1),
                                jnp.concatenate([zero, x2[:, PAIR:]], axis=1)], axis=0)

    def group(gi):
        units = []
        for cc in range(SCAN_GROUP):
            rows = pl.ds((gi * SCAN_GROUP + cc) * C, C)
            r_, lw_, k_, v_, kk_, b_ = (s[0, rows, :] for s in (r_s, lw_s, k_s, v_s, kk_s, b_s))
            cum = _split3_dot(cum_mat, lw_)
            cum_end = cum[C - 1:C, :]
            e_neg = jnp.exp(-cum)
            e_end = jnp.exp(cum_end - cum)
            a_t = -kk_ * jnp.exp(cum - lw_)
            r_t = r_ * jnp.exp(cum)
            b_t = b_ * e_neg
            k_t = k_ * e_neg
            b_h = b_ * e_end
            k_h = k_ * e_end
            w_end = jnp.exp(cum_end)
            for p in range(n_pairs):
                pc = slice(p * PAIR, (p + 1) * PAIR)
                units.append(dict(p=p, rows=rows, pc=pc, a_t=a_t[:, pc],
                                  r_t=r_t[:, pc], b_t=b_t[:, pc], k_t=k_t[:, pc], b_h=b_h[:, pc], k_h=k_h[:, pc],
                                  v=v_[:, pc], w_end=w_end[:, pc]))

        for u in units:
            lhs = jnp.concatenate([u["a_t"], u["r_t"]], axis=0)
            abk = _dot_nt(lhs, jnp.concatenate([by_head_rows(u["b_t"]), by_head_rows(u["k_t"])], axis=0))
            u["ab"], u["ak"] = abk[:, :2 * C], abk[:, 2 * C:]
        for u in units:
            ab, ak = u["ab"], u["ak"]
            u["a"] = jnp.where(tri2_strict, ab[:C], 0.0)
            u["a_rb"] = jnp.where(tri2_incl, ab[C:], 0.0)
            u["akrk"] = jnp.concatenate([jnp.where(tri2_strict, ak[:C], 0.0), jnp.where(tri2_incl, ak[C:], 0.0)],
                                        axis=0)
        for u in units:
            v_sw = swap_heads(u["v"]).astype(bf16)
            x1 = _dot(u["akrk"], block_diag(jnp.concatenate([v_sw, v_sw], axis=1)))
            u["x"] = jnp.concatenate([jnp.where(first, u["a_t"], x1[:C, :PAIR]),
                                      jnp.where(first, x1[:C, PAIR:], u["a_t"])], axis=1)
            u["arkv"] = x1[C:]
            u["t"] = eye2 + u["a"]
            u["pw"] = _dot(u["a"], by_head_rows(u["a"]))
        for j in range(1, 5):
            for u in units:
                both = _dot(jnp.concatenate([u["pw"], u["t"]], axis=0), by_head_rows(u["pw"]))
                u["pw"] = both[:C]
                u["t"] = u["t"] + both[C:]
        for u in units:
            u["t"] = u["t"] + _dot(u["t"], by_head_rows(u["pw"]))
        for u in units:
            u["x"] = _dot(u["t"], block_diag(u["x"].astype(bf16)))
        for u in units:
            u["ex"] = _dot(u["a_rb"], block_diag(u["x"].astype(bf16)))
        for u in units:
            x0, x1 = u["x"][:, :PAIR], u["x"][:, PAIR:]
            ex0, ex1 = u["ex"][:, :PAIR], u["ex"][:, PAIR:]
            ua = jnp.where(first, x0, x1)
            w2 = swap_heads(jnp.where(first, x1, x0))
            u["rq"] = u["r_t"] + jnp.where(first, ex0, ex1)
            u["yin"] = swap_heads(jnp.where(first, u["arkv"][:, PAIR:] + ex1, u["arkv"][:, :PAIR] + ex0))
            u["gmat"] = jnp.where(blockdiag, _dot_tn(ua, u["b_h"]), 0.0)
            u["qmat"] = jnp.where(blockdiag, _dot_tn(jnp.concatenate([w2, u["v"]], axis=0),
                                                     jnp.concatenate([u["b_h"], u["k_h"]], axis=0)), 0.0)
        state = [s_scr[p] for p in range(n_pairs)]
        for u in units:
            s0 = state[u["p"]]
            u["y"] = _dot_nt(u["rq"], s0) + u["yin"]
            state[u["p"]] = s0 * u["w_end"] + _dot(s0, u["gmat"]) + u["qmat"]
        for p in range(n_pairs):
            s_scr[p] = state[p]
        y_all = jnp.concatenate([u["y"] for u in units], axis=0)
        d_all = y_all - jnp.dot(y_all.astype(bf16), head_mean, preferred_element_type=f32) * (1.0 / HEAD_DIM)
        var_all = jnp.dot((d_all * d_all).astype(bf16), head_mean, preferred_element_type=f32) * (1.0 / HEAD_DIM)
        yn_all = d_all * lax.rsqrt(var_all + GN_EPS)
        for n, u in enumerate(units):
            yn = yn_all[n * C:(n + 1) * C] * lnw_ref[:, u["pc"]] + lnb_ref[:, u["pc"]]
            y_ref[0, u["rows"], u["pc"]] = (yn + bv_s[0, u["rows"], u["pc"]]) * g_s[0, u["rows"], u["pc"]]

    for gi in range(tt // (C * SCAN_GROUP)):
        group(gi)


def _rwkv(r, lw, k, v, kk, b, g, bv, lnw, lnb, tt):
    B, S, W = r.shape
    tok = pl.BlockSpec((1, tt, W), lambda bb, i: (bb, i, 0))
    const = pl.BlockSpec((1, W), lambda bb, i: (0, 0))
    return pl.pallas_call(
        functools.partial(_rwkv_body, tt=tt),
        grid=(B, S // tt),
        in_specs=[tok] * 8 + [const, const],
        out_specs=tok,
        out_shape=jax.ShapeDtypeStruct((B, S, W), f32),
        scratch_shapes=[pltpu.VMEM((W // PAIR, PAIR, PAIR), f32)],
        compiler_params=pltpu.CompilerParams(
            dimension_semantics=("arbitrary", "arbitrary"), vmem_limit_bytes=VMEM_LIMIT),
        name="rwkv",
    )(r, lw, k, v, kk, b, g, bv, lnw, lnb)


def _post_body(x_ref, yn_ref, yr_ref, p_ref, wo_ref, gpost_ref, gpre_ref, gmlp_ref, wup_ref, wdn_ref,
               wpg_ref, wple_ref, o_ref):
    y = jnp.concatenate([yn_ref[0], yr_ref[0]], axis=1).astype(bf16)
    mix = jnp.dot(y, wo_ref[...], preferred_element_type=f32)
    x1 = x_ref[0] + _rms(mix, gpost_ref[...])
    h = _rms(x1, gpre_ref[...]).astype(bf16)
    acc = None
    for c in range(D_FF // D_MODEL):
        cs = slice(c * D_MODEL, (c + 1) * D_MODEL)
        u = jnp.dot(h, wup_ref[:, cs], preferred_element_type=f32)
        u = jnp.square(jnp.maximum(u, 0.0)).astype(bf16)
        part = jnp.dot(u, wdn_ref[cs, :], preferred_element_type=f32)
        acc = part if acc is None else acc + part
    x2 = x1 + _rms(acc, gmlp_ref[...])
    gate = _sigmoid(jnp.dot(x2.astype(bf16), wpg_ref[...], preferred_element_type=f32))
    o_ref[0] = x2 + gate * jnp.dot(p_ref[0].astype(bf16), wple_ref[...], preferred_element_type=f32)


def _post(x, yn, yr, p, weights, tm):
    B, S, D = x.shape
    tok = lambda w: pl.BlockSpec((1, tm, w), lambda b, i: (b, i, 0))
    const = lambda a: pl.BlockSpec(a.shape, lambda b, i: (0,) * a.ndim, pipeline_mode=pl.Buffered(1))
    return pl.pallas_call(
        _post_body,
        grid=(B, S // tm),
        in_specs=[tok(D), tok(512), tok(512), tok(PLE_DIM)] + [const(a) for a in weights],
        out_specs=tok(D),
        out_shape=jax.ShapeDtypeStruct((B, S, D), f32),
        compiler_params=pltpu.CompilerParams(
            dimension_semantics=("arbitrary", "arbitrary"), vmem_limit_bytes=VMEM_LIMIT),
        name="post",
    )(x, yn, yr, p, *weights)


def _pack_inproj(w_in, gate_bias):
    wg = jnp.pad(w_in[:, 1280:1304], ((0, 0), (0, 128 - 24)))
    wcat = jnp.concatenate([w_in[:, :1280], wg, w_in[:, 1304:]], axis=1).astype(bf16)
    return wcat, jnp.pad(gate_bias, (0, 128 - 24)).reshape(1, 128)


def _pack_compress(pe, w1, b1, w2):
    eye2 = jnp.eye(NSA_KV_HEADS, dtype=f32)
    w1r = w1.reshape(CMP_LEN, HEAD_DIM, CMP_HIDDEN)
    halves = []
    for part in (w1r[:CMP_STRIDE], w1r[CMP_STRIDE:]):
        halves.append(jnp.einsum("jdc,gh->jgdhc", part, eye2).reshape(CMP_STRIDE * PAIR, 2 * CMP_HIDDEN))
    w = jnp.stack(halves).astype(bf16)
    per = jnp.broadcast_to(pe.reshape(2, CMP_STRIDE, 1, HEAD_DIM), (2, CMP_STRIDE, NSA_KV_HEADS, HEAD_DIM))
    per = per.reshape(2, CMP_STRIDE * PAIR)
    b1p = jnp.tile(b1, NSA_KV_HEADS).reshape(1, 2 * CMP_HIDDEN)
    w2p = jnp.einsum("cd,gh->gchd", w2, eye2).reshape(2 * CMP_HIDDEN, PAIR).astype(bf16)
    return per, w, b1p, w2p


def _sel_map_t(seq):
    nc, ns = seq // CMP_STRIDE, seq // SEL_BLOCK
    c0 = np.arange(nc) * CMP_STRIDE
    s0 = np.arange(ns) * SEL_BLOCK
    ov = (np.minimum(c0[:, None] + CMP_LEN - 1, s0[None, :] + SEL_BLOCK - 1)
          - np.maximum(c0[:, None], s0[None, :]) + 1)
    m = np.clip(ov, 0, None).astype(np.float32) / CMP_STRIDE
    m[nc - 1] = 0.0
    return jnp.asarray(m.T, dtype=bf16)


def _row(a):
    return a.reshape(1, -1)


def _mixers(x, g_mix_pre, w_in, nsa_gate_bias, cmp_k, cmp_v, shift_mu, w0, w_lora_up, a0, a_lora_up, g_lora_up,
            k_k, k_a, r_k, lnx_w, lnx_b):
    S = x.shape[1]
    bd = jnp.asarray(np.kron(np.eye(RWKV_WIDTH // HEAD_DIM), np.ones((HEAD_DIM, HEAD_DIM))), dtype=bf16)
    wcat, gbias = _pack_inproj(w_in, nsa_gate_bias)
    wl = jnp.concatenate([w_lora_up, jnp.zeros((ICLR_LORA, RWKV_WIDTH), f32)], axis=0).astype(bf16)
    al = jnp.concatenate([jnp.zeros((DECAY_LORA, RWKV_WIDTH), f32), a_lora_up], axis=0).astype(bf16)
    rwkv_params = (_row(shift_mu), _row(w0), wl, _row(a0), al, g_lora_up.astype(bf16),
                   _row(k_k), _row(k_a), _row(r_k), bd)
    q, kc, vc, ks, vs, kw, vw, gates, r, lw, k2, v, kk, b, g, bv = _inproj(
        x, _row(g_mix_pre), wcat, gbias, rwkv_params, tm=INPROJ_TM)
    kcmp, vcmp = _compress(kc, vc, _pack_compress(*cmp_k), _pack_compress(*cmp_v))
    y_nsa = _nsa(q, kcmp, vcmp, ks, vs, kw, vw, gates, _sel_map_t(S), tq=NSA_TQ, tk=NSA_TK)
    y_rwkv = _rwkv(r, lw, k2, v, kk, b, g, bv, _row(lnx_w), _row(lnx_b), tt=RWKV_TT)
    return r, y_nsa, y_rwkv


def kernel(x, p, g_mix_pre, g_mix_post, g_mlp_pre, g_mlp_post, w_in, nsa_gate_bias, cmp_pe_k, cmp_k_w1, cmp_k_b1, cmp_k_w2, cmp_pe_v, cmp_v_w1, cmp_v_b1, cmp_v_w2, shift_mu, w0, w_lora_up, a0, a_lora_up, g_lora_up, k_k, k_a, r_k, lnx_w, lnx_b, w_out, w_up, w_down, w_ple, w_ple_gate):
    D = x.shape[-1]
    for i in range(p.shape[0]):
        _, y_nsa, y_rwkv = _mixers(
            x, g_mix_pre[i], w_in[i], nsa_gate_bias[i],
            (cmp_pe_k[i], cmp_k_w1[i], cmp_k_b1[i], cmp_k_w2[i]), (cmp_pe_v[i], cmp_v_w1[i], cmp_v_b1[i], cmp_v_w2[i]),
            shift_mu[i], w0[i], w_lora_up[i], a0[i], a_lora_up[i], g_lora_up[i], k_k[i], k_a[i], r_k[i],
            lnx_w[i], lnx_b[i])
        wo_nsa = w_out[i][:512].reshape(NSA_KV_HEADS, NSA_GROUP, HEAD_DIM, D).transpose(1, 0, 2, 3).reshape(512, D)
        wo = jnp.concatenate([wo_nsa, w_out[i][512:]], axis=0).astype(bf16)
        weights = (wo, _row(g_mix_post[i]), _row(g_mlp_pre[i]), _row(g_mlp_post[i]), w_up[i].astype(bf16),
                   w_down[i].astype(bf16), w_ple_gate[i].astype(bf16), w_ple[i].astype(bf16))
        x = _post(x, y_nsa, y_rwkv, p[i], weights, tm=POST_TM)
    return x
```
